```python
import jax, jax.numpy as jnp
from jax import lax
import numpy as np

D_MODEL = 1024
BATCH = 4
SEQ = 4096
DEPTH = 1

GRID_W = 64
CTX_LEN = 256
CONV_CH = 512
CONV_GROUPS = 8
LRU_WIDTH = 512
LRU_HEADS = 8
LRU_HEAD_DIM = LRU_WIDTH // LRU_HEADS
MIX_WIDTH = CONV_CH + LRU_WIDTH
IN_COLS = 2 * CONV_CH + 2 * LRU_WIDTH
CONV_TAPS = 31
LRU_CONV_TAPS = 4
LRU_C = 8.0
N_GROUPS = 4
EXPERTS_PER_GROUP = 8
N_EXPERTS = N_GROUPS * EXPERTS_PER_GROUP
TOP_K = 2
D_EXPERT = 1024
MOE_BLOCK = 256
EPS = 1e-6

kernel_name = "hybrid_conformer_rglru_hmoe_prefix_dit"


def rms_norm(x, g):
    xf = x.astype(jnp.float32)
    y = xf * lax.rsqrt(jnp.mean(xf * xf, axis=-1, keepdims=True) + EPS)
    return (y * g.astype(jnp.float32)).astype(x.dtype)


def layer_norm(x, g, b):
    xf = x.astype(jnp.float32)
    mu = jnp.mean(xf, axis=-1, keepdims=True)
    var = jnp.mean(jnp.square(xf - mu), axis=-1, keepdims=True)
    y = (xf - mu) * lax.rsqrt(var + EPS)
    return (y * g.astype(jnp.float32) + b.astype(jnp.float32)).astype(x.dtype)


def modulate(h, shift, scale):
    return h * (1.0 + scale) + shift


def dwconv(u, w, b, pad):
    C = u.shape[-1]
    y = lax.conv_general_dilated(u, w.astype(u.dtype)[:, None, :], window_strides=(1,),
                                 padding=[pad], dimension_numbers=("NWC", "WIO", "NWC"),
                                 feature_group_count=C)
    return y + b.astype(u.dtype)


def conformer_conv(val, gate, dw, b, ln_g, ln_b):
    u = val * jax.nn.sigmoid(gate)
    u = dwconv(u, dw, b, (CONV_TAPS // 2, CONV_TAPS // 2))
    return jax.nn.silu(layer_norm(u, ln_g, ln_b))


def rglru_coeffs(u, wa, ba, wx, bx, lam):
    B_, L, C = u.shape
    uf = u.astype(jnp.float32)
    uh = uf.reshape(B_, L, LRU_HEADS, LRU_HEAD_DIM)
    r = jax.nn.sigmoid(jnp.einsum("blhi,hij->blhj", uh, wa.astype(jnp.float32)).reshape(B_, L, C) + ba.astype(jnp.float32))
    i = jax.nn.sigmoid(jnp.einsum("blhi,hij->blhj", uh, wx.astype(jnp.float32)).reshape(B_, L, C) + bx.astype(jnp.float32))
    log_a = -LRU_C * r * jax.nn.softplus(-lam.astype(jnp.float32))
    a = jnp.exp(log_a)
    mult = jnp.sqrt(jnp.maximum(-jnp.expm1(2.0 * log_a), 1e-12))
    return a, mult * i * uf


def linear_scan(a, b, h0):
    b = b.at[:, 0].add(a[:, 0] * h0)

    def combine(left, right):
        return (left[0] * right[0], right[0] * left[1] + right[1])

    _, h = lax.associative_scan(combine, (a, b), axis=1)
    return h


def directional_scan(a, b, h0, reverse):
    if reverse:
        return jnp.flip(linear_scan(jnp.flip(a, 1), jnp.flip(b, 1), h0), 1)
    return linear_scan(a, b, h0)


def swiglu_expert(xb, w1, w3, w2):
    return (jax.nn.silu(xb @ w1) * (xb @ w3)) @ w2


def hier_moe(h, wg, bg, we, be, w1, w3, w2):
    T, D = h.shape
    hf = h.astype(jnp.float32)
    pg = jax.nn.softmax(hf @ wg.astype(jnp.float32) + bg.astype(jnp.float32), axis=-1)
    p_grp, grp = lax.top_k(pg, 1)
    p_grp, grp = p_grp[:, 0], grp[:, 0]
    le = jnp.einsum("td,dge->tge", hf, we.astype(jnp.float32)) + be.astype(jnp.float32)
    le = jnp.take_along_axis(le, grp[:, None, None], axis=1)[:, 0]
    top_p, top_i = lax.top_k(jax.nn.softmax(le, axis=-1), TOP_K)
    gates = p_grp[:, None] * top_p / jnp.sum(top_p, axis=-1, keepdims=True)
    eid = (grp[:, None] * EXPERTS_PER_GROUP + top_i).reshape(-1)
    tok = jnp.repeat(jnp.arange(T, dtype=jnp.int32), TOP_K)
    gflat = gates.reshape(-1)
    order = jnp.argsort(eid)
    e_sorted, tok_sorted, g_sorted = eid[order], tok[order], gflat[order]
    counts = jnp.bincount(eid, length=N_EXPERTS)
    padded = ((counts + MOE_BLOCK - 1) // MOE_BLOCK) * MOE_BLOCK
    start = jnp.cumsum(counts) - counts
    pend = jnp.cumsum(padded)
    pstart = pend - padded
    dest = pstart[e_sorted] + (jnp.arange(T * TOP_K) - start[e_sorted])
    n_rows = -(-(T * TOP_K) // MOE_BLOCK) * MOE_BLOCK + N_EXPERTS * MOE_BLOCK
    n_blocks = n_rows // MOE_BLOCK
    tok_buf = jnp.zeros((n_rows,), jnp.int32).at[dest].set(tok_sorted)
    gate_buf = jnp.zeros((n_rows,), h.dtype).at[dest].set(g_sorted.astype(h.dtype))
    blk_e = jnp.minimum(jnp.searchsorted(pend, jnp.arange(n_blocks) * MOE_BLOCK, side="right"), N_EXPERTS - 1)
    xb = h[tok_buf].reshape(n_blocks, MOE_BLOCK, D)

    def run(args):
        xblk, e = args
        return swiglu_expert(xblk, w1[e], w3[e], w2[e])

    yb = lax.map(run, (xb, blk_e)).reshape(n_rows, D)
    return jnp.zeros_like(h).at[tok_buf].add(yb * gate_buf[:, None])


def setup_inputs(seed: int = 0) -> dict:
    key = jax.random.key(seed)
    ks = jax.random.split(key, 32)
    f32 = jnp.float32

    def n(k, shape, s):
        return jax.random.normal(k, shape, f32) * s

    L = DEPTH
    u = jax.random.uniform(ks[20], (L, 2, LRU_WIDTH), f32, 0.9, 0.999)
    a0 = u ** (1.0 / LRU_C)
    lam = jnp.log(a0) - jnp.log1p(-a0)
    return {
        "x": n(ks[0], (BATCH, SEQ, D_MODEL), 1.0),
        "c": n(ks[1], (BATCH, D_MODEL), 1.0),
        "ctx": n(ks[2], (BATCH, CTX_LEN, D_MODEL), 1.0),
        "c_ctx": n(ks[3], (D_MODEL,), 1.0),
        "w_ada": n(ks[4], (L, D_MODEL, 6 * D_MODEL), 0.5 * D_MODEL ** -0.5),
        "b_ada": n(ks[5], (L, 6 * D_MODEL), 0.02),
        "norm1_g": 1.0 + n(ks[6], (L, D_MODEL), 0.02),
        "norm2_g": 1.0 + n(ks[7], (L, D_MODEL), 0.02),
        "w_in": n(ks[8], (L, D_MODEL, IN_COLS), D_MODEL ** -0.5),
        "conv_dw": n(ks[9], (L, CONV_TAPS, CONV_CH), CONV_TAPS ** -0.5),
        "conv_b": n(ks[10], (L, CONV_CH), 0.02),
        "conv_ln_g": 1.0 + n(ks[11], (L, CONV_CH), 0.02),
        "conv_ln_b": n(ks[12], (L, CONV_CH), 0.02),
        "lru_conv_w": n(ks[13], (L, LRU_CONV_TAPS, LRU_WIDTH), LRU_CONV_TAPS ** -0.5),
        "lru_conv_b": n(ks[14], (L, LRU_WIDTH), 0.02),
        "lru_wa": n(ks[15], (L, 2, LRU_HEADS, LRU_HEAD_DIM, LRU_HEAD_DIM), LRU_HEAD_DIM ** -0.5),
        "lru_ba": n(ks[16], (L, 2, LRU_WIDTH), 0.02),
        "lru_wx": n(ks[17], (L, 2, LRU_HEADS, LRU_HEAD_DIM, LRU_HEAD_DIM), LRU_HEAD_DIM ** -0.5),
        "lru_bx": n(ks[18], (L, 2, LRU_WIDTH), 0.02),
        "lru_lam": lam,
        "w_out": n(ks[19], (L, MIX_WIDTH, D_MODEL), MIX_WIDTH ** -0.5),
        "router_wg": n(ks[21], (L, D_MODEL, N_GROUPS), D_MODEL ** -0.5),
        "router_bg": n(ks[22], (L, N_GROUPS), 0.01),
        "router_we": n(ks[23], (L, D_MODEL, N_GROUPS, EXPERTS_PER_GROUP), D_MODEL ** -0.5),
        "router_be": n(ks[24], (L, N_GROUPS, EXPERTS_PER_GROUP), 0.01),
        "w1": n(ks[25], (L, N_EXPERTS, D_MODEL, D_EXPERT), D_MODEL ** -0.5),
        "w3": n(ks[26], (L, N_EXPERTS, D_MODEL, D_EXPERT), D_MODEL ** -0.5),
        "w2": n(ks[27], (L, N_EXPERTS, D_EXPERT, D_MODEL), D_EXPERT ** -0.5),
        "final_g": 1.0 + n(ks[28], (D_MODEL,), 0.02),
    }


def reference(x, c, ctx, c_ctx, w_ada, b_ada, norm1_g, norm2_g, w_in, conv_dw, conv_b, conv_ln_g, conv_ln_b,
              lru_conv_w, lru_conv_b, lru_wa, lru_ba, lru_wx, lru_bx, lru_lam, w_out,
              router_wg, router_bg, router_we, router_be, w1, w3, w2, final_g):
    B, S, D = x.shape
    rows = S // GRID_W
    xc = ctx
    Cc, Lw = CONV_CH, LRU_WIDTH
    for l in range(DEPTH):
        last = l == DEPTH - 1
        mod_l = (jax.nn.silu(c) @ w_ada[l] + b_ada[l])[:, None, :]
        mod_c = jax.nn.silu(c_ctx) @ w_ada[l] + b_ada[l]
        sh1, sc1, g1, sh2, sc2, g2 = jnp.split(mod_l, 6, axis=-1)
        csh1, csc1, cg1, csh2, csc2, cg2 = jnp.split(mod_c, 6, axis=-1)

        zl = modulate(rms_norm(x, norm1_g[l]), sh1, sc1) @ w_in[l]
        zc = modulate(rms_norm(xc, norm1_g[l]), csh1, csc1) @ w_in[l]

        conv_l = conformer_conv(zl[..., :Cc].reshape(B * rows, GRID_W, Cc),
                                zl[..., Cc:2 * Cc].reshape(B * rows, GRID_W, Cc),
                                conv_dw[l], conv_b[l], conv_ln_g[l], conv_ln_b[l]).reshape(B, S, Cc)

        ul = dwconv(zl[..., 2 * Cc:2 * Cc + Lw], lru_conv_w[l], lru_conv_b[l], (2, 1))
        uc = dwconv(zc[..., 2 * Cc:2 * Cc + Lw], lru_conv_w[l], lru_conv_b[l], (2, 1))
        h_lat = jnp.zeros((B, S, Lw), jnp.float32)
        h_ctx = jnp.zeros((B, uc.shape[1], Lw), jnp.float32)
        for d, reverse in enumerate((False, True)):
            ac, bc = rglru_coeffs(uc, lru_wa[l, d], lru_ba[l, d], lru_wx[l, d], lru_bx[l, d], lru_lam[l, d])
            al, bl = rglru_coeffs(ul, lru_wa[l, d], lru_ba[l, d], lru_wx[l, d], lru_bx[l, d], lru_lam[l, d])
            hc_d = directional_scan(ac, bc, jnp.zeros((B, Lw), jnp.float32), reverse)
            h0 = hc_d[:, 0] if reverse else hc_d[:, -1]
            h_lat = h_lat + directional_scan(al, bl, h0, reverse)
            if not last:
                h_ctx = h_ctx + hc_d
        lru_l = h_lat.astype(x.dtype) * jax.nn.gelu(zl[..., 2 * Cc + Lw:])

        x = x + g1 * (jnp.concatenate([conv_l, lru_l], axis=-1) @ w_out[l])

        if not last:
            conv_c = conformer_conv(zc[..., :Cc], zc[..., Cc:2 * Cc],
                                    conv_dw[l], conv_b[l], conv_ln_g[l], conv_ln_b[l])
            lru_c = h_ctx.astype(xc.dtype) * jax.nn.gelu(zc[..., 2 * Cc + Lw:])
            xc = xc + cg1 * (jnp.concatenate([conv_c, lru_c], axis=-1) @ w_out[l])

        h2 = modulate(rms_norm(x, norm2_g[l]), sh2, sc2).reshape(B * S, D)
        y2 = hier_moe(h2, router_wg[l], router_bg[l], router_we[l], router_be[l], w1[l], w3[l], w2[l])
        x = x + g2 * y2.reshape(B, S, D)

        if not last:
            hc2 = modulate(rms_norm(xc, norm2_g[l]), csh2, csc2).reshape(-1, D)
            yc2 = hier_moe(hc2, router_wg[l], router_bg[l], router_we[l], router_be[l], w1[l], w3[l], w2[l])
            xc = xc + cg2 * yc2.reshape(xc.shape)

    return rms_norm(x, final_g)
```

```python
import functools

import jax
import jax.numpy as jnp
from jax import lax
from jax.experimental import pallas as pl
from jax.experimental.pallas import tpu as pltpu

F32 = jnp.float32
BF16 = jnp.bfloat16

EPS = 1e-6
CONV_TAPS = 31
LRU_TAPS = 4
LRU_C = 8.0
GRID_W = 64
N_GROUPS = 4
EXPERTS_PER_GROUP = 8
N_EXPERTS = N_GROUPS * EXPERTS_PER_GROUP
TOP_K = 2

SUBLANES = 8
LANES = 128
TOKEN_BLOCK = 512
LRU_LANES = 128
ROUTE_ROWS = 128
EXPERT_ROW0 = 8
MOE_BLOCK = 256
VMEM_LIMIT = 48 * 1024 * 1024


def _cparams(sem, vmem=VMEM_LIMIT):
    return pltpu.CompilerParams(dimension_semantics=sem, vmem_limit_bytes=vmem)


def _split_bf16(a):
    hi = a.astype(BF16)
    lo = (a - hi.astype(F32)).astype(BF16)
    return hi, lo


def _dot(a, b):
    return jnp.dot(a, b, preferred_element_type=F32)


def _dot_nt(a, b):
    return lax.dot_general(a, b, (((1,), (1,)), ((), ())), preferred_element_type=F32)


def _sigmoid(x):
    return 1.0 / (1.0 + jnp.exp(-x))


def _silu(x):
    return x * _sigmoid(x)


def _gelu_tanh(x):
    c = 0.7978845608028654
    return 0.5 * x * (1.0 + jnp.tanh(c * (x + 0.044715 * (x * x * x))))


def _rms_mod(x, g, shift, scale):
    y = x * lax.rsqrt(jnp.mean(x * x, axis=-1, keepdims=True) + EPS)
    return (y * g) * (1.0 + scale) + shift


def _ada_kernel(c_ref, w_ref, b_ref, o_ref):
    a = _silu(c_ref[...])
    a_hi, a_lo = _split_bf16(a)
    w_hi, w_lo = _split_bf16(w_ref[...])
    o_ref[...] = _dot(a_hi, w_hi) + _dot(a_lo, w_hi) + _dot(a_hi, w_lo) + b_ref[...]


def _ada(c_rows, w, b):
    m, d = c_rows.shape
    n = w.shape[1]
    bn = 768
    return pl.pallas_call(
        _ada_kernel,
        out_shape=jax.ShapeDtypeStruct((m, n), F32),
        grid=(n // bn,),
        in_specs=[pl.BlockSpec((m, d), lambda j: (0, 0)),
                  pl.BlockSpec((d, bn), lambda j: (0, j)),
                  pl.BlockSpec((1, bn), lambda j: (0, j))],
        out_specs=pl.BlockSpec((m, bn), lambda j: (0, j)),
        compiler_params=_cparams(("arbitrary",)),
        name="ada_mod",
    )(c_rows, w, b)


def _inproj_kernel(x_ref, sh_ref, sc_ref, g_ref, w_ref, *o_refs, glu):
    h = _rms_mod(x_ref[...], g_ref[...], sh_ref[0], sc_ref[0])
    z = _dot(h.astype(BF16), w_ref[...])
    if not glu:
        o_refs[0][...] = z
        return
    u_ref, zl_ref, gg_ref = o_refs
    c = u_ref.shape[1]
    u_ref[...] = z[:, :c] * _sigmoid(z[:, c:2 * c])
    zl_ref[...] = z[:, 2 * c:3 * c]
    gg_ref[...] = _gelu_tanh(z[:, 3 * c:])


def _inproj(x2, shift, scale, g, w, rows_per_mod, tm, glu, name):
    t, d = x2.shape
    n = w.shape[1]
    blocks_per_mod = rows_per_mod // tm
    mod_spec = pl.BlockSpec((1, 1, d), lambda i: (i // blocks_per_mod, 0, 0))
    if glu:
        c = n // 4
        out_shape = [jax.ShapeDtypeStruct((t, c), F32)] * 3
        out_specs = [pl.BlockSpec((tm, c), lambda i: (i, 0))] * 3
    else:
        out_shape = [jax.ShapeDtypeStruct((t, n), F32)]
        out_specs = [pl.BlockSpec((tm, n), lambda i: (i, 0))]
    return pl.pallas_call(
        functools.partial(_inproj_kernel, glu=glu),
        out_shape=out_shape,
        grid=(t // tm,),
        in_specs=[pl.BlockSpec((tm, d), lambda i: (i, 0)), mod_spec, mod_spec,
                  pl.BlockSpec((1, d), lambda i: (0, 0)),
                  pl.BlockSpec((d, n), lambda i: (0, 0))],
        out_specs=out_specs,
        compiler_params=_cparams(("arbitrary",)),
        name=name,
    )(x2, shift, scale, g, w)


def _conv_kernel(u_ref, w_ref, b_ref, lg_ref, lb_ref, o_ref, pad_ref):
    half = CONV_TAPS // 2
    top = 2 * SUBLANES
    seq_rows = pad_ref.shape[1]
    c = u_ref.shape[1]
    n_seq = u_ref.shape[0] // GRID_W
    for r in range(n_seq):
        pad_ref[r, 0:top, :] = jnp.zeros((top, c), F32)
        pad_ref[r, top:top + GRID_W, :] = u_ref[r * GRID_W:(r + 1) * GRID_W, :]
        pad_ref[r, top + GRID_W:seq_rows, :] = jnp.zeros((seq_rows - top - GRID_W, c), F32)
    for r in range(n_seq):
        acc = jnp.zeros((GRID_W, c), F32)
        for k in range(CONV_TAPS):
            off = top + k - half
            acc = acc + w_ref[k:k + 1, :] * pad_ref[r, off:off + GRID_W, :]
        acc = acc + b_ref[...]
        mu = jnp.mean(acc, axis=-1, keepdims=True)
        cen = acc - mu
        var = jnp.mean(cen * cen, axis=-1, keepdims=True)
        y = cen * lax.rsqrt(var + EPS) * lg_ref[...] + lb_ref[...]
        o_ref[r * GRID_W:(r + 1) * GRID_W, :] = _silu(y).astype(o_ref.dtype)


def _conv_module(u, w, b, lg, lb, tm):
    t, c = u.shape
    n_seq = tm // GRID_W
    seq_rows = GRID_W + 4 * SUBLANES
    row = lambda i: (0, 0)
    return pl.pallas_call(
        _conv_kernel,
        out_shape=jax.ShapeDtypeStruct((t, c), BF16),
        grid=(t // tm,),
        in_specs=[pl.BlockSpec((tm, c), lambda i: (i, 0)),
                  pl.BlockSpec((CONV_TAPS, c), row),
                  pl.BlockSpec((1, c), row), pl.BlockSpec((1, c), row), pl.BlockSpec((1, c), row)],
        out_specs=pl.BlockSpec((tm, c), lambda i: (i, 0)),
        scratch_shapes=[pltpu.VMEM((n_seq, seq_rows, c), F32)],
        compiler_params=_cparams(("arbitrary",)),
        name="conformer_conv",
    )(u, w, b, lg, lb)


def _lru_kernel(zl_ref, h0_ref, cw_ref, cb_ref, wg_ref, gb_ref, lam_ref, *rest, seq, final_only):
    if final_only:
        o_ref, pad_ref, af_ref, bf_ref, ab_ref, bb_ref = rest
        gg_ref = None
    else:
        gg_ref, o_ref, pad_ref, af_ref, bf_ref, ab_ref, bb_ref = rest
    cl = seq // SUBLANES
    pitch = cl + SUBLANES
    c = zl_ref.shape[2]
    a_refs = (af_ref, ab_ref)
    b_refs = (bf_ref, bb_ref)

    pad_ref[0:SUBLANES, :] = jnp.zeros((SUBLANES, c), F32)
    pad_ref[SUBLANES:SUBLANES + seq, :] = zl_ref[0]
    pad_ref[SUBLANES + seq:2 * SUBLANES + seq, :] = jnp.zeros((SUBLANES, c), F32)

    lam = lam_ref[...]
    nlam = -lam
    softplus = jnp.maximum(nlam, 0.0) + jnp.log1p(jnp.exp(-jnp.abs(nlam)))
    decay = -LRU_C * softplus
    wg = wg_ref[0]

    for j in range(SUBLANES):
        base = SUBLANES + j * cl
        ul = cb_ref[...] + jnp.zeros((cl, c), F32)
        for k in range(LRU_TAPS):
            off = base + k - 2
            ul = ul + cw_ref[k:k + 1, :] * pad_ref[off:off + cl, :]
        g = _dot(ul.astype(BF16), wg)
        for d in range(2):
            r = _sigmoid(g[:, (2 * d) * c:(2 * d + 1) * c] + gb_ref[2 * d:2 * d + 1, :])
            i = _sigmoid(g[:, (2 * d + 1) * c:(2 * d + 2) * c] + gb_ref[2 * d + 1:2 * d + 2, :])
            log_a = decay[d:d + 1, :] * r
            a = jnp.exp(log_a)
            mult = jnp.sqrt(jnp.maximum(-jnp.tanh(log_a) * (a * a + 1.0), 1e-12))
            a_refs[d][j * pitch:j * pitch + cl, :] = a
            b_refs[d][j * pitch:j * pitch + cl, :] = mult * i * ul

    def step(n, carry):
        hf, pf, hb, pb = carry
        tf = n
        tb = cl - 1 - n
        sf = pl.ds(tf, SUBLANES, stride=pitch)
        sb = pl.ds(tb, SUBLANES, stride=pitch)
        a_f = af_ref[sf, :]
        a_b = ab_ref[sb, :]
        hf = a_f * hf + bf_ref[sf, :]
        hb = a_b * hb + bb_ref[sb, :]
        pf = a_f * pf
        pb = a_b * pb
        bf_ref[sf, :] = hf
        af_ref[sf, :] = pf
        bb_ref[sb, :] = hb
        ab_ref[sb, :] = pb
        return hf, pf, hb, pb

    zero = jnp.zeros((SUBLANES, c), F32)
    one = jnp.ones((SUBLANES, c), F32)
    lax.fori_loop(0, cl, step, (zero, one, zero, one), unroll=8)

    cf = [None] * SUBLANES
    cbk = [None] * SUBLANES
    s = h0_ref[0, 0]
    for j in range(SUBLANES):
        cf[j] = s
        last = j * pitch + cl - 1
        s = bf_ref[last:last + 1, :] + af_ref[last:last + 1, :] * s
    final_f = s
    s = h0_ref[1, 0]
    for j in reversed(range(SUBLANES)):
        cbk[j] = s
        first = j * pitch
        s = bb_ref[first:first + 1, :] + ab_ref[first:first + 1, :] * s
    final_b = s

    if final_only:
        o_ref[0, 0] = final_f
        o_ref[1, 0] = final_b
        return
    for j in range(SUBLANES):
        rows = slice(j * pitch, j * pitch + cl)
        hf = bf_ref[rows, :] + af_ref[rows, :] * cf[j]
        hb = bb_ref[rows, :] + ab_ref[rows, :] * cbk[j]
        o_ref[0, j * cl:(j + 1) * cl, :] = ((hf + hb) * gg_ref[0, j * cl:(j + 1) * cl, :]).astype(o_ref.dtype)


def _lru(zl, gg, h0, cw, cb, wg, gb, lam, final_only, name):
    b, seq, c = zl.shape
    cbk = LRU_LANES
    cl = seq // SUBLANES
    pitch = cl + SUBLANES
    seq_spec = pl.BlockSpec((1, seq, cbk), lambda bi, ci: (bi, 0, ci))
    st_spec = pl.BlockSpec((2, 1, 1, cbk), lambda bi, ci: (0, bi, 0, ci))
    chan = lambda rows: pl.BlockSpec((rows, cbk), lambda bi, ci: (0, ci))
    in_specs = [seq_spec, st_spec, chan(LRU_TAPS), chan(1),
                pl.BlockSpec((1, cbk, 4 * cbk), lambda bi, ci: (ci, 0, 0)), chan(4), chan(2)]
    args = [zl, h0, cw, cb, wg, gb, lam]
    if final_only:
        out_shape = jax.ShapeDtypeStruct((2, b, 1, c), F32)
        out_spec = st_spec
    else:
        in_specs.append(seq_spec)
        args.append(gg)
        out_shape = jax.ShapeDtypeStruct((b, seq, c), BF16)
        out_spec = seq_spec
    coef = pltpu.VMEM((SUBLANES * pitch, cbk), F32)
    return pl.pallas_call(
        functools.partial(_lru_kernel, seq=seq, final_only=final_only),
        out_shape=out_shape,
        grid=(b, c // cbk),
        in_specs=in_specs,
        out_specs=out_spec,
        scratch_shapes=[pltpu.VMEM((seq + 2 * SUBLANES, cbk), F32), coef, coef, coef, coef],
        compiler_params=_cparams(("arbitrary", "arbitrary")),
        name=name,
    )(*args)


def _outproj_kernel(cv_ref, lr_ref, x_ref, g1_ref, sh_ref, sc_ref, ng_ref, wo_ref, rh_ref, rl_ref, rb_ref,
                    x1_ref, h2_ref, lg_ref):
    c = cv_ref.shape[1]
    y = _dot(cv_ref[...], wo_ref[0:c, :]) + _dot(lr_ref[...], wo_ref[c:2 * c, :])
    x1 = x_ref[...] + g1_ref[0] * y
    x1_ref[...] = x1
    h2 = _rms_mod(x1, ng_ref[...], sh_ref[0], sc_ref[0])
    h_hi, h_lo = _split_bf16(h2)
    h2_ref[...] = h_hi.astype(F32)
    lg_ref[...] = (_dot_nt(rh_ref[...], h_hi) + _dot_nt(rl_ref[...], h_hi) + _dot_nt(rh_ref[...], h_lo)
                   + rb_ref[...])


def _outproj(conv_l, lru_l, x2, g1, sh2, sc2, ng, wo, r_hi, r_lo, r_b, rows_per_mod, tm):
    t, d = x2.shape
    c = conv_l.shape[1]
    blocks_per_mod = rows_per_mod // tm
    mod_spec = pl.BlockSpec((1, 1, d), lambda i: (i // blocks_per_mod, 0, 0))
    full = lambda a: pl.BlockSpec(a.shape, lambda i: (0, 0))
    return pl.pallas_call(
        _outproj_kernel,
        out_shape=[jax.ShapeDtypeStruct((t, d), F32), jax.ShapeDtypeStruct((t, d), F32),
                   jax.ShapeDtypeStruct((ROUTE_ROWS, t), F32)],
        grid=(t // tm,),
        in_specs=[pl.BlockSpec((tm, c), lambda i: (i, 0)), pl.BlockSpec((tm, c), lambda i: (i, 0)),
                  pl.BlockSpec((tm, d), lambda i: (i, 0)), mod_spec, mod_spec, mod_spec,
                  full(ng), full(wo), full(r_hi), full(r_lo), full(r_b)],
        out_specs=[pl.BlockSpec((tm, d), lambda i: (i, 0)), pl.BlockSpec((tm, d), lambda i: (i, 0)),
                   pl.BlockSpec((ROUTE_ROWS, tm), lambda i: (0, i))],
        compiler_params=_cparams(("arbitrary",)),
        name="out_proj_router",
    )(conv_l, lru_l, x2, g1, sh2, sc2, ng, wo, r_hi, r_lo, r_b)


def _route_kernel(lg_ref, eid_ref, gate_ref):
    n = lg_ref.shape[1]
    e = EXPERTS_PER_GROUP
    lgrp = lg_ref[0:N_GROUPS, :]
    gidx = lax.broadcasted_iota(jnp.int32, (N_GROUPS, n), 0)
    m = jnp.max(lgrp, axis=0, keepdims=True)
    ex = jnp.exp(lgrp - m)
    pg = ex / jnp.sum(ex, axis=0, keepdims=True)
    p_grp = jnp.max(pg, axis=0, keepdims=True)
    grp = jnp.min(jnp.where(pg == p_grp, gidx, N_GROUPS), axis=0, keepdims=True)
    le = jnp.zeros((e, n), F32)
    for g in range(N_GROUPS):
        rows = lg_ref[EXPERT_ROW0 + g * e:EXPERT_ROW0 + (g + 1) * e, :]
        le = jnp.where(grp == g, rows, le)
    m = jnp.max(le, axis=0, keepdims=True)
    ex = jnp.exp(le - m)
    pe = ex / jnp.sum(ex, axis=0, keepdims=True)
    eidx = lax.broadcasted_iota(jnp.int32, (e, n), 0)
    p1 = jnp.max(pe, axis=0, keepdims=True)
    i1 = jnp.min(jnp.where(pe == p1, eidx, e), axis=0, keepdims=True)
    pe2 = jnp.where(eidx == i1, -1.0, pe)
    p2 = jnp.max(pe2, axis=0, keepdims=True)
    i2 = jnp.min(jnp.where(pe2 == p2, eidx, e), axis=0, keepdims=True)
    denom = p1 + p2
    row = lax.broadcasted_iota(jnp.int32, (SUBLANES, n), 0)
    base = grp * e
    eid_ref[...] = jnp.where(row == 0, base + i1, jnp.where(row == 1, base + i2, 0))
    gate_ref[...] = jnp.where(row == 0, p_grp * p1 / denom, jnp.where(row == 1, p_grp * p2 / denom, 0.0))


def _route(logits_t):
    rows, t = logits_t.shape
    tl = 2048
    return pl.pallas_call(
        _route_kernel,
        out_shape=[jax.ShapeDtypeStruct((SUBLANES, t), jnp.int32), jax.ShapeDtypeStruct((SUBLANES, t), F32)],
        grid=(t // tl,),
        in_specs=[pl.BlockSpec((rows, tl), lambda i: (0, i))],
        out_specs=[pl.BlockSpec((SUBLANES, tl), lambda i: (0, i))] * 2,
        compiler_params=_cparams(("arbitrary",)),
        name="route",
    )(logits_t)


def _moe_kernel(blk_e_ref, nvalid_ref, nact_ref, src_ref,
                w1_ref, w3_ref, w2_ref, h_hbm, y_hbm,
                w1b, w3b, w2b, xbuf, ybuf, gsem, ssem, *, n_tok):
    i = pl.program_id(0)
    nact = nact_ref[0]
    bm = xbuf.shape[1]
    slot = lax.rem(i, 2)

    def gather_copy(blk, r, sl):
        n = src_ref[blk * bm + r]
        tok = lax.rem(n, n_tok)
        return pltpu.make_async_copy(h_hbm.at[pl.ds(tok, 1), :], xbuf.at[sl, pl.ds(r, 1), :], gsem.at[sl])

    def scatter_copy(blk, r, sl):
        n = src_ref[blk * bm + r]
        return pltpu.make_async_copy(ybuf.at[sl, pl.ds(r, 1), :], y_hbm.at[pl.ds(n, 1), :], ssem.at[sl])

    def gather_start(blk, sl):
        def body(r, carry):
            gather_copy(blk, r, sl).start()
            return carry
        lax.fori_loop(0, bm, body, 0)

    def gather_wait(blk, sl):
        def body(r, carry):
            gather_copy(blk, r, sl).wait()
            return carry
        lax.fori_loop(0, bm, body, 0)

    def scatter_start(blk, sl):
        def body(r, carry):
            scatter_copy(blk, r, sl).start()
            return carry
        lax.fori_loop(0, nvalid_ref[blk], body, 0)

    def scatter_wait(blk, sl):
        def body(r, carry):
            scatter_copy(blk, r, sl).wait()
            return carry
        lax.fori_loop(0, nvalid_ref[blk], body, 0)

    @pl.when(i < nact)
    def _():
        @pl.when(i == 0)
        def _():
            gather_start(0, 0)

        @pl.when(i + 1 < nact)
        def _():
            gather_start(i + 1, 1 - slot)

        prev_e = blk_e_ref[jnp.maximum(i - 1, 0)]

        @pl.when((i == 0) | (blk_e_ref[i] != prev_e))
        def _():
            w1b[...] = w1_ref[0].astype(BF16)
            w3b[...] = w3_ref[0].astype(BF16)
            w2b[...] = w2_ref[0].astype(BF16)

        gather_wait(i, slot)
        xb = xbuf[slot].astype(BF16)
        a = _dot(xb, w1b[...])
        g = _dot(xb, w3b[...])
        act = (_silu(a) * g).astype(BF16)
        y = _dot(act, w2b[...])

        @pl.when(i >= 2)
        def _():
            scatter_wait(i - 2, slot)

        ybuf[slot] = y
        scatter_start(i, slot)

        @pl.when(i == nact - 1)
        def _():
            @pl.when(i >= 1)
            def _():
                scatter_wait(i - 1, 1 - slot)
            scatter_wait(i, slot)


def _moe(blk_e, nvalid, nact, src, w1, w3, w2, h2, n_blocks):
    t, d = h2.shape
    de = w1.shape[2]
    bm = MOE_BLOCK
    wspec = lambda k, n: pl.BlockSpec((1, k, n), lambda i, be, nv, na, sr: (be[i], 0, 0))
    grid_spec = pltpu.PrefetchScalarGridSpec(
        num_scalar_prefetch=4,
        grid=(n_blocks,),
        in_specs=[wspec(d, de), wspec(d, de), wspec(de, d), pl.BlockSpec(memory_space=pl.ANY)],
        out_specs=pl.BlockSpec(memory_space=pl.ANY),
        scratch_shapes=[pltpu.VMEM((d, de), BF16), pltpu.VMEM((d, de), BF16), pltpu.VMEM((de, d), BF16),
                        pltpu.VMEM((2, bm, d), F32), pltpu.VMEM((2, bm, d), F32),
                        pltpu.SemaphoreType.DMA((2,)), pltpu.SemaphoreType.DMA((2,))],
    )
    return pl.pallas_call(
        functools.partial(_moe_kernel, n_tok=t),
        out_shape=jax.ShapeDtypeStruct((TOP_K * t, d), F32),
        grid_spec=grid_spec,
        compiler_params=_cparams(("arbitrary",), vmem=56 * 1024 * 1024),
        name="moe_experts",
    )(blk_e, nvalid, nact, src, w1, w3, w2, h2)


def _final_kernel(x1_ref, y0_ref, y1_ref, gt_ref, g2_ref, fg_ref, o_ref):
    gt = gt_ref[...]
    y2 = gt[:, 0:1] * y0_ref[...] + gt[:, 1:2] * y1_ref[...]
    x = x1_ref[...] + g2_ref[0] * y2
    o_ref[...] = x * lax.rsqrt(jnp.mean(x * x, axis=-1, keepdims=True) + EPS) * fg_ref[...]


def _final(x1, y, gates_tok, g2, fg, rows_per_mod, tm):
    t, d = x1.shape
    nb = t // tm
    blocks_per_mod = rows_per_mod // tm
    return pl.pallas_call(
        _final_kernel,
        out_shape=jax.ShapeDtypeStruct((t, d), F32),
        grid=(nb,),
        in_specs=[pl.BlockSpec((tm, d), lambda i: (i, 0)),
                  pl.BlockSpec((tm, d), lambda i: (i, 0)),
                  pl.BlockSpec((tm, d), lambda i: (i + nb, 0)),
                  pl.BlockSpec((tm, TOP_K), lambda i: (i, 0)),
                  pl.BlockSpec((1, 1, d), lambda i: (i // blocks_per_mod, 0, 0)),
                  pl.BlockSpec((1, d), lambda i: (0, 0))],
        out_specs=pl.BlockSpec((tm, d), lambda i: (i, 0)),
        compiler_params=_cparams(("arbitrary",)),
        name="combine_final_norm",
    )(x1, y, y, gates_tok, g2, fg)


def _dispatch(eid, n_tok, n_blocks):
    bm = MOE_BLOCK
    e_flat = eid.reshape(-1)
    n_assign = e_flat.shape[0]
    onehot = (e_flat[:, None] == jnp.arange(N_EXPERTS, dtype=jnp.int32)[None, :]).astype(jnp.int32)
    csum = jnp.cumsum(onehot, axis=0)
    counts = csum[-1]
    rank = jnp.sum((csum - onehot) * onehot, axis=1)
    padded = ((counts + bm - 1) // bm) * bm
    pend = jnp.cumsum(padded)
    pstart = pend - padded
    dest = pstart[e_flat] + rank
    src = jnp.zeros((n_blocks * bm,), jnp.int32).at[dest].set(jnp.arange(n_assign, dtype=jnp.int32))
    blk_start = jnp.arange(n_blocks, dtype=jnp.int32) * bm
    blk_e = jnp.minimum(jnp.searchsorted(pend, blk_start, side="right"), N_EXPERTS - 1).astype(jnp.int32)
    nvalid = jnp.clip(pstart[blk_e] + counts[blk_e] - blk_start, 0, bm).astype(jnp.int32)
    nact = (pend[-1] // bm).astype(jnp.int32).reshape(1)
    return blk_e, nvalid, nact, src


def kernel(x, c, ctx, c_ctx, w_ada, b_ada, norm1_g, norm2_g, w_in, conv_dw, conv_b, conv_ln_g, conv_ln_b,
           lru_conv_w, lru_conv_b, lru_wa, lru_ba, lru_wx, lru_bx, lru_lam, w_out,
           router_wg, router_bg, router_we, router_be, w1, w3, w2, final_g):
    assert w_ada.shape[0] == 1, "single-layer block"
    b, s, d = x.shape
    n_ctx = ctx.shape[1]
    t = b * s
    cc = conv_dw.shape[2]
    lw = lru_conv_w.shape[2]

    c_rows = jnp.zeros((SUBLANES, d), F32).at[:b].set(c).at[b].set(c_ctx)
    mod = _ada(c_rows, w_ada[0], b_ada)
    mod_l = mod[:b].reshape(b, 6, 1, d)
    sh1, sc1, g1, sh2, sc2, g2 = (mod_l[:, k] for k in range(6))
    mod_c = mod[b].reshape(6, 1, 1, d)
    csh1, csc1 = mod_c[0], mod_c[1]

    w_in_b = w_in[0].astype(BF16)
    w_out_b = w_out[0].astype(BF16)
    heads_per_blk = LRU_LANES // lru_wa.shape[3]
    n_cblk = lw // LRU_LANES

    def blockdiag(wh):
        hd = wh.shape[1]
        wh = wh.reshape(n_cblk, heads_per_blk, hd, hd)
        eye = jnp.eye(heads_per_blk, dtype=wh.dtype)
        return jnp.einsum("chij,hg->chigj", wh, eye).reshape(n_cblk, LRU_LANES, LRU_LANES)

    wg = jnp.concatenate([blockdiag(lru_wa[0, 0]), blockdiag(lru_wx[0, 0]),
                          blockdiag(lru_wa[0, 1]), blockdiag(lru_wx[0, 1])], axis=2).astype(BF16)
    gb = jnp.stack([lru_ba[0, 0], lru_bx[0, 0], lru_ba[0, 1], lru_bx[0, 1]])
    lam = lru_lam[0]

    zc = _inproj(ctx.reshape(b * n_ctx, d), csh1, csc1, norm1_g, w_in_b[:, 2 * cc:2 * cc + lw],
                 b * n_ctx, n_ctx, False, "in_proj_ctx")[0]
    h0 = _lru(zc.reshape(b, n_ctx, lw), None, jnp.zeros((2, b, 1, lw), F32),
              lru_conv_w[0], lru_conv_b, wg, gb, lam, True, "rglru_ctx")

    x2 = x.reshape(t, d)
    u, zl, gg = _inproj(x2, sh1, sc1, norm1_g, w_in_b, s, TOKEN_BLOCK, True, "in_proj")
    conv_l = _conv_module(u, conv_dw[0], conv_b, conv_ln_g, conv_ln_b, TOKEN_BLOCK)
    lru_l = _lru(zl.reshape(b, s, lw), gg.reshape(b, s, lw), h0,
                 lru_conv_w[0], lru_conv_b, wg, gb, lam, False, "rglru")

    wr = jnp.zeros((ROUTE_ROWS, d), F32)
    wr = wr.at[:N_GROUPS].set(router_wg[0].T)
    wr = wr.at[EXPERT_ROW0:EXPERT_ROW0 + N_EXPERTS].set(router_we[0].reshape(d, N_EXPERTS).T)
    rb = jnp.zeros((ROUTE_ROWS, 1), F32)
    rb = rb.at[:N_GROUPS, 0].set(router_bg[0])
    rb = rb.at[EXPERT_ROW0:EXPERT_ROW0 + N_EXPERTS, 0].set(router_be[0].reshape(-1))
    r_hi, r_lo = _split_bf16(wr)
    x1, h2, logits_t = _outproj(conv_l, lru_l.reshape(t, lw), x2, g1, sh2, sc2, norm2_g, w_out_b,
                                r_hi, r_lo, rb, s, TOKEN_BLOCK)

    eid8, gate8 = _route(logits_t)
    n_blocks = (TOP_K * t) // MOE_BLOCK + N_EXPERTS
    blk_e, nvalid, nact, src = _dispatch(eid8[:TOP_K], t, n_blocks)
    y = _moe(blk_e, nvalid, nact, src, w1[0], w3[0], w2[0], h2, n_blocks)
    out = _final(x1, y, gate8[:TOP_K].T, g2, final_g.reshape(1, d), s, TOKEN_BLOCK)
    return out.reshape(b, s, d)
```

```python
import functools

import jax
import jax.numpy as jnp
from jax import lax
from jax.experimental import pallas as pl
from jax.experimental.pallas import tpu as pltpu

F32 = jnp.float32
BF16 = jnp.bfloat16

EPS = 1e-6
CONV_TAPS = 31
LRU_TAPS = 4
LRU_C = 8.0
GRID_W = 64
N_GROUPS = 4
EXPERTS_PER_GROUP = 8
N_EXPERTS = N_GROUPS * EXPERTS_PER_GROUP
TOP_K = 2

SUBLANES = 8
LANES = 128
TOKEN_BLOCK = 512
LRU_LANES = 128
ROUTE_ROWS = 128
EXPERT_ROW0 = 8
MOE_BLOCK = 256
VMEM_LIMIT = 48 * 1024 * 1024


def _cparams(sem, vmem=VMEM_LIMIT):
    return pltpu.CompilerParams(dimension_semantics=sem, vmem_limit_bytes=vmem)


def _split_bf16(a):
    hi = a.astype(BF16)
    lo = (a - hi.astype(F32)).astype(BF16)
    return hi, lo


def _dot(a, b):
    return jnp.dot(a, b, preferred_element_type=F32)


def _dot_nt(a, b):
    return lax.dot_general(a, b, (((1,), (1,)), ((), ())), preferred_element_type=F32)


def _sigmoid(x):
    return 1.0 / (1.0 + jnp.exp(-x))


def _silu(x):
    return x * _sigmoid(x)


def _gelu_tanh(x):
    c = 0.7978845608028654
    return 0.5 * x * (1.0 + jnp.tanh(c * (x + 0.044715 * (x * x * x))))


def _rms_mod(x, g, shift, scale):
    y = x * lax.rsqrt(jnp.mean(x * x, axis=-1, keepdims=True) + EPS)
    return (y * g) * (1.0 + scale) + shift


def _ada_kernel(c_ref, w_ref, b_ref, o_ref):
    a = _silu(c_ref[...])
    a_hi, a_lo = _split_bf16(a)
    w_hi, w_lo = _split_bf16(w_ref[...])
    o_ref[...] = _dot(a_hi, w_hi) + _dot(a_lo, w_hi) + _dot(a_hi, w_lo) + b_ref[...]


def _ada(c_rows, w, b):
    m, d = c_rows.shape
    n = w.shape[1]
    bn = 768
    return pl.pallas_call(
        _ada_kernel,
        out_shape=jax.ShapeDtypeStruct((m, n), F32),
        grid=(n // bn,),
        in_specs=[pl.BlockSpec((m, d), lambda j: (0, 0)),
                  pl.BlockSpec((d, bn), lambda j: (0, j)),
                  pl.BlockSpec((1, bn), lambda j: (0, j))],
        out_specs=pl.BlockSpec((m, bn), lambda j: (0, j)),
        compiler_params=_cparams(("arbitrary",)),
        name="ada_mod",
    )(c_rows, w, b)


def _inproj_kernel(x_ref, sh_ref, sc_ref, g_ref, w_ref, *o_refs, glu):
    h = _rms_mod(x_ref[...], g_ref[...], sh_ref[0], sc_ref[0])
    z = _dot(h.astype(BF16), w_ref[...])
    if not glu:
        o_refs[0][...] = z
        return
    u_ref, zl_ref, gg_ref = o_refs
    c = u_ref.shape[1]
    u_ref[...] = z[:, :c] * _sigmoid(z[:, c:2 * c])
    zl_ref[...] = z[:, 2 * c:3 * c]
    gg_ref[...] = _gelu_tanh(z[:, 3 * c:])


def _inproj(x2, shift, scale, g, w, rows_per_mod, tm, glu, name):
    t, d = x2.shape
    n = w.shape[1]
    blocks_per_mod = rows_per_mod // tm
    mod_spec = pl.BlockSpec((1, 1, d), lambda i: (i // blocks_per_mod, 0, 0))
    if glu:
        c = n // 4
        out_shape = [jax.ShapeDtypeStruct((t, c), F32)] * 3
        out_specs = [pl.BlockSpec((tm, c), lambda i: (i, 0))] * 3
    else:
        out_shape = [jax.ShapeDtypeStruct((t, n), F32)]
        out_specs = [pl.BlockSpec((tm, n), lambda i: (i, 0))]
    return pl.pallas_call(
        functools.partial(_inproj_kernel, glu=glu),
        out_shape=out_shape,
        grid=(t // tm,),
        in_specs=[pl.BlockSpec((tm, d), lambda i: (i, 0)), mod_spec, mod_spec,
                  pl.BlockSpec((1, d), lambda i: (0, 0)),
                  pl.BlockSpec((d, n), lambda i: (0, 0))],
        out_specs=out_specs,
        compiler_params=_cparams(("arbitrary",)),
        name=name,
    )(x2, shift, scale, g, w)


def _conv_kernel(u_ref, w_ref, b_ref, lg_ref, lb_ref, o_ref, pad_ref):
    half = CONV_TAPS // 2
    top = 2 * SUBLANES
    seq_rows = pad_ref.shape[1]
    c = u_ref.shape[1]
    n_seq = u_ref.shape[0] // GRID_W
    for r in range(n_seq):
        pad_ref[r, 0:top, :] = jnp.zeros((top, c), F32)
        pad_ref[r, top:top + GRID_W, :] = u_ref[r * GRID_W:(r + 1) * GRID_W, :]
        pad_ref[r, top + GRID_W:seq_rows, :] = jnp.zeros((seq_rows - top - GRID_W, c), F32)
    for r in range(n_seq):
        acc = jnp.zeros((GRID_W, c), F32)
        for k in range(CONV_TAPS):
            off = top + k - half
            acc = acc + w_ref[k:k + 1, :] * pad_ref[r, off:off + GRID_W, :]
        acc = acc + b_ref[...]
        mu = jnp.mean(acc, axis=-1, keepdims=True)
        cen = acc - mu
        var = jnp.mean(cen * cen, axis=-1, keepdims=True)
        y = cen * lax.rsqrt(var + EPS) * lg_ref[...] + lb_ref[...]
        o_ref[r * GRID_W:(r + 1) * GRID_W, :] = _silu(y).astype(o_ref.dtype)


def _conv_module(u, w, b, lg, lb, tm):
    t, c = u.shape
    n_seq = tm // GRID_W
    seq_rows = GRID_W + 4 * SUBLANES
    row = lambda i: (0, 0)
    return pl.pallas_call(
        _conv_kernel,
        out_shape=jax.ShapeDtypeStruct((t, c), BF16),
        grid=(t // tm,),
        in_specs=[pl.BlockSpec((tm, c), lambda i: (i, 0)),
                  pl.BlockSpec((CONV_TAPS, c), row),
                  pl.BlockSpec((1, c), row), pl.BlockSpec((1, c), row), pl.BlockSpec((1, c), row)],
        out_specs=pl.BlockSpec((tm, c), lambda i: (i, 0)),
        scratch_shapes=[pltpu.VMEM((n_seq, seq_rows, c), F32)],
        compiler_params=_cparams(("arbitrary",)),
        name="conformer_conv",
    )(u, w, b, lg, lb)


def _lru_kernel(zl_ref, h0_ref, cw_ref, cb_ref, wg_ref, gb_ref, lam_ref, *rest, seq, final_only):
    if final_only:
        o_ref, pad_ref, af_ref, bf_ref, ab_ref, bb_ref = rest
        gg_ref = None
    else:
        gg_ref, o_ref, pad_ref, af_ref, bf_ref, ab_ref, bb_ref = rest
    cl = seq // SUBLANES
    pitch = cl + SUBLANES
    c = zl_ref.shape[2]
    a_refs = (af_ref, ab_ref)
    b_refs = (bf_ref, bb_ref)

    pad_ref[0:SUBLANES, :] = jnp.zeros((SUBLANES, c), F32)
    pad_ref[SUBLANES:SUBLANES + seq, :] = zl_ref[0]
    pad_ref[SUBLANES + seq:2 * SUBLANES + seq, :] = jnp.zeros((SUBLANES, c), F32)

    lam = lam_ref[...]
    nlam = -lam
    softplus = jnp.maximum(nlam, 0.0) + jnp.log1p(jnp.exp(-jnp.abs(nlam)))
    decay = -LRU_C * softplus
    wg = wg_ref[0]

    for j in range(SUBLANES):
        base = SUBLANES + j * cl
        ul = cb_ref[...] + jnp.zeros((cl, c), F32)
        for k in range(LRU_TAPS):
            off = base + k - 2
            ul = ul + cw_ref[k:k + 1, :] * pad_ref[off:off + cl, :]
        g = _dot(ul.astype(BF16), wg)
        for d in range(2):
            r = _sigmoid(g[:, (2 * d) * c:(2 * d + 1) * c] + gb_ref[2 * d:2 * d + 1, :])
            i = _sigmoid(g[:, (2 * d + 1) * c:(2 * d + 2) * c] + gb_ref[2 * d + 1:2 * d + 2, :])
            log_a = decay[d:d + 1, :] * r
            a = jnp.exp(log_a)
            mult = jnp.sqrt(jnp.maximum(-jnp.tanh(log_a) * (a * a + 1.0), 1e-12))
            a_refs[d][j * pitch:j * pitch + cl, :] = a
            b_refs[d][j * pitch:j * pitch + cl, :] = mult * i * ul

    def step(n, carry):
        hf, pf, hb, pb = carry
        tf = n
        tb = cl - 1 - n
        sf = pl.ds(tf, SUBLANES, stride=pitch)
        sb = pl.ds(tb, SUBLANES, stride=pitch)
        a_f = af_ref[sf, :]
        a_b = ab_ref[sb, :]
        hf = a_f * hf + bf_ref[sf, :]
        hb = a_b * hb + bb_ref[sb, :]
        pf = a_f * pf
        pb = a_b * pb
        bf_ref[sf, :] = hf
        af_ref[sf, :] = pf
        bb_ref[sb, :] = hb
        ab_ref[sb, :] = pb
        return hf, pf, hb, pb

    zero = jnp.zeros((SUBLANES, c), F32)
    one = jnp.ones((SUBLANES, c), F32)
    lax.fori_loop(0, cl, step, (zero, one, zero, one), unroll=8)

    cf = [None] * SUBLANES
    cbk = [None] * SUBLANES
    s = h0_ref[0, 0]
    for j in range(SUBLANES):
        cf[j] = s
        last = j * pitch + cl - 1
        s = bf_ref[last:last + 1, :] + af_ref[last:last + 1, :] * s
    final_f = s
    s = h0_ref[1, 0]
    for j in reversed(range(SUBLANES)):
        cbk[j] = s
        first = j * pitch
        s = bb_ref[first:first + 1, :] + ab_ref[first:first + 1, :] * s
    final_b = s

    if final_only:
        o_ref[0, 0] = final_f
        o_ref[1, 0] = final_b
        return
    for j in range(SUBLANES):
        rows = slice(j * pitch, j * pitch + cl)
        hf = bf_ref[rows, :] + af_ref[rows, :] * cf[j]
        hb = bb_ref[rows, :] + ab_ref[rows, :] * cbk[j]
        o_ref[0, j * cl:(j + 1) * cl, :] = ((hf + hb) * gg_ref[0, j * cl:(j + 1) * cl, :]).astype(o_ref.dtype)


def _lru(zl, gg, h0, cw, cb, wg, gb, lam, final_only, name):
    b, seq, c = zl.shape
    cbk = LRU_LANES
    cl = seq // SUBLANES
    pitch = cl + SUBLANES
    seq_spec = pl.BlockSpec((1, seq, cbk), lambda bi, ci: (bi, 0, ci))
    st_spec = pl.BlockSpec((2, 1, 1, cbk), lambda bi, ci: (0, bi, 0, ci))
    chan = lambda rows: pl.BlockSpec((rows, cbk), lambda bi, ci: (0, ci))
    in_specs = [seq_spec, st_spec, chan(LRU_TAPS), chan(1),
                pl.BlockSpec((1, cbk, 4 * cbk), lambda bi, ci: (ci, 0, 0)), chan(4), chan(2)]
    args = [zl, h0, cw, cb, wg, gb, lam]
    if final_only:
        out_shape = jax.ShapeDtypeStruct((2, b, 1, c), F32)
        out_spec = st_spec
    else:
        in_specs.append(seq_spec)
        args.append(gg)
        out_shape = jax.ShapeDtypeStruct((b, seq, c), BF16)
        out_spec = seq_spec
    coef = pltpu.VMEM((SUBLANES * pitch, cbk), F32)
    return pl.pallas_call(
        functools.partial(_lru_kernel, seq=seq, final_only=final_only),
        out_shape=out_shape,
        grid=(b, c // cbk),
        in_specs=in_specs,
        out_specs=out_spec,
        scratch_shapes=[pltpu.VMEM((seq + 2 * SUBLANES, cbk), F32), coef, coef, coef, coef],
        compiler_params=_cparams(("arbitrary", "arbitrary")),
        name=name,
    )(*args)


def _outproj_kernel(cv_ref, lr_ref, x_ref, g1_ref, sh_ref, sc_ref, ng_ref, wo_ref, rh_ref, rl_ref, rb_ref,
                    x1_ref, h2_ref, lg_ref):
    c = cv_ref.shape[1]
    y = _dot(cv_ref[...], wo_ref[0:c, :]) + _dot(lr_ref[...], wo_ref[c:2 * c, :])
    x1 = x_ref[...] + g1_ref[0] * y
    x1_ref[...] = x1
    h2 = _rms_mod(x1, ng_ref[...], sh_ref[0], sc_ref[0])
    h_hi, h_lo = _split_bf16(h2)
    h2_ref[...] = h_hi.astype(F32)
    lg_ref[...] = (_dot_nt(rh_ref[...], h_hi) + _dot_nt(rl_ref[...], h_hi) + _dot_nt(rh_ref[...], h_lo)
                   + rb_ref[...])


def _outproj(conv_l, lru_l, x2, g1, sh2, sc2, ng, wo, r_hi, r_lo, r_b, rows_per_mod, tm):
    t, d = x2.shape
    c = conv_l.shape[1]
    blocks_per_mod = rows_per_mod // tm
    mod_spec = pl.BlockSpec((1, 1, d), lambda i: (i // blocks_per_mod, 0, 0))
    full = lambda a: pl.BlockSpec(a.shape, lambda i: (0, 0))
    return pl.pallas_call(
        _outproj_kernel,
        out_shape=[jax.ShapeDtypeStruct((t, d), F32), jax.ShapeDtypeStruct((t, d), F32),
                   jax.ShapeDtypeStruct((ROUTE_ROWS, t), F32)],
        grid=(t // tm,),
        in_specs=[pl.BlockSpec((tm, c), lambda i: (i, 0)), pl.BlockSpec((tm, c), lambda i: (i, 0)),
                  pl.BlockSpec((tm, d), lambda i: (i, 0)), mod_spec, mod_spec, mod_spec,
                  full(ng), full(wo), full(r_hi), full(r_lo), full(r_b)],
        out_specs=[pl.BlockSpec((tm, d), lambda i: (i, 0)), pl.BlockSpec((tm, d), lambda i: (i, 0)),
                   pl.BlockSpec((ROUTE_ROWS, tm), lambda i: (0, i))],
        compiler_params=_cparams(("arbitrary",)),
        name="out_proj_router",
    )(conv_l, lru_l, x2, g1, sh2, sc2, ng, wo, r_hi, r_lo, r_b)


def _route_chunk(lg_ref, lanes):
    n = lanes.size
    e = EXPERTS_PER_GROUP
    lgrp = lg_ref[0:N_GROUPS, lanes]
    gidx = lax.broadcasted_iota(jnp.int32, (N_GROUPS, n), 0)
    m = jnp.max(lgrp, axis=0, keepdims=True)
    ex = jnp.exp(lgrp - m)
    pg = ex / jnp.sum(ex, axis=0, keepdims=True)
    p_grp = jnp.max(pg, axis=0, keepdims=True)
    grp = jnp.min(jnp.where(pg == p_grp, gidx, N_GROUPS), axis=0, keepdims=True)
    le = jnp.zeros((e, n), F32)
    for g in range(N_GROUPS):
        rows = lg_ref[EXPERT_ROW0 + g * e:EXPERT_ROW0 + (g + 1) * e, lanes]
        le = jnp.where(grp == g, rows, le)
    m = jnp.max(le, axis=0, keepdims=True)
    ex = jnp.exp(le - m)
    pe = ex / jnp.sum(ex, axis=0, keepdims=True)
    eidx = lax.broadcasted_iota(jnp.int32, (e, n), 0)
    p1 = jnp.max(pe, axis=0, keepdims=True)
    i1 = jnp.min(jnp.where(pe == p1, eidx, e), axis=0, keepdims=True)
    pe2 = jnp.where(eidx == i1, -1.0, pe)
    p2 = jnp.max(pe2, axis=0, keepdims=True)
    i2 = jnp.min(jnp.where(pe2 == p2, eidx, e), axis=0, keepdims=True)
    denom = p1 + p2
    base = grp * e
    return (base + i1, base + i2), (p_grp * p1 / denom, p_grp * p2 / denom)


def _route_kernel(lg_ref, gate_ref, dest_ref, cnt_ref, run_ref, *, block_rows):
    p = pl.program_id(0)
    i = pl.program_id(1)
    tl = lg_ref.shape[1]
    chunk = 2 * LANES
    ne = N_EXPERTS

    @pl.when((p == 0) & (i == 0))
    def _():
        run_ref[...] = jnp.zeros_like(run_ref)

    @pl.when((p == 1) & (i == 0))
    def _():
        counts = run_ref[...]
        cnt_ref[...] = counts
        padded = jnp.ceil(counts * (1.0 / block_rows)) * block_rows
        rows = []
        start = jnp.zeros((1, LANES), F32)
        for ex in range(ne):
            rows.append(start)
            start = start + padded[ex:ex + 1, :]
        run_ref[...] = jnp.concatenate(rows, axis=0)

    tri = (lax.broadcasted_iota(jnp.int32, (chunk, chunk), 0)
           <= lax.broadcasted_iota(jnp.int32, (chunk, chunk), 1)).astype(BF16)
    eidx = lax.broadcasted_iota(jnp.int32, (ne, chunk), 0)
    for cix in range(tl // chunk):
        lanes = pl.ds(cix * chunk, chunk)
        eids, gates = _route_chunk(lg_ref, lanes)
        dests = []
        for k in range(TOP_K):
            onehot = eidx == eids[k]
            cum = _dot(onehot.astype(BF16), tri)
            run = run_ref[...][:, 0:1]
            dests.append(jnp.sum(jnp.where(onehot, run + cum - 1.0, 0.0), axis=0, keepdims=True))
            run_ref[...] = run_ref[...] + cum[:, chunk - 1:chunk]
        gate_ref[:, lanes] = jnp.concatenate(gates, axis=0)
        dest_ref[:, lanes] = jnp.concatenate(dests, axis=0).astype(jnp.int32)


def _route(logits_t, block_rows):
    rows, t = logits_t.shape
    tl = 2048
    blk = pl.BlockSpec((TOP_K, tl), lambda p, i: (0, i * p))
    return pl.pallas_call(
        functools.partial(_route_kernel, block_rows=block_rows),
        out_shape=[jax.ShapeDtypeStruct((TOP_K, t), F32), jax.ShapeDtypeStruct((TOP_K, t), jnp.int32),
                   jax.ShapeDtypeStruct((N_EXPERTS, LANES), F32)],
        grid=(2, t // tl),
        in_specs=[pl.BlockSpec((rows, tl), lambda p, i: (0, i))],
        out_specs=[blk, blk, pl.BlockSpec((N_EXPERTS, LANES), lambda p, i: (0, 0))],
        scratch_shapes=[pltpu.VMEM((N_EXPERTS, LANES), F32)],
        compiler_params=_cparams(("arbitrary", "arbitrary")),
        name="route",
    )(logits_t)


def _tables_kernel(cnt_ref, dest_ref, src_ref, blk_e_ref, nact_ref, *, n_tok, bm):
    i = pl.program_id(0)
    cb = dest_ref.shape[1]
    n_rows = src_ref.shape[0]
    n_blocks = blk_e_ref.shape[0]
    shift = bm.bit_length() - 1

    def spare(r):
        return TOP_K * n_tok + ((r >> shift) & 1) * bm + (r & (bm - 1))

    @pl.when(i == 0)
    def _():
        def per_expert(ex, run):
            cnt = cnt_ref[ex]
            padded = ((cnt + (bm - 1)) >> shift) << shift

            def fill_blk(b, carry):
                blk_e_ref[b] = ex
                return carry
            lax.fori_loop(run >> shift, (run + padded) >> shift, fill_blk, 0)

            def fill_pad(r, carry):
                src_ref[r] = spare(r)
                return carry
            lax.fori_loop(run + cnt, run + padded, fill_pad, 0)
            return run + padded
        total = lax.fori_loop(0, N_EXPERTS, per_expert, 0)
        nact_ref[0] = total >> shift

        def fill_blk(b, carry):
            blk_e_ref[b] = N_EXPERTS - 1
            return carry
        lax.fori_loop(total >> shift, n_blocks, fill_blk, 0)

        def fill_pad(r, carry):
            src_ref[r] = spare(r)
            return carry
        lax.fori_loop(total, n_rows, fill_pad, 0)

    for k in range(TOP_K):
        def invert(j, carry):
            src_ref[dest_ref[k, j]] = k * n_tok + i * cb + j
            return carry
        lax.fori_loop(0, cb, invert, 0, unroll=8)


def _tables(counts, dest, n_tok, n_blocks, bm):
    cb = 2048
    smem = functools.partial(pl.BlockSpec, memory_space=pltpu.SMEM)
    return pl.pallas_call(
        functools.partial(_tables_kernel, n_tok=n_tok, bm=bm),
        out_shape=[jax.ShapeDtypeStruct((n_blocks * bm,), jnp.int32),
                   jax.ShapeDtypeStruct((n_blocks,), jnp.int32),
                   jax.ShapeDtypeStruct((1,), jnp.int32)],
        grid=(n_tok // cb,),
        in_specs=[smem(), smem((TOP_K, cb), lambda i: (0, i))],
        out_specs=[smem(), smem(), smem()],
        compiler_params=_cparams(("arbitrary",)),
        name="dispatch_tables",
    )(counts, dest)


def _moe_kernel(blk_e_ref, nact_ref, src_ref,
                w1_ref, w3_ref, w2_ref, h_hbm, y_hbm,
                w1b, w3b, w2b, xbuf, ybuf, gsem, ssem, *, n_tok):
    i = pl.program_id(0)
    nact = nact_ref[0]
    bm = xbuf.shape[1]
    slot = lax.rem(i, 2)

    def gather_start(blk, sl):
        for r in range(bm):
            tok = lax.rem(src_ref[blk * bm + r], n_tok)
            pltpu.make_async_copy(h_hbm.at[pl.ds(tok, 1), :], xbuf.at[sl, pl.ds(r, 1), :], gsem.at[sl]).start()

    def gather_wait(sl):
        pltpu.make_async_copy(h_hbm.at[pl.ds(0, bm), :], xbuf.at[sl], gsem.at[sl]).wait()

    def scatter_start(blk, sl):
        for r in range(bm):
            n = src_ref[blk * bm + r]
            pltpu.make_async_copy(ybuf.at[sl, pl.ds(r, 1), :], y_hbm.at[pl.ds(n, 1), :], ssem.at[sl]).start()

    def scatter_wait(sl):
        pltpu.make_async_copy(ybuf.at[sl], y_hbm.at[pl.ds(0, bm), :], ssem.at[sl]).wait()

    @pl.when(i < nact)
    def _():
        @pl.when(i == 0)
        def _():
            gather_start(0, 0)
            ybuf[...] = jnp.zeros_like(ybuf)
            for sl in range(2):
                spare = pltpu.make_async_copy(
                    ybuf.at[sl], y_hbm.at[pl.ds(TOP_K * n_tok + sl * bm, bm), :], ssem.at[sl])
                spare.start()
                spare.wait()

        gather_wait(slot)

        @pl.when(i + 1 < nact)
        def _():
            gather_start(i + 1, 1 - slot)

        prev_e = blk_e_ref[jnp.maximum(i - 1, 0)]

        @pl.when((i == 0) | (blk_e_ref[i] != prev_e))
        def _():
            w1b[...] = w1_ref[0].astype(BF16)
            w3b[...] = w3_ref[0].astype(BF16)
            w2b[...] = w2_ref[0].astype(BF16)

        xb = xbuf[slot].astype(BF16)
        a = _dot(xb, w1b[...])
        g = _dot(xb, w3b[...])
        act = (_silu(a) * g).astype(BF16)
        y = _dot(act, w2b[...])

        @pl.when(i >= 2)
        def _():
            scatter_wait(slot)

        ybuf[slot] = y
        scatter_start(i, slot)

        @pl.when(i == nact - 1)
        def _():
            @pl.when(i >= 1)
            def _():
                scatter_wait(1 - slot)
            scatter_wait(slot)


def _moe(blk_e, nact, src, w1, w3, w2, h2, n_blocks):
    t, d = h2.shape
    de = w1.shape[2]
    bm = MOE_BLOCK
    wspec = lambda k, n: pl.BlockSpec((1, k, n), lambda i, be, na, sr: (be[i], 0, 0))
    grid_spec = pltpu.PrefetchScalarGridSpec(
        num_scalar_prefetch=3,
        grid=(n_blocks,),
        in_specs=[wspec(d, de), wspec(d, de), wspec(de, d), pl.BlockSpec(memory_space=pl.ANY)],
        out_specs=pl.BlockSpec(memory_space=pl.ANY),
        scratch_shapes=[pltpu.VMEM((d, de), BF16), pltpu.VMEM((d, de), BF16), pltpu.VMEM((de, d), BF16),
                        pltpu.VMEM((2, bm, d), F32), pltpu.VMEM((2, bm, d), F32),
                        pltpu.SemaphoreType.DMA((2,)), pltpu.SemaphoreType.DMA((2,))],
    )
    return pl.pallas_call(
        functools.partial(_moe_kernel, n_tok=t),
        out_shape=jax.ShapeDtypeStruct((TOP_K * t + 2 * bm, d), F32),
        grid_spec=grid_spec,
        compiler_params=_cparams(("arbitrary",), vmem=56 * 1024 * 1024),
        name="moe_experts",
    )(blk_e, nact, src, w1, w3, w2, h2)


def _final_kernel(x1_ref, y0_ref, y1_ref, gt_ref, g2_ref, fg_ref, o_ref):
    gt = gt_ref[...]
    y2 = gt[:, 0:1] * y0_ref[...] + gt[:, 1:2] * y1_ref[...]
    x = x1_ref[...] + g2_ref[0] * y2
    o_ref[...] = x * lax.rsqrt(jnp.mean(x * x, axis=-1, keepdims=True) + EPS) * fg_ref[...]


def _final(x1, y, gates_tok, g2, fg, rows_per_mod, tm):
    t, d = x1.shape
    nb = t // tm
    blocks_per_mod = rows_per_mod // tm
    return pl.pallas_call(
        _final_kernel,
        out_shape=jax.ShapeDtypeStruct((t, d), F32),
        grid=(nb,),
        in_specs=[pl.BlockSpec((tm, d), lambda i: (i, 0)),
                  pl.BlockSpec((tm, d), lambda i: (i, 0)),
                  pl.BlockSpec((tm, d), lambda i: (i + nb, 0)),
                  pl.BlockSpec((tm, TOP_K), lambda i: (i, 0)),
                  pl.BlockSpec((1, 1, d), lambda i: (i // blocks_per_mod, 0, 0)),
                  pl.BlockSpec((1, d), lambda i: (0, 0))],
        out_specs=pl.BlockSpec((tm, d), lambda i: (i, 0)),
        compiler_params=_cparams(("arbitrary",)),
        name="combine_final_norm",
    )(x1, y, y, gates_tok, g2, fg)


def kernel(x, c, ctx, c_ctx, w_ada, b_ada, norm1_g, norm2_g, w_in, conv_dw, conv_b, conv_ln_g, conv_ln_b,
           lru_conv_w, lru_conv_b, lru_wa, lru_ba, lru_wx, lru_bx, lru_lam, w_out,
           router_wg, router_bg, router_we, router_be, w1, w3, w2, final_g):
    assert w_ada.shape[0] == 1, "single-layer block"
    b, s, d = x.shape
    n_ctx = ctx.shape[1]
    t = b * s
    cc = conv_dw.shape[2]
    lw = lru_conv_w.shape[2]

    c_rows = jnp.zeros((SUBLANES, d), F32).at[:b].set(c).at[b].set(c_ctx)
    mod = _ada(c_rows, w_ada[0], b_ada)
    mod_l = mod[:b].reshape(b, 6, 1, d)
    sh1, sc1, g1, sh2, sc2, g2 = (mod_l[:, k] for k in range(6))
    mod_c = mod[b].reshape(6, 1, 1, d)
    csh1, csc1 = mod_c[0], mod_c[1]

    w_in_b = w_in[0].astype(BF16)
    w_out_b = w_out[0].astype(BF16)
    heads_per_blk = LRU_LANES // lru_wa.shape[3]
    n_cblk = lw // LRU_LANES

    def blockdiag(wh):
        hd = wh.shape[1]
        wh = wh.reshape(n_cblk, heads_per_blk, hd, hd)
        eye = jnp.eye(heads_per_blk, dtype=wh.dtype)
        return jnp.einsum("chij,hg->chigj", wh, eye).reshape(n_cblk, LRU_LANES, LRU_LANES)

    wg = jnp.concatenate([blockdiag(lru_wa[0, 0]), blockdiag(lru_wx[0, 0]),
                          blockdiag(lru_wa[0, 1]), blockdiag(lru_wx[0, 1])], axis=2).astype(BF16)
    gb = jnp.stack([lru_ba[0, 0], lru_bx[0, 0], lru_ba[0, 1], lru_bx[0, 1]])
    lam = lru_lam[0]

    zc = _inproj(ctx.reshape(b * n_ctx, d), csh1, csc1, norm1_g, w_in_b[:, 2 * cc:2 * cc + lw],
                 b * n_ctx, n_ctx, False, "in_proj_ctx")[0]
    h0 = _lru(zc.reshape(b, n_ctx, lw), None, jnp.zeros((2, b, 1, lw), F32),
              lru_conv_w[0], lru_conv_b, wg, gb, lam, True, "rglru_ctx")

    x2 = x.reshape(t, d)
    u, zl, gg = _inproj(x2, sh1, sc1, norm1_g, w_in_b, s, TOKEN_BLOCK, True, "in_proj")
    conv_l = _conv_module(u, conv_dw[0], conv_b, conv_ln_g, conv_ln_b, TOKEN_BLOCK)
    lru_l = _lru(zl.reshape(b, s, lw), gg.reshape(b, s, lw), h0,
                 lru_conv_w[0], lru_conv_b, wg, gb, lam, False, "rglru")

    wr = jnp.zeros((ROUTE_ROWS, d), F32)
    wr = wr.at[:N_GROUPS].set(router_wg[0].T)
    wr = wr.at[EXPERT_ROW0:EXPERT_ROW0 + N_EXPERTS].set(router_we[0].reshape(d, N_EXPERTS).T)
    rb = jnp.zeros((ROUTE_ROWS, 1), F32)
    rb = rb.at[:N_GROUPS, 0].set(router_bg[0])
    rb = rb.at[EXPERT_ROW0:EXPERT_ROW0 + N_EXPERTS, 0].set(router_be[0].reshape(-1))
    r_hi, r_lo = _split_bf16(wr)
    x1, h2, logits_t = _outproj(conv_l, lru_l.reshape(t, lw), x2, g1, sh2, sc2, norm2_g, w_out_b,
                                r_hi, r_lo, rb, s, TOKEN_BLOCK)

    gates, dest, counts = _route(logits_t, MOE_BLOCK)
    n_blocks = (TOP_K * t) // MOE_BLOCK + N_EXPERTS
    src, blk_e, nact = _tables(counts[:, 0].astype(jnp.int32), dest, t, n_blocks, MOE_BLOCK)
    y = _moe(blk_e, nact, src, w1[0], w3[0], w2[0], h2, n_blocks)
    out = _final(x1, y, gates.T, g2, final_g.reshape(1, d), s, TOKEN_BLOCK)
    return out.reshape(b, s, d)
```

```python
import functools

import jax
import jax.numpy as jnp
from jax import lax
from jax.experimental import pallas as pl
from jax.experimental.pallas import tpu as pltpu

F32 = jnp.float32
BF16 = jnp.bfloat16

EPS = 1e-6
CONV_TAPS = 31
LRU_TAPS = 4
LRU_C = 8.0
GRID_W = 64
N_GROUPS = 4
EXPERTS_PER_GROUP = 8
N_EXPERTS = N_GROUPS * EXPERTS_PER_GROUP
TOP_K = 2

SUBLANES = 8
LANES = 128
TOKEN_BLOCK = 512
LRU_LANES = 128
ROUTE_ROWS = 128
EXPERT_ROW0 = 8
MOE_BLOCK = 256
VMEM_LIMIT = 48 * 1024 * 1024


def _cparams(sem, vmem=VMEM_LIMIT):
    return pltpu.CompilerParams(dimension_semantics=sem, vmem_limit_bytes=vmem)


def _split_bf16(a):
    hi = a.astype(BF16)
    lo = (a - hi.astype(F32)).astype(BF16)
    return hi, lo


def _dot(a, b):
    return jnp.dot(a, b, preferred_element_type=F32)


def _dot_nt(a, b):
    return lax.dot_general(a, b, (((1,), (1,)), ((), ())), preferred_element_type=F32)


def _sigmoid(x):
    return 1.0 / (1.0 + jnp.exp(-x))


def _silu(x):
    return x * _sigmoid(x)


def _gelu_tanh(x):
    c = 0.7978845608028654
    return 0.5 * x * (1.0 + jnp.tanh(c * (x + 0.044715 * (x * x * x))))


def _store_row_tiles(ref, x):
    rows = x.shape[0]
    for s in range(SUBLANES):
        ref[pl.ds(s, rows, stride=SUBLANES), :] = x[:, s * LANES:(s + 1) * LANES]


def _load_row_tiles(ref, rows):
    return jnp.concatenate([ref[pl.ds(s, rows, stride=SUBLANES), :] for s in range(SUBLANES)], axis=1)


def _rms_mod(x, g, shift, scale):
    y = x * lax.rsqrt(jnp.mean(x * x, axis=-1, keepdims=True) + EPS)
    return (y * g) * (1.0 + scale) + shift


def _ada_kernel(c_ref, w_ref, b_ref, o_ref):
    a = _silu(c_ref[...])
    a_hi, a_lo = _split_bf16(a)
    w_hi, w_lo = _split_bf16(w_ref[...])
    o_ref[...] = _dot(a_hi, w_hi) + _dot(a_lo, w_hi) + _dot(a_hi, w_lo) + b_ref[...]


def _ada(c_rows, w, b):
    m, d = c_rows.shape
    n = w.shape[1]
    bn = 768
    return pl.pallas_call(
        _ada_kernel,
        out_shape=jax.ShapeDtypeStruct((m, n), F32),
        grid=(n // bn,),
        in_specs=[pl.BlockSpec((m, d), lambda j: (0, 0)),
                  pl.BlockSpec((d, bn), lambda j: (0, j)),
                  pl.BlockSpec((1, bn), lambda j: (0, j))],
        out_specs=pl.BlockSpec((m, bn), lambda j: (0, j)),
        compiler_params=_cparams(("arbitrary",)),
        name="ada_mod",
    )(c_rows, w, b)


def _inproj_ctx_kernel(x_ref, sh_ref, sc_ref, g_ref, w_ref, o_ref):
    h = _rms_mod(x_ref[...], g_ref[...], sh_ref[0], sc_ref[0])
    o_ref[...] = _dot(h.astype(BF16), w_ref[...])


def _conformer_conv(u, cw_ref, cb_ref, lg_ref, lb_ref, o_ref, stg_ref, xt_ref, ot_ref):
    c = u.shape[1]
    slabs = range(c // LANES)
    half = CONV_TAPS // 2
    pitch = GRID_W + SUBLANES
    for q in range(SUBLANES):
        for l in slabs:
            stg_ref[l, q * pitch:q * pitch + GRID_W, :] = u[q * GRID_W:(q + 1) * GRID_W, l * LANES:(l + 1) * LANES]
    for t in range(GRID_W):
        xt_ref[t * SUBLANES:(t + 1) * SUBLANES, :] = jnp.concatenate(
            [stg_ref[l, pl.ds(t, SUBLANES, stride=pitch), :] for l in slabs], axis=1)
    group = SUBLANES
    for t0 in range(0, GRID_W, group):
        accs = [None] * group
        for k in range(CONV_TAPS):
            srcs = [t0 + j + k - half for j in range(group)]
            if not any(0 <= sidx < GRID_W for sidx in srcs):
                continue
            wk = cw_ref[k * SUBLANES:(k + 1) * SUBLANES, :]
            for j, sidx in enumerate(srcs):
                if 0 <= sidx < GRID_W:
                    term = wk * xt_ref[sidx * SUBLANES:(sidx + 1) * SUBLANES, :]
                    accs[j] = term if accs[j] is None else accs[j] + term
        for j in range(group):
            acc = accs[j] + cb_ref[...]
            mu = jnp.mean(acc, axis=-1, keepdims=True)
            cen = acc - mu
            var = jnp.mean(cen * cen, axis=-1, keepdims=True)
            y = cen * lax.rsqrt(var + EPS) * lg_ref[...] + lb_ref[...]
            y = _silu(y)
            for l in slabs:
                ot_ref[l, (t0 + j) * SUBLANES:(t0 + j + 1) * SUBLANES, :] = y[:, l * LANES:(l + 1) * LANES]
    for q in range(SUBLANES):
        o_ref[q * GRID_W:(q + 1) * GRID_W, :] = jnp.concatenate(
            [ot_ref[l, pl.ds(q, GRID_W, stride=SUBLANES), :] for l in slabs], axis=1).astype(o_ref.dtype)


def _inproj_kernel(x_ref, sh_ref, sc_ref, g_ref, w_ref, cw_ref, cb_ref, lg_ref, lb_ref,
                   cv_ref, zl_ref, gg_ref, stg_ref, xt_ref, ot_ref):
    h = _rms_mod(x_ref[...], g_ref[...], sh_ref[0], sc_ref[0])
    z = _dot(h.astype(BF16), w_ref[...])
    c = cv_ref.shape[1]
    zl_ref[...] = z[:, 2 * c:3 * c]
    gg_ref[...] = _gelu_tanh(z[:, 3 * c:])
    u = z[:, :c] * _sigmoid(z[:, c:2 * c])
    _conformer_conv(u, cw_ref, cb_ref, lg_ref, lb_ref, cv_ref, stg_ref, xt_ref, ot_ref)


def _inproj_ctx(x2, shift, scale, g, w, tm):
    t, d = x2.shape
    n = w.shape[1]
    mod_spec = pl.BlockSpec((1, 1, d), lambda i: (0, 0, 0))
    return pl.pallas_call(
        _inproj_ctx_kernel,
        out_shape=jax.ShapeDtypeStruct((t, n), F32),
        grid=(t // tm,),
        in_specs=[pl.BlockSpec((tm, d), lambda i: (i, 0)), mod_spec, mod_spec,
                  pl.BlockSpec((1, d), lambda i: (0, 0)),
                  pl.BlockSpec((d, n), lambda i: (0, 0))],
        out_specs=pl.BlockSpec((tm, n), lambda i: (i, 0)),
        compiler_params=_cparams(("arbitrary",)),
        name="in_proj_ctx",
    )(x2, shift, scale, g, w)


def _inproj(x2, shift, scale, g, w, cw, cb, lg, lb, rows_per_mod):
    t, d = x2.shape
    n = w.shape[1]
    c = n // 4
    tm = SUBLANES * GRID_W
    blocks_per_mod = rows_per_mod // tm
    mod_spec = pl.BlockSpec((1, 1, d), lambda i: (i // blocks_per_mod, 0, 0))
    const = lambda a: pl.BlockSpec(a.shape, lambda i: (0, 0))
    rows = pl.BlockSpec((tm, c), lambda i: (i, 0))
    return pl.pallas_call(
        _inproj_kernel,
        out_shape=[jax.ShapeDtypeStruct((t, c), BF16), jax.ShapeDtypeStruct((t, c), F32),
                   jax.ShapeDtypeStruct((t, c), F32)],
        grid=(t // tm,),
        in_specs=[pl.BlockSpec((tm, d), lambda i: (i, 0)), mod_spec, mod_spec, const(g), const(w),
                  const(cw), const(cb), const(lg), const(lb)],
        out_specs=[rows, rows, rows],
        scratch_shapes=[pltpu.VMEM((c // LANES, SUBLANES * (GRID_W + SUBLANES), LANES), F32),
                        pltpu.VMEM((tm, c), F32), pltpu.VMEM((c // LANES, tm, LANES), F32)],
        compiler_params=_cparams(("arbitrary",)),
        name="in_proj",
    )(x2, shift, scale, g, w, cw, cb, lg, lb)


def _lru_kernel(zl_ref, h0_ref, cw_ref, cb_ref, wg_ref, gb_ref, lam_ref, *rest, seq, final_only):
    if final_only:
        o_ref, pad_ref, af_ref, bf_ref, ab_ref, bb_ref = rest
        gg_ref = None
    else:
        gg_ref, o_ref, pad_ref, af_ref, bf_ref, ab_ref, bb_ref = rest
    cl = seq // SUBLANES
    pitch = cl + SUBLANES
    c = zl_ref.shape[2]
    a_refs = (af_ref, ab_ref)
    b_refs = (bf_ref, bb_ref)

    pad_ref[0:SUBLANES, :] = jnp.zeros((SUBLANES, c), F32)
    pad_ref[SUBLANES:SUBLANES + seq, :] = zl_ref[0]
    pad_ref[SUBLANES + seq:2 * SUBLANES + seq, :] = jnp.zeros((SUBLANES, c), F32)

    lam = lam_ref[...]
    nlam = -lam
    softplus = jnp.maximum(nlam, 0.0) + jnp.log1p(jnp.exp(-jnp.abs(nlam)))
    decay = -LRU_C * softplus
    wg = wg_ref[0]

    for j in range(SUBLANES):
        base = SUBLANES + j * cl
        ul = cb_ref[...] + jnp.zeros((cl, c), F32)
        for k in range(LRU_TAPS):
            off = base + k - 2
            ul = ul + cw_ref[k:k + 1, :] * pad_ref[off:off + cl, :]
        g = _dot(ul.astype(BF16), wg)
        for d in range(2):
            r = _sigmoid(g[:, (2 * d) * c:(2 * d + 1) * c] + gb_ref[2 * d:2 * d + 1, :])
            i = _sigmoid(g[:, (2 * d + 1) * c:(2 * d + 2) * c] + gb_ref[2 * d + 1:2 * d + 2, :])
            log_a = decay[d:d + 1, :] * r
            a = jnp.exp(log_a)
            mult = jnp.sqrt(jnp.maximum(-jnp.tanh(log_a) * (a * a + 1.0), 1e-12))
            a_refs[d][j * pitch:j * pitch + cl, :] = a
            b_refs[d][j * pitch:j * pitch + cl, :] = mult * i * ul

    def step(n, carry):
        hf, pf, hb, pb = carry
        tf = n
        tb = cl - 1 - n
        sf = pl.ds(tf, SUBLANES, stride=pitch)
        sb = pl.ds(tb, SUBLANES, stride=pitch)
        a_f = af_ref[sf, :]
        a_b = ab_ref[sb, :]
        hf = a_f * hf + bf_ref[sf, :]
        hb = a_b * hb + bb_ref[sb, :]
        pf = a_f * pf
        pb = a_b * pb
        bf_ref[sf, :] = hf
        af_ref[sf, :] = pf
        bb_ref[sb, :] = hb
        ab_ref[sb, :] = pb
        return hf, pf, hb, pb

    zero = jnp.zeros((SUBLANES, c), F32)
    one = jnp.ones((SUBLANES, c), F32)
    lax.fori_loop(0, cl, step, (zero, one, zero, one), unroll=8)

    cf = [None] * SUBLANES
    cbk = [None] * SUBLANES
    s = h0_ref[0, 0]
    for j in range(SUBLANES):
        cf[j] = s
        last = j * pitch + cl - 1
        s = bf_ref[last:last + 1, :] + af_ref[last:last + 1, :] * s
    final_f = s
    s = h0_ref[1, 0]
    for j in reversed(range(SUBLANES)):
        cbk[j] = s
        first = j * pitch
        s = bb_ref[first:first + 1, :] + ab_ref[first:first + 1, :] * s
    final_b = s

    if final_only:
        o_ref[0, 0] = final_f
        o_ref[1, 0] = final_b
        return
    for j in range(SUBLANES):
        rows = slice(j * pitch, j * pitch + cl)
        hf = bf_ref[rows, :] + af_ref[rows, :] * cf[j]
        hb = bb_ref[rows, :] + ab_ref[rows, :] * cbk[j]
        o_ref[0, j * cl:(j + 1) * cl, :] = ((hf + hb) * gg_ref[0, j * cl:(j + 1) * cl, :]).astype(o_ref.dtype)


def _lru(zl, gg, h0, cw, cb, wg, gb, lam, final_only, name):
    b, seq, c = zl.shape
    cbk = LRU_LANES
    cl = seq // SUBLANES
    pitch = cl + SUBLANES
    seq_spec = pl.BlockSpec((1, seq, cbk), lambda bi, ci: (bi, 0, ci))
    st_spec = pl.BlockSpec((2, 1, 1, cbk), lambda bi, ci: (0, bi, 0, ci))
    chan = lambda rows: pl.BlockSpec((rows, cbk), lambda bi, ci: (0, ci))
    in_specs = [seq_spec, st_spec, chan(LRU_TAPS), chan(1),
                pl.BlockSpec((1, cbk, 4 * cbk), lambda bi, ci: (ci, 0, 0)), chan(4), chan(2)]
    args = [zl, h0, cw, cb, wg, gb, lam]
    if final_only:
        out_shape = jax.ShapeDtypeStruct((2, b, 1, c), F32)
        out_spec = st_spec
    else:
        in_specs.append(seq_spec)
        args.append(gg)
        out_shape = jax.ShapeDtypeStruct((b, seq, c), BF16)
        out_spec = seq_spec
    coef = pltpu.VMEM((SUBLANES * pitch, cbk), F32)
    return pl.pallas_call(
        functools.partial(_lru_kernel, seq=seq, final_only=final_only),
        out_shape=out_shape,
        grid=(b, c // cbk),
        in_specs=in_specs,
        out_specs=out_spec,
        scratch_shapes=[pltpu.VMEM((seq + 2 * SUBLANES, cbk), F32), coef, coef, coef, coef],
        compiler_params=_cparams(("arbitrary", "arbitrary")),
        name=name,
    )(*args)


def _outproj_kernel(cv_ref, lr_ref, x_ref, g1_ref, sh_ref, sc_ref, ng_ref, wo_ref, rh_ref, rl_ref, rb_ref,
                    x1_ref, h2_ref, lg_ref):
    c = cv_ref.shape[1]
    y = _dot(cv_ref[...], wo_ref[0:c, :]) + _dot(lr_ref[...], wo_ref[c:2 * c, :])
    x1 = x_ref[...] + g1_ref[0] * y
    x1_ref[...] = x1
    h2 = _rms_mod(x1, ng_ref[...], sh_ref[0], sc_ref[0])
    h_hi, h_lo = _split_bf16(h2)
    _store_row_tiles(h2_ref, h_hi.astype(F32))
    lg_ref[...] = (_dot_nt(rh_ref[...], h_hi) + _dot_nt(rl_ref[...], h_hi) + _dot_nt(rh_ref[...], h_lo)
                   + rb_ref[...])


def _outproj(conv_l, lru_l, x2, g1, sh2, sc2, ng, wo, r_hi, r_lo, r_b, rows_per_mod, tm):
    t, d = x2.shape
    c = conv_l.shape[1]
    blocks_per_mod = rows_per_mod // tm
    mod_spec = pl.BlockSpec((1, 1, d), lambda i: (i // blocks_per_mod, 0, 0))
    full = lambda a: pl.BlockSpec(a.shape, lambda i: (0, 0))
    return pl.pallas_call(
        _outproj_kernel,
        out_shape=[jax.ShapeDtypeStruct((t, d), F32), jax.ShapeDtypeStruct((t * SUBLANES, LANES), F32),
                   jax.ShapeDtypeStruct((ROUTE_ROWS, t), F32)],
        grid=(t // tm,),
        in_specs=[pl.BlockSpec((tm, c), lambda i: (i, 0)), pl.BlockSpec((tm, c), lambda i: (i, 0)),
                  pl.BlockSpec((tm, d), lambda i: (i, 0)), mod_spec, mod_spec, mod_spec,
                  full(ng), full(wo), full(r_hi), full(r_lo), full(r_b)],
        out_specs=[pl.BlockSpec((tm, d), lambda i: (i, 0)), pl.BlockSpec((tm * SUBLANES, LANES), lambda i: (i, 0)),
                   pl.BlockSpec((ROUTE_ROWS, tm), lambda i: (0, i))],
        compiler_params=_cparams(("arbitrary",)),
        name="out_proj_router",
    )(conv_l, lru_l, x2, g1, sh2, sc2, ng, wo, r_hi, r_lo, r_b)


def _route_chunk(lg_ref, lanes):
    n = lanes.size
    e = EXPERTS_PER_GROUP
    lgrp = lg_ref[0:N_GROUPS, lanes]
    gidx = lax.broadcasted_iota(jnp.int32, (N_GROUPS, n), 0)
    m = jnp.max(lgrp, axis=0, keepdims=True)
    ex = jnp.exp(lgrp - m)
    pg = ex / jnp.sum(ex, axis=0, keepdims=True)
    p_grp = jnp.max(pg, axis=0, keepdims=True)
    grp = jnp.min(jnp.where(pg == p_grp, gidx, N_GROUPS), axis=0, keepdims=True)
    le = jnp.zeros((e, n), F32)
    for g in range(N_GROUPS):
        rows = lg_ref[EXPERT_ROW0 + g * e:EXPERT_ROW0 + (g + 1) * e, lanes]
        le = jnp.where(grp == g, rows, le)
    m = jnp.max(le, axis=0, keepdims=True)
    ex = jnp.exp(le - m)
    pe = ex / jnp.sum(ex, axis=0, keepdims=True)
    eidx = lax.broadcasted_iota(jnp.int32, (e, n), 0)
    p1 = jnp.max(pe, axis=0, keepdims=True)
    i1 = jnp.min(jnp.where(pe == p1, eidx, e), axis=0, keepdims=True)
    pe2 = jnp.where(eidx == i1, -1.0, pe)
    p2 = jnp.max(pe2, axis=0, keepdims=True)
    i2 = jnp.min(jnp.where(pe2 == p2, eidx, e), axis=0, keepdims=True)
    denom = p1 + p2
    base = grp * e
    return (base + i1, base + i2), (p_grp * p1 / denom, p_grp * p2 / denom)


def _route_kernel(lg_ref, gate_ref, dest_ref, cnt_ref, run_ref, *, block_rows):
    p = pl.program_id(0)
    i = pl.program_id(1)
    tl = lg_ref.shape[1]
    chunk = 2 * LANES
    ne = N_EXPERTS

    @pl.when((p == 0) & (i == 0))
    def _():
        run_ref[...] = jnp.zeros_like(run_ref)

    @pl.when((p == 1) & (i == 0))
    def _():
        counts = run_ref[...]
        cnt_ref[...] = counts
        padded = jnp.ceil(counts * (1.0 / block_rows)) * block_rows
        rows = []
        start = jnp.zeros((1, LANES), F32)
        for ex in range(ne):
            rows.append(start)
            start = start + padded[ex:ex + 1, :]
        run_ref[...] = jnp.concatenate(rows, axis=0)

    tri = (lax.broadcasted_iota(jnp.int32, (chunk, chunk), 0)
           <= lax.broadcasted_iota(jnp.int32, (chunk, chunk), 1)).astype(BF16)
    eidx = lax.broadcasted_iota(jnp.int32, (ne, chunk), 0)
    for cix in range(tl // chunk):
        lanes = pl.ds(cix * chunk, chunk)
        eids, gates = _route_chunk(lg_ref, lanes)
        dests = []
        for k in range(TOP_K):
            onehot = eidx == eids[k]
            cum = _dot(onehot.astype(BF16), tri)
            run = run_ref[...][:, 0:1]
            dests.append(jnp.sum(jnp.where(onehot, run + cum - 1.0, 0.0), axis=0, keepdims=True))
            run_ref[...] = run_ref[...] + cum[:, chunk - 1:chunk]
        gate_ref[:, lanes] = jnp.concatenate(gates, axis=0)
        dest_ref[:, lanes] = jnp.concatenate(dests, axis=0).astype(jnp.int32)


def _route(logits_t, block_rows):
    rows, t = logits_t.shape
    tl = 2048
    blk = pl.BlockSpec((TOP_K, tl), lambda p, i: (0, i * p))
    return pl.pallas_call(
        functools.partial(_route_kernel, block_rows=block_rows),
        out_shape=[jax.ShapeDtypeStruct((TOP_K, t), F32), jax.ShapeDtypeStruct((TOP_K, t), jnp.int32),
                   jax.ShapeDtypeStruct((N_EXPERTS, LANES), F32)],
        grid=(2, t // tl),
        in_specs=[pl.BlockSpec((rows, tl), lambda p, i: (0, i))],
        out_specs=[blk, blk, pl.BlockSpec((N_EXPERTS, LANES), lambda p, i: (0, 0))],
        scratch_shapes=[pltpu.VMEM((N_EXPERTS, LANES), F32)],
        compiler_params=_cparams(("arbitrary", "arbitrary")),
        name="route",
    )(logits_t)


def _tables_kernel(cnt_ref, dest_ref, src_ref, blk_e_ref, nact_ref, *, n_tok, bm):
    i = pl.program_id(0)
    cb = dest_ref.shape[0]
    n_rows = src_ref.shape[0]
    n_blocks = blk_e_ref.shape[0]
    shift = bm.bit_length() - 1

    def spare(r):
        return TOP_K * n_tok + ((r >> shift) & 1) * bm + (r & (bm - 1))

    @pl.when(i == 0)
    def _():
        def per_expert(ex, run):
            cnt = cnt_ref[ex]
            padded = ((cnt + (bm - 1)) >> shift) << shift

            def fill_blk(b, carry):
                blk_e_ref[b] = ex
                return carry
            lax.fori_loop(run >> shift, (run + padded) >> shift, fill_blk, 0)

            def fill_pad(r, carry):
                src_ref[r] = spare(r)
                return carry
            lax.fori_loop(run + cnt, run + padded, fill_pad, 0)
            return run + padded
        total = lax.fori_loop(0, N_EXPERTS, per_expert, 0)
        nact_ref[0] = total >> shift

        def fill_blk(b, carry):
            blk_e_ref[b] = N_EXPERTS - 1
            return carry
        lax.fori_loop(total >> shift, n_blocks, fill_blk, 0)

        def fill_pad(r, carry):
            src_ref[r] = spare(r)
            return carry
        lax.fori_loop(total, n_rows, fill_pad, 0)

    base = i * cb

    def invert(j, carry):
        src_ref[dest_ref[j]] = base + j
        return carry
    lax.fori_loop(0, cb, invert, 0, unroll=16)


def _tables(counts, dest_flat, n_tok, n_blocks, bm):
    cb = 4096
    smem = functools.partial(pl.BlockSpec, memory_space=pltpu.SMEM)
    return pl.pallas_call(
        functools.partial(_tables_kernel, n_tok=n_tok, bm=bm),
        out_shape=[jax.ShapeDtypeStruct((n_blocks * bm,), jnp.int32),
                   jax.ShapeDtypeStruct((n_blocks,), jnp.int32),
                   jax.ShapeDtypeStruct((1,), jnp.int32)],
        grid=(dest_flat.shape[0] // cb,),
        in_specs=[smem(), smem((cb,), lambda i: (i,))],
        out_specs=[smem(), smem(), smem()],
        compiler_params=_cparams(("arbitrary",)),
        name="dispatch_tables",
    )(counts, dest_flat)


def _moe_kernel(blk_e_ref, nact_ref, src_ref,
                w1_ref, w3_ref, w2_ref, h_hbm, y_hbm,
                w1b, w3b, w2b, xbuf, ybuf, gsem, ssem, *, n_tok):
    i = pl.program_id(0)
    nact = nact_ref[0]
    bm = xbuf.shape[1] // SUBLANES
    slot = lax.rem(i, 2)

    def tile(row):
        return pl.ds(pl.multiple_of(row * SUBLANES, SUBLANES), SUBLANES)

    def token_of(n):
        if n_tok & (n_tok - 1) == 0:
            return n & (n_tok - 1)
        return lax.rem(n, n_tok)

    def gather_start(blk, sl):
        for r in range(bm):
            tok = token_of(src_ref[blk * bm + r])
            pltpu.make_async_copy(h_hbm.at[tile(tok), :], xbuf.at[sl, tile(r), :], gsem.at[sl]).start()

    def gather_wait(sl):
        pltpu.make_async_copy(h_hbm.at[pl.ds(0, bm * SUBLANES), :], xbuf.at[sl], gsem.at[sl]).wait()

    def scatter_start(blk, sl):
        for r in range(bm):
            n = src_ref[blk * bm + r]
            pltpu.make_async_copy(ybuf.at[sl, tile(r), :], y_hbm.at[tile(n), :], ssem.at[sl]).start()

    def scatter_wait(sl):
        pltpu.make_async_copy(ybuf.at[sl], y_hbm.at[pl.ds(0, bm * SUBLANES), :], ssem.at[sl]).wait()

    @pl.when(i < nact)
    def _():
        @pl.when(i == 0)
        def _():
            gather_start(0, 0)
            ybuf[...] = jnp.zeros_like(ybuf)
            for sl in range(2):
                spare = pltpu.make_async_copy(
                    ybuf.at[sl], y_hbm.at[pl.ds((TOP_K * n_tok + sl * bm) * SUBLANES, bm * SUBLANES), :],
                    ssem.at[sl])
                spare.start()
                spare.wait()

        gather_wait(slot)

        @pl.when(i + 1 < nact)
        def _():
            gather_start(i + 1, 1 - slot)

        prev_e = blk_e_ref[jnp.maximum(i - 1, 0)]

        @pl.when((i == 0) | (blk_e_ref[i] != prev_e))
        def _():
            w1b[...] = w1_ref[0].astype(BF16)
            w3b[...] = w3_ref[0].astype(BF16)
            w2b[...] = w2_ref[0].astype(BF16)

        xb = _load_row_tiles(xbuf.at[slot], bm).astype(BF16)
        a = _dot(xb, w1b[...])
        g = _dot(xb, w3b[...])
        act = (_silu(a) * g).astype(BF16)
        y = _dot(act, w2b[...])

        @pl.when(i >= 2)
        def _():
            scatter_wait(slot)

        _store_row_tiles(ybuf.at[slot], y)
        scatter_start(i, slot)

        @pl.when(i == nact - 1)
        def _():
            @pl.when(i >= 1)
            def _():
                scatter_wait(1 - slot)
            scatter_wait(slot)


def _moe(blk_e, nact, src, w1, w3, w2, h2_tiles, n_blocks):
    t = h2_tiles.shape[0] // SUBLANES
    d = w1.shape[1]
    de = w1.shape[2]
    bm = MOE_BLOCK
    wspec = lambda k, n: pl.BlockSpec((1, k, n), lambda i, be, na, sr: (be[i], 0, 0))
    grid_spec = pltpu.PrefetchScalarGridSpec(
        num_scalar_prefetch=3,
        grid=(n_blocks,),
        in_specs=[wspec(d, de), wspec(d, de), wspec(de, d), pl.BlockSpec(memory_space=pl.ANY)],
        out_specs=pl.BlockSpec(memory_space=pl.ANY),
        scratch_shapes=[pltpu.VMEM((d, de), BF16), pltpu.VMEM((d, de), BF16), pltpu.VMEM((de, d), BF16),
                        pltpu.VMEM((2, bm * SUBLANES, LANES), F32), pltpu.VMEM((2, bm * SUBLANES, LANES), F32),
                        pltpu.SemaphoreType.DMA((2,)), pltpu.SemaphoreType.DMA((2,))],
    )
    return pl.pallas_call(
        functools.partial(_moe_kernel, n_tok=t),
        out_shape=jax.ShapeDtypeStruct(((TOP_K * t + 2 * bm) * SUBLANES, LANES), F32),
        grid_spec=grid_spec,
        compiler_params=_cparams(("arbitrary",), vmem=56 * 1024 * 1024),
        name="moe_experts",
    )(blk_e, nact, src, w1, w3, w2, h2_tiles)


def _final_kernel(x1_ref, y0_ref, y1_ref, gt_ref, g2_ref, fg_ref, o_ref):
    gt = gt_ref[...]
    tm = x1_ref.shape[0]
    y2 = gt[:, 0:1] * _load_row_tiles(y0_ref, tm) + gt[:, 1:2] * _load_row_tiles(y1_ref, tm)
    x = x1_ref[...] + g2_ref[0] * y2
    o_ref[...] = x * lax.rsqrt(jnp.mean(x * x, axis=-1, keepdims=True) + EPS) * fg_ref[...]


def _final(x1, y, gates_tok, g2, fg, rows_per_mod, tm):
    t, d = x1.shape
    nb = t // tm
    blocks_per_mod = rows_per_mod // tm
    return pl.pallas_call(
        _final_kernel,
        out_shape=jax.ShapeDtypeStruct((t, d), F32),
        grid=(nb,),
        in_specs=[pl.BlockSpec((tm, d), lambda i: (i, 0)),
                  pl.BlockSpec((tm * SUBLANES, LANES), lambda i: (i, 0)),
                  pl.BlockSpec((tm * SUBLANES, LANES), lambda i: (i + nb, 0)),
                  pl.BlockSpec((tm, TOP_K), lambda i: (i, 0)),
                  pl.BlockSpec((1, 1, d), lambda i: (i // blocks_per_mod, 0, 0)),
                  pl.BlockSpec((1, d), lambda i: (0, 0))],
        out_specs=pl.BlockSpec((tm, d), lambda i: (i, 0)),
        compiler_params=_cparams(("arbitrary",)),
        name="combine_final_norm",
    )(x1, y, y, gates_tok, g2, fg)


def kernel(x, c, ctx, c_ctx, w_ada, b_ada, norm1_g, norm2_g, w_in, conv_dw, conv_b, conv_ln_g, conv_ln_b,
           lru_conv_w, lru_conv_b, lru_wa, lru_ba, lru_wx, lru_bx, lru_lam, w_out,
           router_wg, router_bg, router_we, router_be, w1, w3, w2, final_g):
    assert w_ada.shape[0] == 1, "single-layer block"
    assert x.shape[2] == SUBLANES * LANES, "row-tile layout needs one (8, 128) tile per token row"
    b, s, d = x.shape
    n_ctx = ctx.shape[1]
    t = b * s
    cc = conv_dw.shape[2]
    lw = lru_conv_w.shape[2]

    c_rows = jnp.zeros((SUBLANES, d), F32).at[:b].set(c).at[b].set(c_ctx)
    mod = _ada(c_rows, w_ada[0], b_ada)
    mod_l = mod[:b].reshape(b, 6, 1, d)
    sh1, sc1, g1, sh2, sc2, g2 = (mod_l[:, k] for k in range(6))
    mod_c = mod[b].reshape(6, 1, 1, d)
    csh1, csc1 = mod_c[0], mod_c[1]

    w_in_b = w_in[0].astype(BF16)
    w_out_b = w_out[0].astype(BF16)
    heads_per_blk = LRU_LANES // lru_wa.shape[3]
    n_cblk = lw // LRU_LANES

    def blockdiag(wh):
        hd = wh.shape[1]
        wh = wh.reshape(n_cblk, heads_per_blk, hd, hd)
        eye = jnp.eye(heads_per_blk, dtype=wh.dtype)
        return jnp.einsum("chij,hg->chigj", wh, eye).reshape(n_cblk, LRU_LANES, LRU_LANES)

    wg = jnp.concatenate([blockdiag(lru_wa[0, 0]), blockdiag(lru_wx[0, 0]),
                          blockdiag(lru_wa[0, 1]), blockdiag(lru_wx[0, 1])], axis=2).astype(BF16)
    gb = jnp.stack([lru_ba[0, 0], lru_bx[0, 0], lru_ba[0, 1], lru_bx[0, 1]])
    lam = lru_lam[0]

    zc = _inproj_ctx(ctx.reshape(b * n_ctx, d), csh1, csc1, norm1_g, w_in_b[:, 2 * cc:2 * cc + lw], n_ctx)
    h0 = _lru(zc.reshape(b, n_ctx, lw), None, jnp.zeros((2, b, 1, lw), F32),
              lru_conv_w[0], lru_conv_b, wg, gb, lam, True, "rglru_ctx")

    x2 = x.reshape(t, d)
    conv_w_rep = jnp.repeat(conv_dw[0], SUBLANES, axis=0)
    conv_l, zl, gg = _inproj(x2, sh1, sc1, norm1_g, w_in_b, conv_w_rep, conv_b, conv_ln_g, conv_ln_b, s)
    lru_l = _lru(zl.reshape(b, s, lw), gg.reshape(b, s, lw), h0,
                 lru_conv_w[0], lru_conv_b, wg, gb, lam, False, "rglru")

    wr = jnp.zeros((ROUTE_ROWS, d), F32)
    wr = wr.at[:N_GROUPS].set(router_wg[0].T)
    wr = wr.at[EXPERT_ROW0:EXPERT_ROW0 + N_EXPERTS].set(router_we[0].reshape(d, N_EXPERTS).T)
    rb = jnp.zeros((ROUTE_ROWS, 1), F32)
    rb = rb.at[:N_GROUPS, 0].set(router_bg[0])
    rb = rb.at[EXPERT_ROW0:EXPERT_ROW0 + N_EXPERTS, 0].set(router_be[0].reshape(-1))
    r_hi, r_lo = _split_bf16(wr)
    x1, h2, logits_t = _outproj(conv_l, lru_l.reshape(t, lw), x2, g1, sh2, sc2, norm2_g, w_out_b,
                                r_hi, r_lo, rb, s, TOKEN_BLOCK)

    gates, dest, counts = _route(logits_t, MOE_BLOCK)
    n_blocks = (TOP_K * t) // MOE_BLOCK + N_EXPERTS
    src, blk_e, nact = _tables(counts[:, 0].astype(jnp.int32), dest.reshape(-1), t, n_blocks, MOE_BLOCK)
    y = _moe(blk_e, nact, src, w1[0], w3[0], w2[0], h2, n_blocks)
    out = _final(x1, y, gates.T, g2, final_g.reshape(1, d), s, TOKEN_BLOCK)
    return out.reshape(b, s, d)
```

```python
import functools

import jax
import jax.numpy as jnp
from jax import lax
from jax.experimental import pallas as pl
from jax.experimental.pallas import tpu as pltpu

F32 = jnp.float32
BF16 = jnp.bfloat16

EPS = 1e-6
CONV_TAPS = 31
LRU_TAPS = 4
LRU_C = 8.0
GRID_W = 64
N_GROUPS = 4
EXPERTS_PER_GROUP = 8
N_EXPERTS = N_GROUPS * EXPERTS_PER_GROUP
TOP_K = 2

SUBLANES = 8
LANES = 128
TOKEN_BLOCK = 512
LRU_LANES = 128
ROUTE_ROWS = 128
EXPERT_ROW0 = 8
MOE_BLOCK = 256
VMEM_LIMIT = 48 * 1024 * 1024


def _cparams(sem, vmem=VMEM_LIMIT):
    return pltpu.CompilerParams(dimension_semantics=sem, vmem_limit_bytes=vmem)


def _split_bf16(a):
    hi = a.astype(BF16)
    lo = (a - hi.astype(F32)).astype(BF16)
    return hi, lo


def _dot(a, b):
    return jnp.dot(a, b, preferred_element_type=F32)


def _dot_nt(a, b):
    return lax.dot_general(a, b, (((1,), (1,)), ((), ())), preferred_element_type=F32)


def _sigmoid(x):
    return 0.5 * jnp.tanh(0.5 * x) + 0.5


def _silu(x):
    return x * _sigmoid(x)


def _gelu_tanh(x):
    c = 0.7978845608028654
    return 0.5 * x * (1.0 + jnp.tanh(c * (x + 0.044715 * (x * x * x))))


def _store_row_tiles(ref, x):
    rows = x.shape[0]
    for s in range(SUBLANES):
        ref[pl.ds(s, rows, stride=SUBLANES), :] = x[:, s * LANES:(s + 1) * LANES]


def _load_row_tiles(ref, rows):
    return jnp.concatenate([ref[pl.ds(s, rows, stride=SUBLANES), :] for s in range(SUBLANES)], axis=1)


def _rms_mod(x, g, shift, scale):
    y = x * lax.rsqrt(jnp.mean(x * x, axis=-1, keepdims=True) + EPS)
    return (y * g) * (1.0 + scale) + shift


def _ada_kernel(c_ref, w_ref, b_ref, o_ref):
    a = _silu(c_ref[...])
    a_hi, a_lo = _split_bf16(a)
    w_hi, w_lo = _split_bf16(w_ref[...])
    o_ref[...] = _dot(a_hi, w_hi) + _dot(a_lo, w_hi) + _dot(a_hi, w_lo) + b_ref[...]


def _ada(c_rows, w, b):
    m, d = c_rows.shape
    n = w.shape[1]
    bn = 768
    return pl.pallas_call(
        _ada_kernel,
        out_shape=jax.ShapeDtypeStruct((m, n), F32),
        grid=(n // bn,),
        in_specs=[pl.BlockSpec((m, d), lambda j: (0, 0)),
                  pl.BlockSpec((d, bn), lambda j: (0, j)),
                  pl.BlockSpec((1, bn), lambda j: (0, j))],
        out_specs=pl.BlockSpec((m, bn), lambda j: (0, j)),
        compiler_params=_cparams(("arbitrary",)),
        name="ada_mod",
    )(c_rows, w, b)


def _inproj_ctx_kernel(x_ref, sh_ref, sc_ref, g_ref, w_ref, o_ref):
    h = _rms_mod(x_ref[...], g_ref[...], sh_ref[0], sc_ref[0])
    o_ref[...] = _dot(h.astype(BF16), w_ref[...])


def _conformer_conv(u, cw_ref, cb_ref, lg_ref, lb_ref, o_ref, stg_ref, xt_ref, ot_ref):
    c = u.shape[1]
    slabs = range(c // LANES)
    half = CONV_TAPS // 2
    pitch = GRID_W + SUBLANES
    group = SUBLANES
    for l in slabs:
        for q in range(SUBLANES):
            stg_ref[l, q * pitch:q * pitch + GRID_W, :] = u[q * GRID_W:(q + 1) * GRID_W, l * LANES:(l + 1) * LANES]

    def conv_slab(l, carry):
        for t in range(GRID_W):
            xt_ref[l, t * SUBLANES:(t + 1) * SUBLANES, :] = stg_ref[l, pl.ds(t, SUBLANES, stride=pitch), :]
        for t0 in range(0, GRID_W, group):
            accs = [None] * group
            for k in range(CONV_TAPS):
                srcs = [t0 + j + k - half for j in range(group)]
                if not any(0 <= sidx < GRID_W for sidx in srcs):
                    continue
                wk = cw_ref[l, k * SUBLANES:(k + 1) * SUBLANES, :]
                for j, sidx in enumerate(srcs):
                    if 0 <= sidx < GRID_W:
                        term = wk * xt_ref[l, sidx * SUBLANES:(sidx + 1) * SUBLANES, :]
                        accs[j] = term if accs[j] is None else accs[j] + term
            for j in range(group):
                ot_ref[l, (t0 + j) * SUBLANES:(t0 + j + 1) * SUBLANES, :] = accs[j]
        return carry
    lax.fori_loop(0, c // LANES, conv_slab, 0)
    rows_per_pass = group * SUBLANES
    for r0 in range(0, GRID_W * SUBLANES, rows_per_pass):
        rows = slice(r0, r0 + rows_per_pass)
        acc = jnp.concatenate([ot_ref[l, rows, :] for l in slabs], axis=1) + cb_ref[...]
        mu = jnp.mean(acc, axis=-1, keepdims=True)
        cen = acc - mu
        var = jnp.mean(cen * cen, axis=-1, keepdims=True)
        y = _silu(cen * lax.rsqrt(var + EPS) * lg_ref[...] + lb_ref[...])
        for l in slabs:
            ot_ref[l, rows, :] = y[:, l * LANES:(l + 1) * LANES]
    for q in range(SUBLANES):
        o_ref[q * GRID_W:(q + 1) * GRID_W, :] = jnp.concatenate(
            [ot_ref[l, pl.ds(q, GRID_W, stride=SUBLANES), :] for l in slabs], axis=1).astype(o_ref.dtype)


def _inproj_kernel(x_ref, sh_ref, sc_ref, g_ref, w_ref, cw_ref, cb_ref, lg_ref, lb_ref,
                   cv_ref, zl_ref, gg_ref, stg_ref, xt_ref, ot_ref):
    h = _rms_mod(x_ref[...], g_ref[...], sh_ref[0], sc_ref[0])
    z = _dot(h.astype(BF16), w_ref[...])
    c = cv_ref.shape[1]
    zl_ref[...] = z[:, 2 * c:3 * c]
    gg_ref[...] = _gelu_tanh(z[:, 3 * c:])
    u = z[:, :c] * _sigmoid(z[:, c:2 * c])
    _conformer_conv(u, cw_ref, cb_ref, lg_ref, lb_ref, cv_ref, stg_ref, xt_ref, ot_ref)


def _inproj_ctx(x2, shift, scale, g, w, tm):
    t, d = x2.shape
    n = w.shape[1]
    mod_spec = pl.BlockSpec((1, 1, d), lambda i: (0, 0, 0))
    return pl.pallas_call(
        _inproj_ctx_kernel,
        out_shape=jax.ShapeDtypeStruct((t, n), F32),
        grid=(t // tm,),
        in_specs=[pl.BlockSpec((tm, d), lambda i: (i, 0)), mod_spec, mod_spec,
                  pl.BlockSpec((1, d), lambda i: (0, 0)),
                  pl.BlockSpec((d, n), lambda i: (0, 0))],
        out_specs=pl.BlockSpec((tm, n), lambda i: (i, 0)),
        compiler_params=_cparams(("arbitrary",)),
        name="in_proj_ctx",
    )(x2, shift, scale, g, w)


def _inproj(x2, shift, scale, g, w, cw, cb, lg, lb, rows_per_mod):
    t, d = x2.shape
    n = w.shape[1]
    c = n // 4
    tm = SUBLANES * GRID_W
    blocks_per_mod = rows_per_mod // tm
    mod_spec = pl.BlockSpec((1, 1, d), lambda i: (i // blocks_per_mod, 0, 0))
    const = lambda a: pl.BlockSpec(a.shape, lambda i: (0,) * a.ndim)
    rows = pl.BlockSpec((tm, c), lambda i: (i, 0))
    return pl.pallas_call(
        _inproj_kernel,
        out_shape=[jax.ShapeDtypeStruct((t, c), BF16), jax.ShapeDtypeStruct((t, c), F32),
                   jax.ShapeDtypeStruct((t, c), F32)],
        grid=(t // tm,),
        in_specs=[pl.BlockSpec((tm, d), lambda i: (i, 0)), mod_spec, mod_spec, const(g), const(w),
                  const(cw), const(cb), const(lg), const(lb)],
        out_specs=[rows, rows, rows],
        scratch_shapes=[pltpu.VMEM((c // LANES, SUBLANES * (GRID_W + SUBLANES), LANES), F32),
                        pltpu.VMEM((c // LANES, tm, LANES), F32), pltpu.VMEM((c // LANES, tm, LANES), F32)],
        compiler_params=_cparams(("arbitrary",)),
        name="in_proj",
    )(x2, shift, scale, g, w, cw, cb, lg, lb)


def _lru_kernel(zl_ref, h0_ref, cw_ref, cb_ref, wg_ref, gb_ref, lam_ref, *rest, seq, final_only):
    if final_only:
        o_ref, pad_ref, af_ref, bf_ref, ab_ref, bb_ref = rest
        gg_ref = None
    else:
        gg_ref, o_ref, pad_ref, af_ref, bf_ref, ab_ref, bb_ref = rest
    cl = seq // SUBLANES
    pitch = cl + SUBLANES
    c = zl_ref.shape[2]
    a_refs = (af_ref, ab_ref)
    b_refs = (bf_ref, bb_ref)

    pad_ref[0:SUBLANES, :] = jnp.zeros((SUBLANES, c), F32)
    pad_ref[SUBLANES:SUBLANES + seq, :] = zl_ref[0]
    pad_ref[SUBLANES + seq:2 * SUBLANES + seq, :] = jnp.zeros((SUBLANES, c), F32)

    lam = lam_ref[...]
    nlam = -lam
    softplus = jnp.maximum(nlam, 0.0) + jnp.log1p(jnp.exp(-jnp.abs(nlam)))
    decay = -LRU_C * softplus
    wg = wg_ref[0]

    for j in range(SUBLANES):
        base = SUBLANES + j * cl
        ul = cb_ref[...] + jnp.zeros((cl, c), F32)
        for k in range(LRU_TAPS):
            off = base + k - 2
            ul = ul + cw_ref[k:k + 1, :] * pad_ref[off:off + cl, :]
        g = _dot(ul.astype(BF16), wg)
        for d in range(2):
            r = _sigmoid(g[:, (2 * d) * c:(2 * d + 1) * c] + gb_ref[2 * d:2 * d + 1, :])
            i = _sigmoid(g[:, (2 * d + 1) * c:(2 * d + 2) * c] + gb_ref[2 * d + 1:2 * d + 2, :])
            log_a = decay[d:d + 1, :] * r
            a = jnp.exp(log_a)
            m = jnp.maximum(-jnp.tanh(log_a) * (a * a + 1.0), 1e-12)
            mult = m * lax.rsqrt(m)
            a_refs[d][j * pitch:j * pitch + cl, :] = a
            b_refs[d][j * pitch:j * pitch + cl, :] = mult * i * ul

    def step(n, carry):
        hf, pf, hb, pb = carry
        tf = n
        tb = cl - 1 - n
        sf = pl.ds(tf, SUBLANES, stride=pitch)
        sb = pl.ds(tb, SUBLANES, stride=pitch)
        a_f = af_ref[sf, :]
        a_b = ab_ref[sb, :]
        hf = a_f * hf + bf_ref[sf, :]
        hb = a_b * hb + bb_ref[sb, :]
        pf = a_f * pf
        pb = a_b * pb
        bf_ref[sf, :] = hf
        af_ref[sf, :] = pf
        bb_ref[sb, :] = hb
        ab_ref[sb, :] = pb
        return hf, pf, hb, pb

    zero = jnp.zeros((SUBLANES, c), F32)
    one = jnp.ones((SUBLANES, c), F32)
    lax.fori_loop(0, cl, step, (zero, one, zero, one), unroll=8)

    cf = [None] * SUBLANES
    cbk = [None] * SUBLANES
    s = h0_ref[0, 0]
    for j in range(SUBLANES):
        cf[j] = s
        last = j * pitch + cl - 1
        s = bf_ref[last:last + 1, :] + af_ref[last:last + 1, :] * s
    final_f = s
    s = h0_ref[1, 0]
    for j in reversed(range(SUBLANES)):
        cbk[j] = s
        first = j * pitch
        s = bb_ref[first:first + 1, :] + ab_ref[first:first + 1, :] * s
    final_b = s

    if final_only:
        o_ref[0, 0] = final_f
        o_ref[1, 0] = final_b
        return
    for j in range(SUBLANES):
        rows = slice(j * pitch, j * pitch + cl)
        hf = bf_ref[rows, :] + af_ref[rows, :] * cf[j]
        hb = bb_ref[rows, :] + ab_ref[rows, :] * cbk[j]
        o_ref[0, j * cl:(j + 1) * cl, :] = ((hf + hb) * gg_ref[0, j * cl:(j + 1) * cl, :]).astype(o_ref.dtype)


def _lru(zl, gg, h0, cw, cb, wg, gb, lam, final_only, name):
    b, seq, c = zl.shape
    cbk = LRU_LANES
    cl = seq // SUBLANES
    pitch = cl + SUBLANES
    seq_spec = pl.BlockSpec((1, seq, cbk), lambda bi, ci: (bi, 0, ci))
    st_spec = pl.BlockSpec((2, 1, 1, cbk), lambda bi, ci: (0, bi, 0, ci))
    chan = lambda rows: pl.BlockSpec((rows, cbk), lambda bi, ci: (0, ci))
    in_specs = [seq_spec, st_spec, chan(LRU_TAPS), chan(1),
                pl.BlockSpec((1, cbk, 4 * cbk), lambda bi, ci: (ci, 0, 0)), chan(4), chan(2)]
    args = [zl, h0, cw, cb, wg, gb, lam]
    if final_only:
        out_shape = jax.ShapeDtypeStruct((2, b, 1, c), F32)
        out_spec = st_spec
    else:
        in_specs.append(seq_spec)
        args.append(gg)
        out_shape = jax.ShapeDtypeStruct((b, seq, c), BF16)
        out_spec = seq_spec
    coef = pltpu.VMEM((SUBLANES * pitch, cbk), F32)
    return pl.pallas_call(
        functools.partial(_lru_kernel, seq=seq, final_only=final_only),
        out_shape=out_shape,
        grid=(b, c // cbk),
        in_specs=in_specs,
        out_specs=out_spec,
        scratch_shapes=[pltpu.VMEM((seq + 2 * SUBLANES, cbk), F32), coef, coef, coef, coef],
        compiler_params=_cparams(("arbitrary", "arbitrary")),
        name=name,
    )(*args)


def _outproj_kernel(cv_ref, lr_ref, x_ref, g1_ref, sh_ref, sc_ref, ng_ref, wo_ref, rh_ref, rl_ref, rb_ref,
                    x1_ref, h2_ref, lg_ref):
    c = cv_ref.shape[1]
    y = _dot(cv_ref[...], wo_ref[0:c, :]) + _dot(lr_ref[...], wo_ref[c:2 * c, :])
    x1 = x_ref[...] + g1_ref[0] * y
    x1_ref[...] = x1
    h2 = _rms_mod(x1, ng_ref[...], sh_ref[0], sc_ref[0])
    h_hi, h_lo = _split_bf16(h2)
    _store_row_tiles(h2_ref, h_hi.astype(F32))
    lg_ref[...] = (_dot_nt(rh_ref[...], h_hi) + _dot_nt(rl_ref[...], h_hi) + _dot_nt(rh_ref[...], h_lo)
                   + rb_ref[...])


def _outproj(conv_l, lru_l, x2, g1, sh2, sc2, ng, wo, r_hi, r_lo, r_b, rows_per_mod, tm):
    t, d = x2.shape
    c = conv_l.shape[1]
    blocks_per_mod = rows_per_mod // tm
    mod_spec = pl.BlockSpec((1, 1, d), lambda i: (i // blocks_per_mod, 0, 0))
    full = lambda a: pl.BlockSpec(a.shape, lambda i: (0, 0))
    return pl.pallas_call(
        _outproj_kernel,
        out_shape=[jax.ShapeDtypeStruct((t, d), F32), jax.ShapeDtypeStruct((t * SUBLANES, LANES), F32),
                   jax.ShapeDtypeStruct((ROUTE_ROWS, t), F32)],
        grid=(t // tm,),
        in_specs=[pl.BlockSpec((tm, c), lambda i: (i, 0)), pl.BlockSpec((tm, c), lambda i: (i, 0)),
                  pl.BlockSpec((tm, d), lambda i: (i, 0)), mod_spec, mod_spec, mod_spec,
                  full(ng), full(wo), full(r_hi), full(r_lo), full(r_b)],
        out_specs=[pl.BlockSpec((tm, d), lambda i: (i, 0)), pl.BlockSpec((tm * SUBLANES, LANES), lambda i: (i, 0)),
                   pl.BlockSpec((ROUTE_ROWS, tm), lambda i: (0, i))],
        compiler_params=_cparams(("arbitrary",)),
        name="out_proj_router",
    )(conv_l, lru_l, x2, g1, sh2, sc2, ng, wo, r_hi, r_lo, r_b)


def _route_chunk(lg_ref, lanes):
    n = lanes.size
    e = EXPERTS_PER_GROUP
    lgrp = lg_ref[0:N_GROUPS, lanes]
    gidx = lax.broadcasted_iota(jnp.int32, (N_GROUPS, n), 0)
    m = jnp.max(lgrp, axis=0, keepdims=True)
    ex = jnp.exp(lgrp - m)
    pg = ex / jnp.sum(ex, axis=0, keepdims=True)
    p_grp = jnp.max(pg, axis=0, keepdims=True)
    grp = jnp.min(jnp.where(pg == p_grp, gidx, N_GROUPS), axis=0, keepdims=True)
    le = jnp.zeros((e, n), F32)
    for g in range(N_GROUPS):
        rows = lg_ref[EXPERT_ROW0 + g * e:EXPERT_ROW0 + (g + 1) * e, lanes]
        le = jnp.where(grp == g, rows, le)
    m = jnp.max(le, axis=0, keepdims=True)
    ex = jnp.exp(le - m)
    pe = ex / jnp.sum(ex, axis=0, keepdims=True)
    eidx = lax.broadcasted_iota(jnp.int32, (e, n), 0)
    p1 = jnp.max(pe, axis=0, keepdims=True)
    i1 = jnp.min(jnp.where(pe == p1, eidx, e), axis=0, keepdims=True)
    pe2 = jnp.where(eidx == i1, -1.0, pe)
    p2 = jnp.max(pe2, axis=0, keepdims=True)
    i2 = jnp.min(jnp.where(pe2 == p2, eidx, e), axis=0, keepdims=True)
    denom = p1 + p2
    base = grp * e
    return (base + i1, base + i2), (p_grp * p1 / denom, p_grp * p2 / denom)


def _route_kernel(lg_ref, gate_ref, dest_ref, cnt_ref, run_ref, *, block_rows):
    p = pl.program_id(0)
    i = pl.program_id(1)
    tl = lg_ref.shape[1]
    chunk = 2 * LANES
    ne = N_EXPERTS

    @pl.when((p == 0) & (i == 0))
    def _():
        run_ref[...] = jnp.zeros_like(run_ref)

    @pl.when((p == 1) & (i == 0))
    def _():
        counts = run_ref[...]
        cnt_ref[...] = counts
        padded = jnp.ceil(counts * (1.0 / block_rows)) * block_rows
        rows = []
        start = jnp.zeros((1, LANES), F32)
        for ex in range(ne):
            rows.append(start)
            start = start + padded[ex:ex + 1, :]
        run_ref[...] = jnp.concatenate(rows, axis=0)

    tri = (lax.broadcasted_iota(jnp.int32, (chunk, chunk), 0)
           <= lax.broadcasted_iota(jnp.int32, (chunk, chunk), 1)).astype(BF16)
    eidx = lax.broadcasted_iota(jnp.int32, (ne, chunk), 0)
    for cix in range(tl // chunk):
        lanes = pl.ds(cix * chunk, chunk)
        eids, gates = _route_chunk(lg_ref, lanes)
        dests = []
        for k in range(TOP_K):
            onehot = eidx == eids[k]
            cum = _dot(onehot.astype(BF16), tri)
            run = run_ref[...][:, 0:1]
            dests.append(jnp.sum(jnp.where(onehot, run + cum - 1.0, 0.0), axis=0, keepdims=True))
            run_ref[...] = run_ref[...] + cum[:, chunk - 1:chunk]
        gate_ref[:, lanes] = jnp.concatenate(gates, axis=0)
        dest_ref[:, lanes] = jnp.concatenate(dests, axis=0).astype(jnp.int32)


def _route(logits_t, block_rows):
    rows, t = logits_t.shape
    tl = 2048
    blk = pl.BlockSpec((TOP_K, tl), lambda p, i: (0, i * p))
    return pl.pallas_call(
        functools.partial(_route_kernel, block_rows=block_rows),
        out_shape=[jax.ShapeDtypeStruct((TOP_K, t), F32), jax.ShapeDtypeStruct((TOP_K, t), jnp.int32),
                   jax.ShapeDtypeStruct((N_EXPERTS, LANES), F32)],
        grid=(2, t // tl),
        in_specs=[pl.BlockSpec((rows, tl), lambda p, i: (0, i))],
        out_specs=[blk, blk, pl.BlockSpec((N_EXPERTS, LANES), lambda p, i: (0, 0))],
        scratch_shapes=[pltpu.VMEM((N_EXPERTS, LANES), F32)],
        compiler_params=_cparams(("arbitrary", "arbitrary")),
        name="route",
    )(logits_t)


def _tables_kernel(cnt_ref, dest_ref, src_ref, blk_e_ref, nact_ref, *, n_tok, bm):
    i = pl.program_id(0)
    cb = dest_ref.shape[0]
    n_rows = src_ref.shape[0]
    n_blocks = blk_e_ref.shape[0]
    shift = bm.bit_length() - 1

    def spare(r):
        return TOP_K * n_tok + ((r >> shift) & 1) * bm + (r & (bm - 1))

    @pl.when(i == 0)
    def _():
        def per_expert(ex, run):
            cnt = cnt_ref[ex]
            padded = ((cnt + (bm - 1)) >> shift) << shift

            def fill_blk(b, carry):
                blk_e_ref[b] = ex
                return carry
            lax.fori_loop(run >> shift, (run + padded) >> shift, fill_blk, 0)

            def fill_pad(r, carry):
                src_ref[r] = spare(r)
                return carry
            lax.fori_loop(run + cnt, run + padded, fill_pad, 0)
            return run + padded
        total = lax.fori_loop(0, N_EXPERTS, per_expert, 0)
        nact_ref[0] = total >> shift

        def fill_blk(b, carry):
            blk_e_ref[b] = N_EXPERTS - 1
            return carry
        lax.fori_loop(total >> shift, n_blocks, fill_blk, 0)

        def fill_pad(r, carry):
            src_ref[r] = spare(r)
            return carry
        lax.fori_loop(total, n_rows, fill_pad, 0)

    base = i * cb

    def invert(j, carry):
        src_ref[dest_ref[j]] = base + j
        return carry
    lax.fori_loop(0, cb, invert, 0, unroll=16)


def _tables(counts, dest_flat, n_tok, n_blocks, bm):
    cb = 4096
    smem = functools.partial(pl.BlockSpec, memory_space=pltpu.SMEM)
    return pl.pallas_call(
        functools.partial(_tables_kernel, n_tok=n_tok, bm=bm),
        out_shape=[jax.ShapeDtypeStruct((n_blocks * bm,), jnp.int32),
                   jax.ShapeDtypeStruct((n_blocks,), jnp.int32),
                   jax.ShapeDtypeStruct((1,), jnp.int32)],
        grid=(dest_flat.shape[0] // cb,),
        in_specs=[smem(), smem((cb,), lambda i: (i,))],
        out_specs=[smem(), smem(), smem()],
        compiler_params=_cparams(("arbitrary",)),
        name="dispatch_tables",
    )(counts, dest_flat)


def _moe_kernel(blk_e_ref, nact_ref, src_ref,
                w1_ref, w3_ref, w2_ref, h_hbm, y_hbm,
                w1b, w3b, w2b, xbuf, ybuf, gsem, ssem, *, n_tok):
    i = pl.program_id(0)
    nact = nact_ref[0]
    n_blocks = blk_e_ref.shape[0]
    bm = xbuf.shape[1] // SUBLANES
    de = w1b.shape[1]
    slot = lax.rem(i, 2)
    other = 1 - slot
    chunk = 2 * LANES
    n_chunks = de // chunk
    rows_per_chunk = bm // n_chunks

    def tile(row):
        return pl.ds(pl.multiple_of(row * SUBLANES, SUBLANES), SUBLANES)

    def token_of(n):
        if n_tok & (n_tok - 1) == 0:
            return n & (n_tok - 1)
        return lax.rem(n, n_tok)

    def gather_start(blk, sl, rows):
        for r in rows:
            tok = token_of(src_ref[blk * bm + r])
            pltpu.make_async_copy(h_hbm.at[tile(tok), :], xbuf.at[sl, tile(r), :], gsem.at[sl]).start()

    def gather_wait(sl):
        pltpu.make_async_copy(h_hbm.at[pl.ds(0, bm * SUBLANES), :], xbuf.at[sl], gsem.at[sl]).wait()

    def scatter_wait(sl):
        pltpu.make_async_copy(ybuf.at[sl], y_hbm.at[pl.ds(0, bm * SUBLANES), :], ssem.at[sl]).wait()

    def spare_row(set_id, r):
        return TOP_K * n_tok + set_id * bm + r

    @pl.when(i < nact)
    def _():
        @pl.when(i == 0)
        def _():
            gather_start(0, 0, range(bm))
            ybuf[...] = jnp.zeros_like(ybuf)
            for sl in range(2):
                spare = pltpu.make_async_copy(
                    ybuf.at[sl], y_hbm.at[pl.ds(spare_row(sl, 0) * SUBLANES, bm * SUBLANES), :], ssem.at[sl])
                spare.start()
                spare.wait()

        prev_e = blk_e_ref[jnp.maximum(i - 1, 0)]

        @pl.when((i == 0) | (blk_e_ref[i] != prev_e))
        def _():
            w1b[...] = w1_ref[0].astype(BF16)
            w3b[...] = w3_ref[0].astype(BF16)
            w2b[...] = w2_ref[0].astype(BF16)

        gather_wait(slot)
        xb = _load_row_tiles(xbuf.at[slot], bm).astype(BF16)
        nxt = jnp.minimum(i + 1, n_blocks - 1)
        prv = jnp.maximum(i - 1, 0)
        has_prev = i > 0
        y = None
        for c in range(n_chunks):
            cols = slice(c * chunk, (c + 1) * chunk)
            a = _dot(xb, w1b[:, cols])
            g = _dot(xb, w3b[:, cols])
            act = (_silu(a) * g).astype(BF16)
            part = _dot(act, w2b[cols, :])
            y = part if y is None else y + part
            rows = range(c * rows_per_chunk, (c + 1) * rows_per_chunk)
            gather_start(nxt, other, rows)
            for r in rows:
                n = jnp.where(has_prev, src_ref[prv * bm + r], spare_row(1, r))
                pltpu.make_async_copy(ybuf.at[other, tile(r), :], y_hbm.at[tile(n), :], ssem.at[other]).start()

        @pl.when(i >= 1)
        def _():
            scatter_wait(slot)

        _store_row_tiles(ybuf.at[slot], y)

        @pl.when(i == nact - 1)
        def _():
            for r in range(bm):
                n = src_ref[i * bm + r]
                pltpu.make_async_copy(ybuf.at[slot, tile(r), :], y_hbm.at[tile(n), :], ssem.at[slot]).start()
            scatter_wait(other)
            scatter_wait(slot)
            gather_wait(other)


def _moe(blk_e, nact, src, w1, w3, w2, h2_tiles, n_blocks):
    t = h2_tiles.shape[0] // SUBLANES
    d = w1.shape[1]
    de = w1.shape[2]
    bm = MOE_BLOCK
    wspec = lambda k, n: pl.BlockSpec((1, k, n), lambda i, be, na, sr: (be[i], 0, 0))
    grid_spec = pltpu.PrefetchScalarGridSpec(
        num_scalar_prefetch=3,
        grid=(n_blocks,),
        in_specs=[wspec(d, de), wspec(d, de), wspec(de, d), pl.BlockSpec(memory_space=pl.ANY)],
        out_specs=pl.BlockSpec(memory_space=pl.ANY),
        scratch_shapes=[pltpu.VMEM((d, de), BF16), pltpu.VMEM((d, de), BF16), pltpu.VMEM((de, d), BF16),
                        pltpu.VMEM((2, bm * SUBLANES, LANES), F32), pltpu.VMEM((2, bm * SUBLANES, LANES), F32),
                        pltpu.SemaphoreType.DMA((2,)), pltpu.SemaphoreType.DMA((2,))],
    )
    return pl.pallas_call(
        functools.partial(_moe_kernel, n_tok=t),
        out_shape=jax.ShapeDtypeStruct(((TOP_K * t + 2 * bm) * SUBLANES, LANES), F32),
        grid_spec=grid_spec,
        compiler_params=_cparams(("arbitrary",), vmem=56 * 1024 * 1024),
        name="moe_experts",
    )(blk_e, nact, src, w1, w3, w2, h2_tiles)


def _final_kernel(x1_ref, y0_ref, y1_ref, gt_ref, g2_ref, fg_ref, o_ref):
    gt = gt_ref[...]
    tm = x1_ref.shape[0]
    y2 = gt[:, 0:1] * _load_row_tiles(y0_ref, tm) + gt[:, 1:2] * _load_row_tiles(y1_ref, tm)
    x = x1_ref[...] + g2_ref[0] * y2
    o_ref[...] = x * lax.rsqrt(jnp.mean(x * x, axis=-1, keepdims=True) + EPS) * fg_ref[...]


def _final(x1, y, gates_tok, g2, fg, rows_per_mod, tm):
    t, d = x1.shape
    nb = t // tm
    blocks_per_mod = rows_per_mod // tm
    return pl.pallas_call(
        _final_kernel,
        out_shape=jax.ShapeDtypeStruct((t, d), F32),
        grid=(nb,),
        in_specs=[pl.BlockSpec((tm, d), lambda i: (i, 0)),
                  pl.BlockSpec((tm * SUBLANES, LANES), lambda i: (i, 0)),
                  pl.BlockSpec((tm * SUBLANES, LANES), lambda i: (i + nb, 0)),
                  pl.BlockSpec((tm, TOP_K), lambda i: (i, 0)),
                  pl.BlockSpec((1, 1, d), lambda i: (i // blocks_per_mod, 0, 0)),
                  pl.BlockSpec((1, d), lambda i: (0, 0))],
        out_specs=pl.BlockSpec((tm, d), lambda i: (i, 0)),
        compiler_params=_cparams(("arbitrary",)),
        name="combine_final_norm",
    )(x1, y, y, gates_tok, g2, fg)


def kernel(x, c, ctx, c_ctx, w_ada, b_ada, norm1_g, norm2_g, w_in, conv_dw, conv_b, conv_ln_g, conv_ln_b,
           lru_conv_w, lru_conv_b, lru_wa, lru_ba, lru_wx, lru_bx, lru_lam, w_out,
           router_wg, router_bg, router_we, router_be, w1, w3, w2, final_g):
    assert w_ada.shape[0] == 1, "single-layer block"
    assert x.shape[2] == SUBLANES * LANES, "row-tile layout needs one (8, 128) tile per token row"
    b, s, d = x.shape
    n_ctx = ctx.shape[1]
    t = b * s
    cc = conv_dw.shape[2]
    lw = lru_conv_w.shape[2]

    c_rows = jnp.zeros((SUBLANES, d), F32).at[:b].set(c).at[b].set(c_ctx)
    mod = _ada(c_rows, w_ada[0], b_ada)
    mod_l = mod[:b].reshape(b, 6, 1, d)
    sh1, sc1, g1, sh2, sc2, g2 = (mod_l[:, k] for k in range(6))
    mod_c = mod[b].reshape(6, 1, 1, d)
    csh1, csc1 = mod_c[0], mod_c[1]

    w_in_b = w_in[0].astype(BF16)
    w_out_b = w_out[0].astype(BF16)
    heads_per_blk = LRU_LANES // lru_wa.shape[3]
    n_cblk = lw // LRU_LANES

    def blockdiag(wh):
        hd = wh.shape[1]
        wh = wh.reshape(n_cblk, heads_per_blk, hd, hd)
        eye = jnp.eye(heads_per_blk, dtype=wh.dtype)
        return jnp.einsum("chij,hg->chigj", wh, eye).reshape(n_cblk, LRU_LANES, LRU_LANES)

    wg = jnp.concatenate([blockdiag(lru_wa[0, 0]), blockdiag(lru_wx[0, 0]),
                          blockdiag(lru_wa[0, 1]), blockdiag(lru_wx[0, 1])], axis=2).astype(BF16)
    gb = jnp.stack([lru_ba[0, 0], lru_bx[0, 0], lru_ba[0, 1], lru_bx[0, 1]])
    lam = lru_lam[0]

    zc = _inproj_ctx(ctx.reshape(b * n_ctx, d), csh1, csc1, norm1_g, w_in_b[:, 2 * cc:2 * cc + lw], n_ctx)
    h0 = _lru(zc.reshape(b, n_ctx, lw), None, jnp.zeros((2, b, 1, lw), F32),
              lru_conv_w[0], lru_conv_b, wg, gb, lam, True, "rglru_ctx")

    x2 = x.reshape(t, d)
    conv_w_rep = jnp.repeat(conv_dw[0], SUBLANES, axis=0).reshape(CONV_TAPS * SUBLANES, cc // LANES, LANES)
    conv_w_rep = conv_w_rep.transpose(1, 0, 2)
    conv_l, zl, gg = _inproj(x2, sh1, sc1, norm1_g, w_in_b, conv_w_rep, conv_b, conv_ln_g, conv_ln_b, s)
    lru_l = _lru(zl.reshape(b, s, lw), gg.reshape(b, s, lw), h0,
                 lru_conv_w[0], lru_conv_b, wg, gb, lam, False, "rglru")

    wr = jnp.zeros((ROUTE_ROWS, d), F32)
    wr = wr.at[:N_GROUPS].set(router_wg[0].T)
    wr = wr.at[EXPERT_ROW0:EXPERT_ROW0 + N_EXPERTS].set(router_we[0].reshape(d, N_EXPERTS).T)
    rb = jnp.zeros((ROUTE_ROWS, 1), F32)
    rb = rb.at[:N_GROUPS, 0].set(router_bg[0])
    rb = rb.at[EXPERT_ROW0:EXPERT_ROW0 + N_EXPERTS, 0].set(router_be[0].reshape(-1))
    r_hi, r_lo = _split_bf16(wr)
    x1, h2, logits_t = _outproj(conv_l, lru_l.reshape(t, lw), x2, g1, sh2, sc2, norm2_g, w_out_b,
                                r_hi, r_lo, rb, s, TOKEN_BLOCK)

    gates, dest, counts = _route(logits_t, MOE_BLOCK)
    n_blocks = (TOP_K * t) // MOE_BLOCK + N_EXPERTS
    src, blk_e, nact = _tables(counts[:, 0].astype(jnp.int32), dest.reshape(-1), t, n_blocks, MOE_BLOCK)
    y = _moe(blk_e, nact, src, w1[0], w3[0], w2[0], h2, n_blocks)
    out = _final(x1, y, gates.T, g2, final_g.reshape(1, d), s, TOKEN_BLOCK)
    return out.reshape(b, s, d)
```

```python
import functools

import jax
import jax.numpy as jnp
from jax import lax
from jax.experimental import pallas as pl
from jax.experimental.pallas import tpu as pltpu

F32 = jnp.float32
BF16 = jnp.bfloat16

EPS = 1e-6
CONV_TAPS = 31
LRU_TAPS = 4
LRU_C = 8.0
GRID_W = 64
N_GROUPS = 4
EXPERTS_PER_GROUP = 8
N_EXPERTS = N_GROUPS * EXPERTS_PER_GROUP
TOP_K = 2

SUBLANES = 8
LANES = 128
TOKEN_BLOCK = 512
LRU_LANES = 128
ROUTE_ROWS = 128
EXPERT_ROW0 = 8
MOE_BLOCK = 256
VMEM_LIMIT = 48 * 1024 * 1024


def _cparams(sem, vmem=VMEM_LIMIT):
    return pltpu.CompilerParams(dimension_semantics=sem, vmem_limit_bytes=vmem)


def _split_bf16(a):
    hi = a.astype(BF16)
    lo = (a - hi.astype(F32)).astype(BF16)
    return hi, lo


def _dot(a, b):
    return jnp.dot(a, b, preferred_element_type=F32)


def _dot_nt(a, b):
    return lax.dot_general(a, b, (((1,), (1,)), ((), ())), preferred_element_type=F32)


def _sigmoid(x):
    return 0.5 * jnp.tanh(0.5 * x) + 0.5


def _silu(x):
    return x * _sigmoid(x)


def _gelu_tanh(x):
    c = 0.7978845608028654
    return 0.5 * x * (1.0 + jnp.tanh(c * (x + 0.044715 * (x * x * x))))


def _store_row_tiles(ref, x):
    rows = x.shape[0]
    for s in range(SUBLANES):
        ref[pl.ds(s, rows, stride=SUBLANES), :] = x[:, s * LANES:(s + 1) * LANES]


def _load_row_tiles(ref, rows):
    return jnp.concatenate([ref[pl.ds(s, rows, stride=SUBLANES), :] for s in range(SUBLANES)], axis=1)


def _rms_mod(x, g, shift, scale):
    y = x * lax.rsqrt(jnp.mean(x * x, axis=-1, keepdims=True) + EPS)
    return (y * g) * (1.0 + scale) + shift


def _ada_kernel(c_ref, w_ref, b_ref, o_ref):
    a = _silu(c_ref[...])
    a_hi, a_lo = _split_bf16(a)
    w_hi, w_lo = _split_bf16(w_ref[...])
    o_ref[...] = _dot(a_hi, w_hi) + _dot(a_lo, w_hi) + _dot(a_hi, w_lo) + b_ref[...]


def _ada(c_rows, w, b):
    m, d = c_rows.shape
    n = w.shape[1]
    bn = 768
    return pl.pallas_call(
        _ada_kernel,
        out_shape=jax.ShapeDtypeStruct((m, n), F32),
        grid=(n // bn,),
        in_specs=[pl.BlockSpec((m, d), lambda j: (0, 0)),
                  pl.BlockSpec((d, bn), lambda j: (0, j)),
                  pl.BlockSpec((1, bn), lambda j: (0, j))],
        out_specs=pl.BlockSpec((m, bn), lambda j: (0, j)),
        compiler_params=_cparams(("arbitrary",)),
        name="ada_mod",
    )(c_rows, w, b)


def _inproj_ctx_kernel(x_ref, sh_ref, sc_ref, g_ref, w_ref, o_ref):
    h = _rms_mod(x_ref[...], g_ref[...], sh_ref[0], sc_ref[0])
    o_ref[...] = _dot(h.astype(BF16), w_ref[...])


def _conformer_conv(u, cw_ref, cb_ref, lg_ref, lb_ref, o_ref, stg_ref, xt_ref, ot_ref):
    c = u.shape[1]
    slabs = range(c // LANES)
    half = CONV_TAPS // 2
    pitch = GRID_W + SUBLANES
    group = SUBLANES
    for l in slabs:
        for q in range(SUBLANES):
            stg_ref[l, q * pitch:q * pitch + GRID_W, :] = u[q * GRID_W:(q + 1) * GRID_W, l * LANES:(l + 1) * LANES]

    def conv_slab(l, carry):
        for t in range(GRID_W):
            xt_ref[l, t * SUBLANES:(t + 1) * SUBLANES, :] = stg_ref[l, pl.ds(t, SUBLANES, stride=pitch), :]
        for t0 in range(0, GRID_W, group):
            accs = [None] * group
            for k in range(CONV_TAPS):
                srcs = [t0 + j + k - half for j in range(group)]
                if not any(0 <= sidx < GRID_W for sidx in srcs):
                    continue
                wk = cw_ref[l, k * SUBLANES:(k + 1) * SUBLANES, :]
                for j, sidx in enumerate(srcs):
                    if 0 <= sidx < GRID_W:
                        term = wk * xt_ref[l, sidx * SUBLANES:(sidx + 1) * SUBLANES, :]
                        accs[j] = term if accs[j] is None else accs[j] + term
            for j in range(group):
                ot_ref[l, (t0 + j) * SUBLANES:(t0 + j + 1) * SUBLANES, :] = accs[j]
        return carry
    lax.fori_loop(0, c // LANES, conv_slab, 0)
    rows_per_pass = group * SUBLANES
    for r0 in range(0, GRID_W * SUBLANES, rows_per_pass):
        rows = slice(r0, r0 + rows_per_pass)
        acc = jnp.concatenate([ot_ref[l, rows, :] for l in slabs], axis=1) + cb_ref[...]
        mu = jnp.mean(acc, axis=-1, keepdims=True)
        cen = acc - mu
        var = jnp.mean(cen * cen, axis=-1, keepdims=True)
        y = _silu(cen * lax.rsqrt(var + EPS) * lg_ref[...] + lb_ref[...])
        for l in slabs:
            ot_ref[l, rows, :] = y[:, l * LANES:(l + 1) * LANES]
    for q in range(SUBLANES):
        o_ref[q * GRID_W:(q + 1) * GRID_W, :] = jnp.concatenate(
            [ot_ref[l, pl.ds(q, GRID_W, stride=SUBLANES), :] for l in slabs], axis=1).astype(o_ref.dtype)


def _inproj_kernel(x_ref, sh_ref, sc_ref, g_ref, w_ref, cw_ref, cb_ref, lg_ref, lb_ref,
                   cv_ref, zl_ref, gg_ref, stg_ref, xt_ref, ot_ref):
    h = _rms_mod(x_ref[...], g_ref[...], sh_ref[0], sc_ref[0])
    z = _dot(h.astype(BF16), w_ref[...])
    c = cv_ref.shape[1]
    zl_ref[...] = z[:, 2 * c:3 * c]
    gg_ref[...] = _gelu_tanh(z[:, 3 * c:])
    u = z[:, :c] * _sigmoid(z[:, c:2 * c])
    _conformer_conv(u, cw_ref, cb_ref, lg_ref, lb_ref, cv_ref, stg_ref, xt_ref, ot_ref)


def _inproj_ctx(x2, shift, scale, g, w, tm):
    t, d = x2.shape
    n = w.shape[1]
    mod_spec = pl.BlockSpec((1, 1, d), lambda i: (0, 0, 0))
    return pl.pallas_call(
        _inproj_ctx_kernel,
        out_shape=jax.ShapeDtypeStruct((t, n), F32),
        grid=(t // tm,),
        in_specs=[pl.BlockSpec((tm, d), lambda i: (i, 0)), mod_spec, mod_spec,
                  pl.BlockSpec((1, d), lambda i: (0, 0)),
                  pl.BlockSpec((d, n), lambda i: (0, 0))],
        out_specs=pl.BlockSpec((tm, n), lambda i: (i, 0)),
        compiler_params=_cparams(("arbitrary",)),
        name="in_proj_ctx",
    )(x2, shift, scale, g, w)


def _inproj(x2, shift, scale, g, w, cw, cb, lg, lb, rows_per_mod):
    t, d = x2.shape
    n = w.shape[1]
    c = n // 4
    tm = SUBLANES * GRID_W
    blocks_per_mod = rows_per_mod // tm
    mod_spec = pl.BlockSpec((1, 1, d), lambda i: (i // blocks_per_mod, 0, 0))
    const = lambda a: pl.BlockSpec(a.shape, lambda i: (0,) * a.ndim)
    rows = pl.BlockSpec((tm, c), lambda i: (i, 0))
    return pl.pallas_call(
        _inproj_kernel,
        out_shape=[jax.ShapeDtypeStruct((t, c), BF16), jax.ShapeDtypeStruct((t, c), F32),
                   jax.ShapeDtypeStruct((t, c), F32)],
        grid=(t // tm,),
        in_specs=[pl.BlockSpec((tm, d), lambda i: (i, 0)), mod_spec, mod_spec, const(g), const(w),
                  const(cw), const(cb), const(lg), const(lb)],
        out_specs=[rows, rows, rows],
        scratch_shapes=[pltpu.VMEM((c // LANES, SUBLANES * (GRID_W + SUBLANES), LANES), F32),
                        pltpu.VMEM((c // LANES, tm, LANES), F32), pltpu.VMEM((c // LANES, tm, LANES), F32)],
        compiler_params=_cparams(("arbitrary",)),
        name="in_proj",
    )(x2, shift, scale, g, w, cw, cb, lg, lb)


def _lru_kernel(zl_ref, h0_ref, cw_ref, cb_ref, wg_ref, gb_ref, lam_ref, *rest, seq, final_only):
    if final_only:
        o_ref, pad_ref, af_ref, bf_ref, ab_ref, bb_ref = rest
        gg_ref = None
    else:
        gg_ref, o_ref, pad_ref, af_ref, bf_ref, ab_ref, bb_ref = rest
    cl = seq // SUBLANES
    pitch = cl + SUBLANES
    c = zl_ref.shape[2]
    a_refs = (af_ref, ab_ref)
    b_refs = (bf_ref, bb_ref)

    pad_ref[0:SUBLANES, :] = jnp.zeros((SUBLANES, c), F32)
    pad_ref[SUBLANES:SUBLANES + seq, :] = zl_ref[0]
    pad_ref[SUBLANES + seq:2 * SUBLANES + seq, :] = jnp.zeros((SUBLANES, c), F32)

    lam = lam_ref[...]
    nlam = -lam
    softplus = jnp.maximum(nlam, 0.0) + jnp.log1p(jnp.exp(-jnp.abs(nlam)))
    decay = -LRU_C * softplus
    wg = wg_ref[0]

    for j in range(SUBLANES):
        base = SUBLANES + j * cl
        ul = cb_ref[...] + jnp.zeros((cl, c), F32)
        for k in range(LRU_TAPS):
            off = base + k - 2
            ul = ul + cw_ref[k:k + 1, :] * pad_ref[off:off + cl, :]
        g = _dot(ul.astype(BF16), wg)
        for d in range(2):
            r = _sigmoid(g[:, (2 * d) * c:(2 * d + 1) * c] + gb_ref[2 * d:2 * d + 1, :])
            i = _sigmoid(g[:, (2 * d + 1) * c:(2 * d + 2) * c] + gb_ref[2 * d + 1:2 * d + 2, :])
            log_a = decay[d:d + 1, :] * r
            a = jnp.exp(log_a)
            m = jnp.maximum(-jnp.tanh(log_a) * (a * a + 1.0), 1e-12)
            mult = m * lax.rsqrt(m)
            a_refs[d][j * pitch:j * pitch + cl, :] = a
            b_refs[d][j * pitch:j * pitch + cl, :] = mult * i * ul

    def step(n, carry):
        hf, pf, hb, pb = carry
        tf = n
        tb = cl - 1 - n
        sf = pl.ds(tf, SUBLANES, stride=pitch)
        sb = pl.ds(tb, SUBLANES, stride=pitch)
        a_f = af_ref[sf, :]
        a_b = ab_ref[sb, :]
        hf = a_f * hf + bf_ref[sf, :]
        hb = a_b * hb + bb_ref[sb, :]
        pf = a_f * pf
        pb = a_b * pb
        bf_ref[sf, :] = hf
        af_ref[sf, :] = pf
        bb_ref[sb, :] = hb
        ab_ref[sb, :] = pb
        return hf, pf, hb, pb

    zero = jnp.zeros((SUBLANES, c), F32)
    one = jnp.ones((SUBLANES, c), F32)
    lax.fori_loop(0, cl, step, (zero, one, zero, one), unroll=8)

    cf = [None] * SUBLANES
    cbk = [None] * SUBLANES
    s = h0_ref[0, 0]
    for j in range(SUBLANES):
        cf[j] = s
        last = j * pitch + cl - 1
        s = bf_ref[last:last + 1, :] + af_ref[last:last + 1, :] * s
    final_f = s
    s = h0_ref[1, 0]
    for j in reversed(range(SUBLANES)):
        cbk[j] = s
        first = j * pitch
        s = bb_ref[first:first + 1, :] + ab_ref[first:first + 1, :] * s
    final_b = s

    if final_only:
        o_ref[0, 0] = final_f
        o_ref[1, 0] = final_b
        return
    for j in range(SUBLANES):
        rows = slice(j * pitch, j * pitch + cl)
        hf = bf_ref[rows, :] + af_ref[rows, :] * cf[j]
        hb = bb_ref[rows, :] + ab_ref[rows, :] * cbk[j]
        o_ref[0, j * cl:(j + 1) * cl, :] = ((hf + hb) * gg_ref[0, j * cl:(j + 1) * cl, :]).astype(o_ref.dtype)


def _lru(zl, gg, h0, cw, cb, wg, gb, lam, final_only, name):
    b, seq, c = zl.shape
    cbk = LRU_LANES
    cl = seq // SUBLANES
    pitch = cl + SUBLANES
    seq_spec = pl.BlockSpec((1, seq, cbk), lambda bi, ci: (bi, 0, ci))
    st_spec = pl.BlockSpec((2, 1, 1, cbk), lambda bi, ci: (0, bi, 0, ci))
    chan = lambda rows: pl.BlockSpec((rows, cbk), lambda bi, ci: (0, ci))
    in_specs = [seq_spec, st_spec, chan(LRU_TAPS), chan(1),
                pl.BlockSpec((1, cbk, 4 * cbk), lambda bi, ci: (ci, 0, 0)), chan(4), chan(2)]
    args = [zl, h0, cw, cb, wg, gb, lam]
    if final_only:
        out_shape = jax.ShapeDtypeStruct((2, b, 1, c), F32)
        out_spec = st_spec
    else:
        in_specs.append(seq_spec)
        args.append(gg)
        out_shape = jax.ShapeDtypeStruct((b, seq, c), BF16)
        out_spec = seq_spec
    coef = pltpu.VMEM((SUBLANES * pitch, cbk), F32)
    return pl.pallas_call(
        functools.partial(_lru_kernel, seq=seq, final_only=final_only),
        out_shape=out_shape,
        grid=(b, c // cbk),
        in_specs=in_specs,
        out_specs=out_spec,
        scratch_shapes=[pltpu.VMEM((seq + 2 * SUBLANES, cbk), F32), coef, coef, coef, coef],
        compiler_params=_cparams(("arbitrary", "arbitrary")),
        name=name,
    )(*args)


def _outproj_kernel(cv_ref, lr_ref, x_ref, g1_ref, sh_ref, sc_ref, ng_ref, wo_ref, rh_ref, rl_ref, rb_ref,
                    x1_ref, h2_ref, lg_ref):
    c = cv_ref.shape[1]
    y = _dot(cv_ref[...], wo_ref[0:c, :]) + _dot(lr_ref[...], wo_ref[c:2 * c, :])
    x1 = x_ref[...] + g1_ref[0] * y
    x1_ref[...] = x1
    h2 = _rms_mod(x1, ng_ref[...], sh_ref[0], sc_ref[0])
    h_hi, h_lo = _split_bf16(h2)
    _store_row_tiles(h2_ref, h_hi.astype(F32))
    lg_ref[...] = (_dot_nt(rh_ref[...], h_hi) + _dot_nt(rl_ref[...], h_hi) + _dot_nt(rh_ref[...], h_lo)
                   + rb_ref[...])


def _outproj(conv_l, lru_l, x2, g1, sh2, sc2, ng, wo, r_hi, r_lo, r_b, rows_per_mod, tm):
    t, d = x2.shape
    c = conv_l.shape[1]
    blocks_per_mod = rows_per_mod // tm
    mod_spec = pl.BlockSpec((1, 1, d), lambda i: (i // blocks_per_mod, 0, 0))
    full = lambda a: pl.BlockSpec(a.shape, lambda i: (0, 0))
    return pl.pallas_call(
        _outproj_kernel,
        out_shape=[jax.ShapeDtypeStruct((t, d), F32), jax.ShapeDtypeStruct((t * SUBLANES, LANES), F32),
                   jax.ShapeDtypeStruct((ROUTE_ROWS, t), F32)],
        grid=(t // tm,),
        in_specs=[pl.BlockSpec((tm, c), lambda i: (i, 0)), pl.BlockSpec((tm, c), lambda i: (i, 0)),
                  pl.BlockSpec((tm, d), lambda i: (i, 0)), mod_spec, mod_spec, mod_spec,
                  full(ng), full(wo), full(r_hi), full(r_lo), full(r_b)],
        out_specs=[pl.BlockSpec((tm, d), lambda i: (i, 0)), pl.BlockSpec((tm * SUBLANES, LANES), lambda i: (i, 0)),
                   pl.BlockSpec((ROUTE_ROWS, tm), lambda i: (0, i))],
        compiler_params=_cparams(("arbitrary",)),
        name="out_proj_router",
    )(conv_l, lru_l, x2, g1, sh2, sc2, ng, wo, r_hi, r_lo, r_b)


def _route_chunk(lg_ref, lanes):
    n = lanes.size
    e = EXPERTS_PER_GROUP
    lgrp = lg_ref[0:N_GROUPS, lanes]
    gidx = lax.broadcasted_iota(jnp.int32, (N_GROUPS, n), 0)
    m = jnp.max(lgrp, axis=0, keepdims=True)
    ex = jnp.exp(lgrp - m)
    pg = ex / jnp.sum(ex, axis=0, keepdims=True)
    p_grp = jnp.max(pg, axis=0, keepdims=True)
    grp = jnp.min(jnp.where(pg == p_grp, gidx, N_GROUPS), axis=0, keepdims=True)
    le = jnp.zeros((e, n), F32)
    for g in range(N_GROUPS):
        rows = lg_ref[EXPERT_ROW0 + g * e:EXPERT_ROW0 + (g + 1) * e, lanes]
        le = jnp.where(grp == g, rows, le)
    m = jnp.max(le, axis=0, keepdims=True)
    ex = jnp.exp(le - m)
    pe = ex / jnp.sum(ex, axis=0, keepdims=True)
    eidx = lax.broadcasted_iota(jnp.int32, (e, n), 0)
    p1 = jnp.max(pe, axis=0, keepdims=True)
    i1 = jnp.min(jnp.where(pe == p1, eidx, e), axis=0, keepdims=True)
    pe2 = jnp.where(eidx == i1, -1.0, pe)
    p2 = jnp.max(pe2, axis=0, keepdims=True)
    i2 = jnp.min(jnp.where(pe2 == p2, eidx, e), axis=0, keepdims=True)
    denom = p1 + p2
    base = grp * e
    return (base + i1, base + i2), (p_grp * p1 / denom, p_grp * p2 / denom)


def _route_kernel(lg_ref, gate_ref, dest_ref, cnt_ref, run_ref, *, block_rows):
    p = pl.program_id(0)
    i = pl.program_id(1)
    tl = lg_ref.shape[1]
    chunk = 2 * LANES
    ne = N_EXPERTS

    @pl.when((p == 0) & (i == 0))
    def _():
        run_ref[...] = jnp.zeros_like(run_ref)

    @pl.when((p == 1) & (i == 0))
    def _():
        counts = run_ref[...]
        cnt_ref[...] = counts
        padded = jnp.ceil(counts * (1.0 / block_rows)) * block_rows
        rows = []
        start = jnp.zeros((1, LANES), F32)
        for ex in range(ne):
            rows.append(start)
            start = start + padded[ex:ex + 1, :]
        run_ref[...] = jnp.concatenate(rows, axis=0)

    tri = (lax.broadcasted_iota(jnp.int32, (chunk, chunk), 0)
           <= lax.broadcasted_iota(jnp.int32, (chunk, chunk), 1)).astype(BF16)
    eidx = lax.broadcasted_iota(jnp.int32, (ne, chunk), 0)
    for cix in range(tl // chunk):
        lanes = pl.ds(cix * chunk, chunk)
        eids, gates = _route_chunk(lg_ref, lanes)
        dests = []
        for k in range(TOP_K):
            onehot = eidx == eids[k]
            cum = _dot(onehot.astype(BF16), tri)
            run = run_ref[...][:, 0:1]
            dests.append(jnp.sum(jnp.where(onehot, run + cum - 1.0, 0.0), axis=0, keepdims=True))
            run_ref[...] = run_ref[...] + cum[:, chunk - 1:chunk]
        gate_ref[:, lanes] = jnp.concatenate(gates, axis=0)
        dest_ref[:, lanes] = jnp.concatenate(dests, axis=0).astype(jnp.int32)


def _route(logits_t, block_rows):
    rows, t = logits_t.shape
    tl = 2048
    blk = pl.BlockSpec((TOP_K, tl), lambda p, i: (0, i * p))
    return pl.pallas_call(
        functools.partial(_route_kernel, block_rows=block_rows),
        out_shape=[jax.ShapeDtypeStruct((TOP_K, t), F32), jax.ShapeDtypeStruct((TOP_K, t), jnp.int32),
                   jax.ShapeDtypeStruct((N_EXPERTS, LANES), F32)],
        grid=(2, t // tl),
        in_specs=[pl.BlockSpec((rows, tl), lambda p, i: (0, i))],
        out_specs=[blk, blk, pl.BlockSpec((N_EXPERTS, LANES), lambda p, i: (0, 0))],
        scratch_shapes=[pltpu.VMEM((N_EXPERTS, LANES), F32)],
        compiler_params=_cparams(("arbitrary", "arbitrary")),
        name="route",
    )(logits_t)


def _tables_kernel(cnt_ref, dest_ref, src_ref, blk_e_ref, nxt_ref, nact_ref, *, n_tok, bm):
    i = pl.program_id(0)
    cb = dest_ref.shape[0]
    n_rows = src_ref.shape[0]
    n_blocks = blk_e_ref.shape[0]
    shift = bm.bit_length() - 1

    def spare(r):
        return TOP_K * n_tok + ((r >> shift) & 1) * bm + (r & (bm - 1))

    @pl.when(i == 0)
    def _():
        def per_expert(ex, run):
            cnt = cnt_ref[ex]
            padded = ((cnt + (bm - 1)) >> shift) << shift

            end_blk = (run + padded) >> shift

            def fill_blk(b, carry):
                blk_e_ref[b] = ex
                nxt_ref[b] = end_blk
                return carry
            lax.fori_loop(run >> shift, end_blk, fill_blk, 0)

            def fill_pad(r, carry):
                src_ref[r] = spare(r)
                return carry
            lax.fori_loop(run + cnt, run + padded, fill_pad, 0)
            return run + padded
        total = lax.fori_loop(0, N_EXPERTS, per_expert, 0)
        nact_ref[0] = total >> shift

        def fill_blk(b, carry):
            blk_e_ref[b] = N_EXPERTS - 1
            nxt_ref[b] = n_blocks
            return carry
        lax.fori_loop(total >> shift, n_blocks, fill_blk, 0)

        def fill_pad(r, carry):
            src_ref[r] = spare(r)
            return carry
        lax.fori_loop(total, n_rows, fill_pad, 0)

    base = i * cb

    def invert(j, carry):
        src_ref[dest_ref[j]] = base + j
        return carry
    lax.fori_loop(0, cb, invert, 0, unroll=16)


def _tables(counts, dest_flat, n_tok, n_blocks, bm):
    cb = 4096
    smem = functools.partial(pl.BlockSpec, memory_space=pltpu.SMEM)
    return pl.pallas_call(
        functools.partial(_tables_kernel, n_tok=n_tok, bm=bm),
        out_shape=[jax.ShapeDtypeStruct((n_blocks * bm,), jnp.int32),
                   jax.ShapeDtypeStruct((n_blocks,), jnp.int32),
                   jax.ShapeDtypeStruct((n_blocks,), jnp.int32),
                   jax.ShapeDtypeStruct((1,), jnp.int32)],
        grid=(dest_flat.shape[0] // cb,),
        in_specs=[smem(), smem((cb,), lambda i: (i,))],
        out_specs=[smem(), smem(), smem(), smem()],
        compiler_params=_cparams(("arbitrary",)),
        name="dispatch_tables",
    )(counts, dest_flat)


def _moe_kernel(blk_e_ref, nxt_ref, nact_ref, src_ref,
                w1_hbm, w3_hbm, w2_hbm, h_hbm, y_hbm,
                wf1, wf3, wf2, w1b, w3b, w2b, xbuf, ybuf, gsem, ssem, wsem, *, n_tok):
    i = pl.program_id(0)
    nact = nact_ref[0]
    n_blocks = blk_e_ref.shape[0]
    bm = xbuf.shape[1] // SUBLANES
    de = w1b.shape[1]
    slot = lax.rem(i, 2)
    other = 1 - slot
    chunk = 2 * LANES
    n_chunks = de // chunk
    rows_per_chunk = 2 * bm // n_chunks

    def tile(row):
        return pl.ds(pl.multiple_of(row * SUBLANES, SUBLANES), SUBLANES)

    def token_of(n):
        if n_tok & (n_tok - 1) == 0:
            return n & (n_tok - 1)
        return lax.rem(n, n_tok)

    def gather_start(blk, sl, rows):
        for r in rows:
            tok = token_of(src_ref[blk * bm + r])
            pltpu.make_async_copy(h_hbm.at[tile(tok), :], xbuf.at[sl, tile(r), :], gsem.at[sl]).start()

    def gather_wait(sl):
        pltpu.make_async_copy(h_hbm.at[pl.ds(0, bm * SUBLANES), :], xbuf.at[sl], gsem.at[sl]).wait()

    def scatter_wait(sl):
        pltpu.make_async_copy(ybuf.at[sl], y_hbm.at[pl.ds(0, bm * SUBLANES), :], ssem.at[sl]).wait()

    def spare_row(set_id, r):
        return TOP_K * n_tok + set_id * bm + r

    @pl.when(i < nact)
    def _():
        @pl.when(i == 0)
        def _():
            gather_start(0, 0, range(bm))
            ybuf[...] = jnp.zeros_like(ybuf)
            for sl in range(2):
                spare = pltpu.make_async_copy(
                    ybuf.at[sl], y_hbm.at[pl.ds(spare_row(sl, 0) * SUBLANES, bm * SUBLANES), :], ssem.at[sl])
                spare.start()
                spare.wait()

        def weight_copies(e):
            return [pltpu.make_async_copy(w_hbm.at[e], stage, wsem.at[k])
                    for k, (w_hbm, stage) in enumerate(((w1_hbm, wf1), (w3_hbm, wf3), (w2_hbm, wf2)))]

        this_e = blk_e_ref[i]

        @pl.when(i == 0)
        def _():
            for cp in weight_copies(this_e):
                cp.start()

        @pl.when((i == 0) | (this_e != blk_e_ref[jnp.maximum(i - 1, 0)]))
        def _():
            for cp in weight_copies(this_e):
                cp.wait()
            for stage, wb in ((wf1, w1b), (wf3, w3b), (wf2, w2b)):
                def cast_rows(rb, carry, stage=stage, wb=wb):
                    rows = pl.ds(pl.multiple_of(rb * LANES, LANES), LANES)
                    wb[rows, :] = stage[rows, :].astype(BF16)
                    return carry
                lax.fori_loop(0, stage.shape[0] // LANES, cast_rows, 0)
            nxt_blk = nxt_ref[i]

            @pl.when(nxt_blk < nact)
            def _():
                for cp in weight_copies(blk_e_ref[nxt_blk]):
                    cp.start()

        gather_wait(slot)
        xb = _load_row_tiles(xbuf.at[slot], bm).astype(BF16)
        nxt = jnp.minimum(i + 1, n_blocks - 1)
        prv = jnp.maximum(i - 1, 0)
        has_prev = i > 0
        y = None
        for c in range(n_chunks):
            cols = slice(c * chunk, (c + 1) * chunk)
            a = _dot(xb, w1b[:, cols])
            g = _dot(xb, w3b[:, cols])
            act = (_silu(a) * g).astype(BF16)
            part = _dot(act, w2b[cols, :])
            y = part if y is None else y + part
            if c < n_chunks // 2:
                gather_start(nxt, other, range(c * rows_per_chunk, (c + 1) * rows_per_chunk))
            else:
                first = (c - n_chunks // 2) * rows_per_chunk
                for r in range(first, first + rows_per_chunk):
                    n = jnp.where(has_prev, src_ref[prv * bm + r], spare_row(1, r))
                    pltpu.make_async_copy(ybuf.at[other, tile(r), :], y_hbm.at[tile(n), :], ssem.at[other]).start()

        @pl.when(i >= 1)
        def _():
            scatter_wait(slot)

        _store_row_tiles(ybuf.at[slot], y)

        @pl.when(i == nact - 1)
        def _():
            for r in range(bm):
                n = src_ref[i * bm + r]
                pltpu.make_async_copy(ybuf.at[slot, tile(r), :], y_hbm.at[tile(n), :], ssem.at[slot]).start()
            scatter_wait(other)
            scatter_wait(slot)
            gather_wait(other)


def _moe(blk_e, nxt_blk, nact, src, w1, w3, w2, h2_tiles, n_blocks):
    t = h2_tiles.shape[0] // SUBLANES
    d = w1.shape[1]
    de = w1.shape[2]
    bm = MOE_BLOCK
    hbm = pl.BlockSpec(memory_space=pl.ANY)
    grid_spec = pltpu.PrefetchScalarGridSpec(
        num_scalar_prefetch=4,
        grid=(n_blocks,),
        in_specs=[hbm, hbm, hbm, hbm],
        out_specs=hbm,
        scratch_shapes=[pltpu.VMEM((d, de), F32), pltpu.VMEM((d, de), F32), pltpu.VMEM((de, d), F32),
                        pltpu.VMEM((d, de), BF16), pltpu.VMEM((d, de), BF16), pltpu.VMEM((de, d), BF16),
                        pltpu.VMEM((2, bm * SUBLANES, LANES), F32), pltpu.VMEM((2, bm * SUBLANES, LANES), F32),
                        pltpu.SemaphoreType.DMA((2,)), pltpu.SemaphoreType.DMA((2,)),
                        pltpu.SemaphoreType.DMA((3,))],
    )
    return pl.pallas_call(
        functools.partial(_moe_kernel, n_tok=t),
        out_shape=jax.ShapeDtypeStruct(((TOP_K * t + 2 * bm) * SUBLANES, LANES), F32),
        grid_spec=grid_spec,
        compiler_params=_cparams(("arbitrary",)),
        name="moe_experts",
    )(blk_e, nxt_blk, nact, src, w1, w3, w2, h2_tiles)


def _final_kernel(x1_ref, y0_ref, y1_ref, gt_ref, g2_ref, fg_ref, o_ref):
    gt = gt_ref[...]
    tm = x1_ref.shape[0]
    y2 = gt[:, 0:1] * _load_row_tiles(y0_ref, tm) + gt[:, 1:2] * _load_row_tiles(y1_ref, tm)
    x = x1_ref[...] + g2_ref[0] * y2
    o_ref[...] = x * lax.rsqrt(jnp.mean(x * x, axis=-1, keepdims=True) + EPS) * fg_ref[...]


def _final(x1, y, gates_tok, g2, fg, rows_per_mod, tm):
    t, d = x1.shape
    nb = t // tm
    blocks_per_mod = rows_per_mod // tm
    return pl.pallas_call(
        _final_kernel,
        out_shape=jax.ShapeDtypeStruct((t, d), F32),
        grid=(nb,),
        in_specs=[pl.BlockSpec((tm, d), lambda i: (i, 0)),
                  pl.BlockSpec((tm * SUBLANES, LANES), lambda i: (i, 0)),
                  pl.BlockSpec((tm * SUBLANES, LANES), lambda i: (i + nb, 0)),
                  pl.BlockSpec((tm, TOP_K), lambda i: (i, 0)),
                  pl.BlockSpec((1, 1, d), lambda i: (i // blocks_per_mod, 0, 0)),
                  pl.BlockSpec((1, d), lambda i: (0, 0))],
        out_specs=pl.BlockSpec((tm, d), lambda i: (i, 0)),
        compiler_params=_cparams(("arbitrary",)),
        name="combine_final_norm",
    )(x1, y, y, gates_tok, g2, fg)


def kernel(x, c, ctx, c_ctx, w_ada, b_ada, norm1_g, norm2_g, w_in, conv_dw, conv_b, conv_ln_g, conv_ln_b,
           lru_conv_w, lru_conv_b, lru_wa, lru_ba, lru_wx, lru_bx, lru_lam, w_out,
           router_wg, router_bg, router_we, router_be, w1, w3, w2, final_g):
    assert w_ada.shape[0] == 1, "single-layer block"
    assert x.shape[2] == SUBLANES * LANES, "row-tile layout needs one (8, 128) tile per token row"
    b, s, d = x.shape
    n_ctx = ctx.shape[1]
    t = b * s
    cc = conv_dw.shape[2]
    lw = lru_conv_w.shape[2]

    c_rows = jnp.zeros((SUBLANES, d), F32).at[:b].set(c).at[b].set(c_ctx)
    mod = _ada(c_rows, w_ada[0], b_ada)
    mod_l = mod[:b].reshape(b, 6, 1, d)
    sh1, sc1, g1, sh2, sc2, g2 = (mod_l[:, k] for k in range(6))
    mod_c = mod[b].reshape(6, 1, 1, d)
    csh1, csc1 = mod_c[0], mod_c[1]

    w_in_b = w_in[0].astype(BF16)
    w_out_b = w_out[0].astype(BF16)
    heads_per_blk = LRU_LANES // lru_wa.shape[3]
    n_cblk = lw // LRU_LANES

    def blockdiag(wh):
        hd = wh.shape[1]
        wh = wh.reshape(n_cblk, heads_per_blk, hd, hd)
        eye = jnp.eye(heads_per_blk, dtype=wh.dtype)
        return jnp.einsum("chij,hg->chigj", wh, eye).reshape(n_cblk, LRU_LANES, LRU_LANES)

    wg = jnp.concatenate([blockdiag(lru_wa[0, 0]), blockdiag(lru_wx[0, 0]),
                          blockdiag(lru_wa[0, 1]), blockdiag(lru_wx[0, 1])], axis=2).astype(BF16)
    gb = jnp.stack([lru_ba[0, 0], lru_bx[0, 0], lru_ba[0, 1], lru_bx[0, 1]])
    lam = lru_lam[0]

    zc = _inproj_ctx(ctx.reshape(b * n_ctx, d), csh1, csc1, norm1_g, w_in_b[:, 2 * cc:2 * cc + lw], n_ctx)
    h0 = _lru(zc.reshape(b, n_ctx, lw), None, jnp.zeros((2, b, 1, lw), F32),
              lru_conv_w[0], lru_conv_b, wg, gb, lam, True, "rglru_ctx")

    x2 = x.reshape(t, d)
    conv_w_rep = jnp.repeat(conv_dw[0], SUBLANES, axis=0).reshape(CONV_TAPS * SUBLANES, cc // LANES, LANES)
    conv_w_rep = conv_w_rep.transpose(1, 0, 2)
    conv_l, zl, gg = _inproj(x2, sh1, sc1, norm1_g, w_in_b, conv_w_rep, conv_b, conv_ln_g, conv_ln_b, s)
    lru_l = _lru(zl.reshape(b, s, lw), gg.reshape(b, s, lw), h0,
                 lru_conv_w[0], lru_conv_b, wg, gb, lam, False, "rglru")

    wr = jnp.zeros((ROUTE_ROWS, d), F32)
    wr = wr.at[:N_GROUPS].set(router_wg[0].T)
    wr = wr.at[EXPERT_ROW0:EXPERT_ROW0 + N_EXPERTS].set(router_we[0].reshape(d, N_EXPERTS).T)
    rb = jnp.zeros((ROUTE_ROWS, 1), F32)
    rb = rb.at[:N_GROUPS, 0].set(router_bg[0])
    rb = rb.at[EXPERT_ROW0:EXPERT_ROW0 + N_EXPERTS, 0].set(router_be[0].reshape(-1))
    r_hi, r_lo = _split_bf16(wr)
    x1, h2, logits_t = _outproj(conv_l, lru_l.reshape(t, lw), x2, g1, sh2, sc2, norm2_g, w_out_b,
                                r_hi, r_lo, rb, s, TOKEN_BLOCK)

    gates, dest, counts = _route(logits_t, MOE_BLOCK)
    n_blocks = (TOP_K * t) // MOE_BLOCK + N_EXPERTS
    src, blk_e, nxt_blk, nact = _tables(counts[:, 0].astype(jnp.int32), dest.reshape(-1), t, n_blocks, MOE_BLOCK)
    y = _moe(blk_e, nxt_blk, nact, src, w1[0], w3[0], w2[0], h2, n_blocks)
    out = _final(x1, y, gates.T, g2, final_g.reshape(1, d), s, TOKEN_BLOCK)
    return out.reshape(b, s, d)
```

```python
import functools

import jax
import jax.numpy as jnp
from jax import lax
from jax.experimental import pallas as pl
from jax.experimental.pallas import tpu as pltpu

F32 = jnp.float32
BF16 = jnp.bfloat16

EPS = 1e-6
CONV_TAPS = 31
LRU_TAPS = 4
LRU_C = 8.0
GRID_W = 64
N_GROUPS = 4
EXPERTS_PER_GROUP = 8
N_EXPERTS = N_GROUPS * EXPERTS_PER_GROUP
TOP_K = 2

SUBLANES = 8
LANES = 128
TOKEN_BLOCK = 512
LRU_LANES = 128
ROUTE_ROWS = 128
EXPERT_ROW0 = 8
MOE_BLOCK = 256
VMEM_LIMIT = 48 * 1024 * 1024


def _cparams(sem, vmem=VMEM_LIMIT):
    return pltpu.CompilerParams(dimension_semantics=sem, vmem_limit_bytes=vmem)


def _split_bf16(a):
    hi = a.astype(BF16)
    lo = (a - hi.astype(F32)).astype(BF16)
    return hi, lo


def _dot(a, b):
    return jnp.dot(a, b, preferred_element_type=F32)


def _dot_nt(a, b):
    return lax.dot_general(a, b, (((1,), (1,)), ((), ())), preferred_element_type=F32)


def _sigmoid(x):
    return 0.5 * jnp.tanh(0.5 * x) + 0.5


def _silu(x):
    return x * _sigmoid(x)


def _gelu_tanh(x):
    c = 0.7978845608028654
    return 0.5 * x * (1.0 + jnp.tanh(c * (x + 0.044715 * (x * x * x))))


def _store_row_tiles(ref, x):
    rows = x.shape[0]
    for s in range(SUBLANES):
        ref[pl.ds(s, rows, stride=SUBLANES), :] = x[:, s * LANES:(s + 1) * LANES]


def _load_row_tiles(ref, rows):
    return jnp.concatenate([ref[pl.ds(s, rows, stride=SUBLANES), :] for s in range(SUBLANES)], axis=1)


def _rms_mod(x, g, shift, scale):
    y = x * lax.rsqrt(jnp.mean(x * x, axis=-1, keepdims=True) + EPS)
    return (y * g) * (1.0 + scale) + shift


def _ada_kernel(c_ref, w_ref, b_ref, o_ref):
    a = _silu(c_ref[...])
    a_hi, a_lo = _split_bf16(a)
    w_hi, w_lo = _split_bf16(w_ref[...])
    o_ref[...] = _dot(a_hi, w_hi) + _dot(a_lo, w_hi) + _dot(a_hi, w_lo) + b_ref[...]


def _ada(c_rows, w, b):
    m, d = c_rows.shape
    n = w.shape[1]
    bn = 768
    return pl.pallas_call(
        _ada_kernel,
        out_shape=jax.ShapeDtypeStruct((m, n), F32),
        grid=(n // bn,),
        in_specs=[pl.BlockSpec((m, d), lambda j: (0, 0)),
                  pl.BlockSpec((d, bn), lambda j: (0, j)),
                  pl.BlockSpec((1, bn), lambda j: (0, j))],
        out_specs=pl.BlockSpec((m, bn), lambda j: (0, j)),
        compiler_params=_cparams(("arbitrary",)),
        name="ada_mod",
    )(c_rows, w, b)


def _inproj_ctx_kernel(x_ref, sh_ref, sc_ref, g_ref, w_ref, o_ref):
    h = _rms_mod(x_ref[...], g_ref[...], sh_ref[0], sc_ref[0])
    o_ref[...] = _dot(h.astype(BF16), w_ref[...])


def _conformer_conv(u, cw_ref, cb_ref, lg_ref, lb_ref, o_ref, stg_ref, xt_ref, ot_ref):
    c = u.shape[1]
    slabs = range(c // LANES)
    half = CONV_TAPS // 2
    pitch = GRID_W + SUBLANES
    group = SUBLANES
    for l in slabs:
        for q in range(SUBLANES):
            stg_ref[l, q * pitch:q * pitch + GRID_W, :] = u[q * GRID_W:(q + 1) * GRID_W, l * LANES:(l + 1) * LANES]

    def conv_slab(l, carry):
        for t in range(GRID_W):
            xt_ref[l, t * SUBLANES:(t + 1) * SUBLANES, :] = stg_ref[l, pl.ds(t, SUBLANES, stride=pitch), :]
        for t0 in range(0, GRID_W, group):
            accs = [None] * group
            for k in range(CONV_TAPS):
                srcs = [t0 + j + k - half for j in range(group)]
                if not any(0 <= sidx < GRID_W for sidx in srcs):
                    continue
                wk = cw_ref[l, k * SUBLANES:(k + 1) * SUBLANES, :]
                for j, sidx in enumerate(srcs):
                    if 0 <= sidx < GRID_W:
                        term = wk * xt_ref[l, sidx * SUBLANES:(sidx + 1) * SUBLANES, :]
                        accs[j] = term if accs[j] is None else accs[j] + term
            for j in range(group):
                ot_ref[l, (t0 + j) * SUBLANES:(t0 + j + 1) * SUBLANES, :] = accs[j]
        return carry
    lax.fori_loop(0, c // LANES, conv_slab, 0)
    rows_per_pass = group * SUBLANES
    for r0 in range(0, GRID_W * SUBLANES, rows_per_pass):
        rows = slice(r0, r0 + rows_per_pass)
        acc = jnp.concatenate([ot_ref[l, rows, :] for l in slabs], axis=1) + cb_ref[...]
        mu = jnp.mean(acc, axis=-1, keepdims=True)
        cen = acc - mu
        var = jnp.mean(cen * cen, axis=-1, keepdims=True)
        y = _silu(cen * lax.rsqrt(var + EPS) * lg_ref[...] + lb_ref[...])
        for l in slabs:
            ot_ref[l, rows, :] = y[:, l * LANES:(l + 1) * LANES]
    for q in range(SUBLANES):
        o_ref[q * GRID_W:(q + 1) * GRID_W, :] = jnp.concatenate(
            [ot_ref[l, pl.ds(q, GRID_W, stride=SUBLANES), :] for l in slabs], axis=1).astype(o_ref.dtype)


def _inproj_kernel(x_ref, sh_ref, sc_ref, g_ref, w_ref, cw_ref, cb_ref, lg_ref, lb_ref,
                   cv_ref, zl_ref, gg_ref, stg_ref, xt_ref, ot_ref):
    h = _rms_mod(x_ref[...], g_ref[...], sh_ref[0], sc_ref[0])
    z = _dot(h.astype(BF16), w_ref[...])
    c = cv_ref.shape[1]
    zl_ref[...] = z[:, 2 * c:3 * c]
    gg_ref[...] = _gelu_tanh(z[:, 3 * c:])
    u = z[:, :c] * _sigmoid(z[:, c:2 * c])
    _conformer_conv(u, cw_ref, cb_ref, lg_ref, lb_ref, cv_ref, stg_ref, xt_ref, ot_ref)


def _inproj_ctx(x2, shift, scale, g, w, tm):
    t, d = x2.shape
    n = w.shape[1]
    mod_spec = pl.BlockSpec((1, 1, d), lambda i: (0, 0, 0))
    return pl.pallas_call(
        _inproj_ctx_kernel,
        out_shape=jax.ShapeDtypeStruct((t, n), F32),
        grid=(t // tm,),
        in_specs=[pl.BlockSpec((tm, d), lambda i: (i, 0)), mod_spec, mod_spec,
                  pl.BlockSpec((1, d), lambda i: (0, 0)),
                  pl.BlockSpec((d, n), lambda i: (0, 0))],
        out_specs=pl.BlockSpec((tm, n), lambda i: (i, 0)),
        compiler_params=_cparams(("arbitrary",)),
        name="in_proj_ctx",
    )(x2, shift, scale, g, w)


def _inproj(x2, shift, scale, g, w, cw, cb, lg, lb, rows_per_mod):
    t, d = x2.shape
    n = w.shape[1]
    c = n // 4
    tm = SUBLANES * GRID_W
    blocks_per_mod = rows_per_mod // tm
    mod_spec = pl.BlockSpec((1, 1, d), lambda i: (i // blocks_per_mod, 0, 0))
    const = lambda a: pl.BlockSpec(a.shape, lambda i: (0,) * a.ndim)
    rows = pl.BlockSpec((tm, c), lambda i: (i, 0))
    return pl.pallas_call(
        _inproj_kernel,
        out_shape=[jax.ShapeDtypeStruct((t, c), BF16), jax.ShapeDtypeStruct((t, c), F32),
                   jax.ShapeDtypeStruct((t, c), F32)],
        grid=(t // tm,),
        in_specs=[pl.BlockSpec((tm, d), lambda i: (i, 0)), mod_spec, mod_spec, const(g), const(w),
                  const(cw), const(cb), const(lg), const(lb)],
        out_specs=[rows, rows, rows],
        scratch_shapes=[pltpu.VMEM((c // LANES, SUBLANES * (GRID_W + SUBLANES), LANES), F32),
                        pltpu.VMEM((c // LANES, tm, LANES), F32), pltpu.VMEM((c // LANES, tm, LANES), F32)],
        compiler_params=_cparams(("arbitrary",)),
        name="in_proj",
    )(x2, shift, scale, g, w, cw, cb, lg, lb)


def _lru_kernel(zl_ref, h0_ref, cw_ref, cb_ref, wg_ref, gb_ref, lam_ref, *rest, seq, final_only):
    if final_only:
        o_ref, stage_ref, zt_ref, af_ref, bf_ref, ab_ref, bb_ref, pf_ref, hf_ref, pb_ref, hb_ref = rest
        gg_ref = None
    else:
        gg_ref, o_ref, stage_ref, zt_ref, af_ref, bf_ref, ab_ref, bb_ref, pf_ref, hf_ref, pb_ref, hb_ref = rest
    cl = seq // SUBLANES
    pitch = cl + SUBLANES
    c = zl_ref.shape[2]
    a_refs = (af_ref, ab_ref)
    b_refs = (bf_ref, bb_ref)
    halo = 2

    def tile_rows(t):
        return pl.ds(pl.multiple_of(t * SUBLANES, SUBLANES), SUBLANES)

    for j in range(SUBLANES):
        stage_ref[j * pitch:j * pitch + cl, :] = zl_ref[0, j * cl:(j + 1) * cl, :]

    def to_chunk_layout(t, carry):
        zt_ref[tile_rows(t + halo), :] = stage_ref[pl.ds(t, SUBLANES, stride=pitch), :]
        return carry
    lax.fori_loop(0, cl, to_chunk_layout, 0, unroll=8)

    sub = lax.broadcasted_iota(jnp.int32, (SUBLANES, c), 0)
    for t_src, t_dst in ((cl - 2, -2), (cl - 1, -1)):
        v = pltpu.roll(zt_ref[(t_src + halo) * SUBLANES:(t_src + halo + 1) * SUBLANES, :], 1, 0)
        zt_ref[(t_dst + halo) * SUBLANES:(t_dst + halo + 1) * SUBLANES, :] = jnp.where(sub == 0, 0.0, v)
    v = pltpu.roll(zt_ref[halo * SUBLANES:(halo + 1) * SUBLANES, :], SUBLANES - 1, 0)
    zt_ref[(cl + halo) * SUBLANES:(cl + halo + 1) * SUBLANES, :] = jnp.where(sub == SUBLANES - 1, 0.0, v)

    lam = lam_ref[...]
    nlam = -lam
    softplus = jnp.maximum(nlam, 0.0) + jnp.log1p(jnp.exp(-jnp.abs(nlam)))
    decay = -LRU_C * softplus
    wg = wg_ref[0]

    piece = cl
    for p0 in range(0, cl * SUBLANES, piece):
        ul = cb_ref[...] + jnp.zeros((piece, c), F32)
        for k in range(LRU_TAPS):
            off = p0 + k * SUBLANES
            ul = ul + cw_ref[k:k + 1, :] * zt_ref[off:off + piece, :]
        g = _dot(ul.astype(BF16), wg)
        for d in range(2):
            r = _sigmoid(g[:, (2 * d) * c:(2 * d + 1) * c] + gb_ref[2 * d:2 * d + 1, :])
            i = _sigmoid(g[:, (2 * d + 1) * c:(2 * d + 2) * c] + gb_ref[2 * d + 1:2 * d + 2, :])
            log_a = decay[d:d + 1, :] * r
            a = jnp.exp(log_a)
            m = jnp.maximum(-jnp.tanh(log_a) * (a * a + 1.0), 1e-12)
            mult = m * lax.rsqrt(m)
            a_refs[d][p0:p0 + piece, :] = a
            b_refs[d][p0:p0 + piece, :] = mult * i * ul

    def step(n, carry):
        hf, pf, hb, pb = carry
        sf = tile_rows(n)
        sb = tile_rows(cl - 1 - n)
        a_f = af_ref[sf, :]
        a_b = ab_ref[sb, :]
        hf = a_f * hf + bf_ref[sf, :]
        hb = a_b * hb + bb_ref[sb, :]
        pf = a_f * pf
        pb = a_b * pb
        hf_ref[sf, :] = hf
        pf_ref[sf, :] = pf
        hb_ref[sb, :] = hb
        pb_ref[sb, :] = pb
        return hf, pf, hb, pb

    zero = jnp.zeros((SUBLANES, c), F32)
    one = jnp.ones((SUBLANES, c), F32)
    lax.fori_loop(0, cl, step, (zero, one, zero, one), unroll=8)

    cf = [None] * SUBLANES
    cbk = [None] * SUBLANES
    s = h0_ref[0, 0]
    for j in range(SUBLANES):
        cf[j] = s
        last = (cl - 1) * SUBLANES + j
        s = hf_ref[last:last + 1, :] + pf_ref[last:last + 1, :] * s
    final_f = s
    s = h0_ref[1, 0]
    for j in reversed(range(SUBLANES)):
        cbk[j] = s
        s = hb_ref[j:j + 1, :] + pb_ref[j:j + 1, :] * s
    final_b = s

    if final_only:
        o_ref[0, 0] = final_f
        o_ref[1, 0] = final_b
        return
    carry_f = jnp.concatenate(cf, axis=0)
    carry_b = jnp.concatenate(cbk, axis=0)

    def to_natural(t, carry):
        rows = tile_rows(t)
        h = (hf_ref[rows, :] + pf_ref[rows, :] * carry_f) + (hb_ref[rows, :] + pb_ref[rows, :] * carry_b)
        stage_ref[pl.ds(t, SUBLANES, stride=pitch), :] = h
        return carry
    lax.fori_loop(0, cl, to_natural, 0, unroll=8)
    for j in range(SUBLANES):
        h = stage_ref[j * pitch:j * pitch + cl, :]
        o_ref[0, j * cl:(j + 1) * cl, :] = (h * gg_ref[0, j * cl:(j + 1) * cl, :]).astype(o_ref.dtype)


def _lru(zl, gg, h0, cw, cb, wg, gb, lam, final_only, name):
    b, seq, c = zl.shape
    cbk = LRU_LANES
    cl = seq // SUBLANES
    pitch = cl + SUBLANES
    seq_spec = pl.BlockSpec((1, seq, cbk), lambda bi, ci: (bi, 0, ci))
    st_spec = pl.BlockSpec((2, 1, 1, cbk), lambda bi, ci: (0, bi, 0, ci))
    chan = lambda rows: pl.BlockSpec((rows, cbk), lambda bi, ci: (0, ci))
    in_specs = [seq_spec, st_spec, chan(LRU_TAPS), chan(1),
                pl.BlockSpec((1, cbk, 4 * cbk), lambda bi, ci: (ci, 0, 0)), chan(4), chan(2)]
    args = [zl, h0, cw, cb, wg, gb, lam]
    if final_only:
        out_shape = jax.ShapeDtypeStruct((2, b, 1, c), F32)
        out_spec = st_spec
    else:
        in_specs.append(seq_spec)
        args.append(gg)
        out_shape = jax.ShapeDtypeStruct((b, seq, c), BF16)
        out_spec = seq_spec
    coef = pltpu.VMEM((seq, cbk), F32)
    return pl.pallas_call(
        functools.partial(_lru_kernel, seq=seq, final_only=final_only),
        out_shape=out_shape,
        grid=(b, c // cbk),
        in_specs=in_specs,
        out_specs=out_spec,
        scratch_shapes=[pltpu.VMEM((SUBLANES * pitch, cbk), F32),
                        pltpu.VMEM((seq + 2 * SUBLANES * SUBLANES, cbk), F32)] + [coef] * 8,
        compiler_params=_cparams(("arbitrary", "arbitrary")),
        name=name,
    )(*args)


def _outproj_kernel(cv_ref, lr_ref, x_ref, g1_ref, sh_ref, sc_ref, ng_ref, wo_ref, rh_ref, rl_ref, rb_ref,
                    x1_ref, h2_ref, lg_ref):
    c = cv_ref.shape[1]
    y = _dot(cv_ref[...], wo_ref[0:c, :]) + _dot(lr_ref[...], wo_ref[c:2 * c, :])
    x1 = x_ref[...] + g1_ref[0] * y
    x1_ref[...] = x1
    h2 = _rms_mod(x1, ng_ref[...], sh_ref[0], sc_ref[0])
    h_hi, h_lo = _split_bf16(h2)
    _store_row_tiles(h2_ref, h_hi.astype(F32))
    lg_ref[...] = (_dot_nt(rh_ref[...], h_hi) + _dot_nt(rl_ref[...], h_hi) + _dot_nt(rh_ref[...], h_lo)
                   + rb_ref[...])


def _outproj(conv_l, lru_l, x2, g1, sh2, sc2, ng, wo, r_hi, r_lo, r_b, rows_per_mod, tm):
    t, d = x2.shape
    c = conv_l.shape[1]
    blocks_per_mod = rows_per_mod // tm
    mod_spec = pl.BlockSpec((1, 1, d), lambda i: (i // blocks_per_mod, 0, 0))
    full = lambda a: pl.BlockSpec(a.shape, lambda i: (0, 0))
    return pl.pallas_call(
        _outproj_kernel,
        out_shape=[jax.ShapeDtypeStruct((t, d), F32), jax.ShapeDtypeStruct((t * SUBLANES, LANES), F32),
                   jax.ShapeDtypeStruct((ROUTE_ROWS, t), F32)],
        grid=(t // tm,),
        in_specs=[pl.BlockSpec((tm, c), lambda i: (i, 0)), pl.BlockSpec((tm, c), lambda i: (i, 0)),
                  pl.BlockSpec((tm, d), lambda i: (i, 0)), mod_spec, mod_spec, mod_spec,
                  full(ng), full(wo), full(r_hi), full(r_lo), full(r_b)],
        out_specs=[pl.BlockSpec((tm, d), lambda i: (i, 0)), pl.BlockSpec((tm * SUBLANES, LANES), lambda i: (i, 0)),
                   pl.BlockSpec((ROUTE_ROWS, tm), lambda i: (0, i))],
        compiler_params=_cparams(("arbitrary",)),
        name="out_proj_router",
    )(conv_l, lru_l, x2, g1, sh2, sc2, ng, wo, r_hi, r_lo, r_b)


def _route_chunk(lg_ref, lanes):
    n = lanes.size
    e = EXPERTS_PER_GROUP
    lgrp = lg_ref[0:N_GROUPS, lanes]
    gidx = lax.broadcasted_iota(jnp.int32, (N_GROUPS, n), 0)
    m = jnp.max(lgrp, axis=0, keepdims=True)
    ex = jnp.exp(lgrp - m)
    pg = ex / jnp.sum(ex, axis=0, keepdims=True)
    p_grp = jnp.max(pg, axis=0, keepdims=True)
    grp = jnp.min(jnp.where(pg == p_grp, gidx, N_GROUPS), axis=0, keepdims=True)
    le = jnp.zeros((e, n), F32)
    for g in range(N_GROUPS):
        rows = lg_ref[EXPERT_ROW0 + g * e:EXPERT_ROW0 + (g + 1) * e, lanes]
        le = jnp.where(grp == g, rows, le)
    m = jnp.max(le, axis=0, keepdims=True)
    ex = jnp.exp(le - m)
    pe = ex / jnp.sum(ex, axis=0, keepdims=True)
    eidx = lax.broadcasted_iota(jnp.int32, (e, n), 0)
    p1 = jnp.max(pe, axis=0, keepdims=True)
    i1 = jnp.min(jnp.where(pe == p1, eidx, e), axis=0, keepdims=True)
    pe2 = jnp.where(eidx == i1, -1.0, pe)
    p2 = jnp.max(pe2, axis=0, keepdims=True)
    i2 = jnp.min(jnp.where(pe2 == p2, eidx, e), axis=0, keepdims=True)
    denom = p1 + p2
    base = grp * e
    return (base + i1, base + i2), (p_grp * p1 / denom, p_grp * p2 / denom)


def _route_kernel(lg_ref, gate_ref, dest_ref, cnt_ref, run_ref, *, block_rows):
    p = pl.program_id(0)
    i = pl.program_id(1)
    tl = lg_ref.shape[1]
    chunk = 2 * LANES
    ne = N_EXPERTS

    @pl.when((p == 0) & (i == 0))
    def _():
        run_ref[...] = jnp.zeros_like(run_ref)

    @pl.when((p == 1) & (i == 0))
    def _():
        counts = run_ref[...]
        cnt_ref[...] = counts
        padded = jnp.ceil(counts * (1.0 / block_rows)) * block_rows
        rows = []
        start = jnp.zeros((1, LANES), F32)
        for ex in range(ne):
            rows.append(start)
            start = start + padded[ex:ex + 1, :]
        run_ref[...] = jnp.concatenate(rows, axis=0)

    tri = (lax.broadcasted_iota(jnp.int32, (chunk, chunk), 0)
           <= lax.broadcasted_iota(jnp.int32, (chunk, chunk), 1)).astype(BF16)
    eidx = lax.broadcasted_iota(jnp.int32, (ne, chunk), 0)
    for cix in range(tl // chunk):
        lanes = pl.ds(cix * chunk, chunk)
        eids, gates = _route_chunk(lg_ref, lanes)
        dests = []
        for k in range(TOP_K):
            onehot = eidx == eids[k]
            cum = _dot(onehot.astype(BF16), tri)
            run = run_ref[...][:, 0:1]
            dests.append(jnp.sum(jnp.where(onehot, run + cum - 1.0, 0.0), axis=0, keepdims=True))
            run_ref[...] = run_ref[...] + cum[:, chunk - 1:chunk]
        gate_ref[:, lanes] = jnp.concatenate(gates, axis=0)
        dest_ref[:, lanes] = jnp.concatenate(dests, axis=0).astype(jnp.int32)


def _route(logits_t, block_rows):
    rows, t = logits_t.shape
    tl = 2048
    blk = pl.BlockSpec((TOP_K, tl), lambda p, i: (0, i * p))
    return pl.pallas_call(
        functools.partial(_route_kernel, block_rows=block_rows),
        out_shape=[jax.ShapeDtypeStruct((TOP_K, t), F32), jax.ShapeDtypeStruct((TOP_K, t), jnp.int32),
                   jax.ShapeDtypeStruct((N_EXPERTS, LANES), F32)],
        grid=(2, t // tl),
        in_specs=[pl.BlockSpec((rows, tl), lambda p, i: (0, i))],
        out_specs=[blk, blk, pl.BlockSpec((N_EXPERTS, LANES), lambda p, i: (0, 0))],
        scratch_shapes=[pltpu.VMEM((N_EXPERTS, LANES), F32)],
        compiler_params=_cparams(("arbitrary", "arbitrary")),
        name="route",
    )(logits_t)


def _tables_kernel(cnt_ref, dest_ref, src_ref, blk_e_ref, nxt_ref, nact_ref, *, n_tok, bm):
    i = pl.program_id(0)
    cb = dest_ref.shape[0]
    n_rows = src_ref.shape[0]
    n_blocks = blk_e_ref.shape[0]
    shift = bm.bit_length() - 1

    def spare(r):
        return TOP_K * n_tok + ((r >> shift) & 1) * bm + (r & (bm - 1))

    @pl.when(i == 0)
    def _():
        def per_expert(ex, run):
            cnt = cnt_ref[ex]
            padded = ((cnt + (bm - 1)) >> shift) << shift

            end_blk = (run + padded) >> shift

            def fill_blk(b, carry):
                blk_e_ref[b] = ex
                nxt_ref[b] = end_blk
                return carry
            lax.fori_loop(run >> shift, end_blk, fill_blk, 0)

            def fill_pad(r, carry):
                src_ref[r] = spare(r)
                return carry
            lax.fori_loop(run + cnt, run + padded, fill_pad, 0)
            return run + padded
        total = lax.fori_loop(0, N_EXPERTS, per_expert, 0)
        nact_ref[0] = total >> shift

        def fill_blk(b, carry):
            blk_e_ref[b] = N_EXPERTS - 1
            nxt_ref[b] = n_blocks
            return carry
        lax.fori_loop(total >> shift, n_blocks, fill_blk, 0)

        def fill_pad(r, carry):
            src_ref[r] = spare(r)
            return carry
        lax.fori_loop(total, n_rows, fill_pad, 0)

    base = i * cb

    def invert(j, carry):
        src_ref[dest_ref[j]] = base + j
        return carry
    lax.fori_loop(0, cb, invert, 0, unroll=16)


def _tables(counts, dest_flat, n_tok, n_blocks, bm):
    cb = 4096
    smem = functools.partial(pl.BlockSpec, memory_space=pltpu.SMEM)
    return pl.pallas_call(
        functools.partial(_tables_kernel, n_tok=n_tok, bm=bm),
        out_shape=[jax.ShapeDtypeStruct((n_blocks * bm,), jnp.int32),
                   jax.ShapeDtypeStruct((n_blocks,), jnp.int32),
                   jax.ShapeDtypeStruct((n_blocks,), jnp.int32),
                   jax.ShapeDtypeStruct((1,), jnp.int32)],
        grid=(dest_flat.shape[0] // cb,),
        in_specs=[smem(), smem((cb,), lambda i: (i,))],
        out_specs=[smem(), smem(), smem(), smem()],
        compiler_params=_cparams(("arbitrary",)),
        name="dispatch_tables",
    )(counts, dest_flat)


def _moe_kernel(blk_e_ref, nxt_ref, nact_ref, src_ref,
                w1_hbm, w3_hbm, w2_hbm, h_hbm, y_hbm,
                wf1, wf3, wf2, w1b, w3b, w2b, xbuf, ybuf, gsem, ssem, wsem, *, n_tok):
    i = pl.program_id(0)
    nact = nact_ref[0]
    n_blocks = blk_e_ref.shape[0]
    bm = xbuf.shape[1] // SUBLANES
    de = w1b.shape[1]
    slot = lax.rem(i, 2)
    other = 1 - slot
    chunk = 2 * LANES
    n_chunks = de // chunk
    rows_per_chunk = 2 * bm // n_chunks

    def tile(row):
        return pl.ds(pl.multiple_of(row * SUBLANES, SUBLANES), SUBLANES)

    def token_of(n):
        if n_tok & (n_tok - 1) == 0:
            return n & (n_tok - 1)
        return lax.rem(n, n_tok)

    def gather_start(blk, sl, rows):
        for r in rows:
            tok = token_of(src_ref[blk * bm + r])
            pltpu.make_async_copy(h_hbm.at[tile(tok), :], xbuf.at[sl, tile(r), :], gsem.at[sl]).start()

    def gather_wait(sl):
        pltpu.make_async_copy(h_hbm.at[pl.ds(0, bm * SUBLANES), :], xbuf.at[sl], gsem.at[sl]).wait()

    def scatter_wait(sl):
        pltpu.make_async_copy(ybuf.at[sl], y_hbm.at[pl.ds(0, bm * SUBLANES), :], ssem.at[sl]).wait()

    def spare_row(set_id, r):
        return TOP_K * n_tok + set_id * bm + r

    @pl.when(i < nact)
    def _():
        @pl.when(i == 0)
        def _():
            gather_start(0, 0, range(bm))
            ybuf[...] = jnp.zeros_like(ybuf)
            for sl in range(2):
                spare = pltpu.make_async_copy(
                    ybuf.at[sl], y_hbm.at[pl.ds(spare_row(sl, 0) * SUBLANES, bm * SUBLANES), :], ssem.at[sl])
                spare.start()
                spare.wait()

        def weight_copies(e):
            return [pltpu.make_async_copy(w_hbm.at[e], stage, wsem.at[k])
                    for k, (w_hbm, stage) in enumerate(((w1_hbm, wf1), (w3_hbm, wf3), (w2_hbm, wf2)))]

        this_e = blk_e_ref[i]

        @pl.when(i == 0)
        def _():
            for cp in weight_copies(this_e):
                cp.start()

        @pl.when((i == 0) | (this_e != blk_e_ref[jnp.maximum(i - 1, 0)]))
        def _():
            for cp in weight_copies(this_e):
                cp.wait()
            for stage, wb in ((wf1, w1b), (wf3, w3b), (wf2, w2b)):
                def cast_rows(rb, carry, stage=stage, wb=wb):
                    rows = pl.ds(pl.multiple_of(rb * LANES, LANES), LANES)
                    wb[rows, :] = stage[rows, :].astype(BF16)
                    return carry
                lax.fori_loop(0, stage.shape[0] // LANES, cast_rows, 0)
            nxt_blk = nxt_ref[i]

            @pl.when(nxt_blk < nact)
            def _():
                for cp in weight_copies(blk_e_ref[nxt_blk]):
                    cp.start(priority=1)

        gather_wait(slot)
        xb = _load_row_tiles(xbuf.at[slot], bm).astype(BF16)
        nxt = jnp.minimum(i + 1, n_blocks - 1)
        prv = jnp.maximum(i - 1, 0)
        has_prev = i > 0
        y = None
        for c in range(n_chunks):
            cols = slice(c * chunk, (c + 1) * chunk)
            a = _dot(xb, w1b[:, cols])
            g = _dot(xb, w3b[:, cols])
            act = (_silu(a) * g).astype(BF16)
            part = _dot(act, w2b[cols, :])
            y = part if y is None else y + part
            if c < n_chunks // 2:
                gather_start(nxt, other, range(c * rows_per_chunk, (c + 1) * rows_per_chunk))
            else:
                first = (c - n_chunks // 2) * rows_per_chunk
                for r in range(first, first + rows_per_chunk):
                    n = jnp.where(has_prev, src_ref[prv * bm + r], spare_row(1, r))
                    pltpu.make_async_copy(ybuf.at[other, tile(r), :], y_hbm.at[tile(n), :],
                                          ssem.at[other]).start(priority=r % 2)

        @pl.when(i >= 1)
        def _():
            scatter_wait(slot)

        _store_row_tiles(ybuf.at[slot], y)

        @pl.when(i == nact - 1)
        def _():
            for r in range(bm):
                n = src_ref[i * bm + r]
                pltpu.make_async_copy(ybuf.at[slot, tile(r), :], y_hbm.at[tile(n), :], ssem.at[slot]).start()
            scatter_wait(other)
            scatter_wait(slot)
            gather_wait(other)


def _moe(blk_e, nxt_blk, nact, src, w1, w3, w2, h2_tiles, n_blocks):
    t = h2_tiles.shape[0] // SUBLANES
    d = w1.shape[1]
    de = w1.shape[2]
    bm = MOE_BLOCK
    hbm = pl.BlockSpec(memory_space=pl.ANY)
    grid_spec = pltpu.PrefetchScalarGridSpec(
        num_scalar_prefetch=4,
        grid=(n_blocks,),
        in_specs=[hbm, hbm, hbm, hbm],
        out_specs=hbm,
        scratch_shapes=[pltpu.VMEM((d, de), F32), pltpu.VMEM((d, de), F32), pltpu.VMEM((de, d), F32),
                        pltpu.VMEM((d, de), BF16), pltpu.VMEM((d, de), BF16), pltpu.VMEM((de, d), BF16),
                        pltpu.VMEM((2, bm * SUBLANES, LANES), F32), pltpu.VMEM((2, bm * SUBLANES, LANES), F32),
                        pltpu.SemaphoreType.DMA((2,)), pltpu.SemaphoreType.DMA((2,)),
                        pltpu.SemaphoreType.DMA((3,))],
    )
    return pl.pallas_call(
        functools.partial(_moe_kernel, n_tok=t),
        out_shape=jax.ShapeDtypeStruct(((TOP_K * t + 2 * bm) * SUBLANES, LANES), F32),
        grid_spec=grid_spec,
        compiler_params=_cparams(("arbitrary",)),
        name="moe_experts",
    )(blk_e, nxt_blk, nact, src, w1, w3, w2, h2_tiles)


def _final_kernel(x1_ref, y0_ref, y1_ref, gt_ref, g2_ref, fg_ref, o_ref):
    gt = gt_ref[...]
    tm = x1_ref.shape[0]
    y2 = gt[:, 0:1] * _load_row_tiles(y0_ref, tm) + gt[:, 1:2] * _load_row_tiles(y1_ref, tm)
    x = x1_ref[...] + g2_ref[0] * y2
    o_ref[...] = x * lax.rsqrt(jnp.mean(x * x, axis=-1, keepdims=True) + EPS) * fg_ref[...]


def _final(x1, y, gates_tok, g2, fg, rows_per_mod, tm):
    t, d = x1.shape
    nb = t // tm
    blocks_per_mod = rows_per_mod // tm
    return pl.pallas_call(
        _final_kernel,
        out_shape=jax.ShapeDtypeStruct((t, d), F32),
        grid=(nb,),
        in_specs=[pl.BlockSpec((tm, d), lambda i: (i, 0)),
                  pl.BlockSpec((tm * SUBLANES, LANES), lambda i: (i, 0)),
                  pl.BlockSpec((tm * SUBLANES, LANES), lambda i: (i + nb, 0)),
                  pl.BlockSpec((tm, TOP_K), lambda i: (i, 0)),
                  pl.BlockSpec((1, 1, d), lambda i: (i // blocks_per_mod, 0, 0)),
                  pl.BlockSpec((1, d), lambda i: (0, 0))],
        out_specs=pl.BlockSpec((tm, d), lambda i: (i, 0)),
        compiler_params=_cparams(("arbitrary",)),
        name="combine_final_norm",
    )(x1, y, y, gates_tok, g2, fg)


def kernel(x, c, ctx, c_ctx, w_ada, b_ada, norm1_g, norm2_g, w_in, conv_dw, conv_b, conv_ln_g, conv_ln_b,
           lru_conv_w, lru_conv_b, lru_wa, lru_ba, lru_wx, lru_bx, lru_lam, w_out,
           router_wg, router_bg, router_we, router_be, w1, w3, w2, final_g):
    assert w_ada.shape[0] == 1, "single-layer block"
    assert x.shape[2] == SUBLANES * LANES, "row-tile layout needs one (8, 128) tile per token row"
    b, s, d = x.shape
    n_ctx = ctx.shape[1]
    t = b * s
    cc = conv_dw.shape[2]
    lw = lru_conv_w.shape[2]

    c_rows = jnp.zeros((SUBLANES, d), F32).at[:b].set(c).at[b].set(c_ctx)
    mod = _ada(c_rows, w_ada[0], b_ada)
    mod_l = mod[:b].reshape(b, 6, 1, d)
    sh1, sc1, g1, sh2, sc2, g2 = (mod_l[:, k] for k in range(6))
    mod_c = mod[b].reshape(6, 1, 1, d)
    csh1, csc1 = mod_c[0], mod_c[1]

    w_in_b = w_in[0].astype(BF16)
    w_out_b = w_out[0].astype(BF16)
    heads_per_blk = LRU_LANES // lru_wa.shape[3]
    n_cblk = lw // LRU_LANES

    def blockdiag(wh):
        hd = wh.shape[1]
        wh = wh.reshape(n_cblk, heads_per_blk, hd, hd)
        eye = jnp.eye(heads_per_blk, dtype=wh.dtype)
        return jnp.einsum("chij,hg->chigj", wh, eye).reshape(n_cblk, LRU_LANES, LRU_LANES)

    wg = jnp.concatenate([blockdiag(lru_wa[0, 0]), blockdiag(lru_wx[0, 0]),
                          blockdiag(lru_wa[0, 1]), blockdiag(lru_wx[0, 1])], axis=2).astype(BF16)
    gb = jnp.stack([lru_ba[0, 0], lru_bx[0, 0], lru_ba[0, 1], lru_bx[0, 1]])
    lam = lru_lam[0]

    zc = _inproj_ctx(ctx.reshape(b * n_ctx, d), csh1, csc1, norm1_g, w_in_b[:, 2 * cc:2 * cc + lw], n_ctx)
    h0 = _lru(zc.reshape(b, n_ctx, lw), None, jnp.zeros((2, b, 1, lw), F32),
              lru_conv_w[0], lru_conv_b, wg, gb, lam, True, "rglru_ctx")

    x2 = x.reshape(t, d)
    conv_w_rep = jnp.repeat(conv_dw[0], SUBLANES, axis=0).reshape(CONV_TAPS * SUBLANES, cc // LANES, LANES)
    conv_w_rep = conv_w_rep.transpose(1, 0, 2)
    conv_l, zl, gg = _inproj(x2, sh1, sc1, norm1_g, w_in_b, conv_w_rep, conv_b, conv_ln_g, conv_ln_b, s)
    lru_l = _lru(zl.reshape(b, s, lw), gg.reshape(b, s, lw), h0,
                 lru_conv_w[0], lru_conv_b, wg, gb, lam, False, "rglru")

    wr = jnp.zeros((ROUTE_ROWS, d), F32)
    wr = wr.at[:N_GROUPS].set(router_wg[0].T)
    wr = wr.at[EXPERT_ROW0:EXPERT_ROW0 + N_EXPERTS].set(router_we[0].reshape(d, N_EXPERTS).T)
    rb = jnp.zeros((ROUTE_ROWS, 1), F32)
    rb = rb.at[:N_GROUPS, 0].set(router_bg[0])
    rb = rb.at[EXPERT_ROW0:EXPERT_ROW0 + N_EXPERTS, 0].set(router_be[0].reshape(-1))
    r_hi, r_lo = _split_bf16(wr)
    x1, h2, logits_t = _outproj(conv_l, lru_l.reshape(t, lw), x2, g1, sh2, sc2, norm2_g, w_out_b,
                                r_hi, r_lo, rb, s, TOKEN_BLOCK)

    gates, dest, counts = _route(logits_t, MOE_BLOCK)
    n_blocks = (TOP_K * t) // MOE_BLOCK + N_EXPERTS
    src, blk_e, nxt_blk, nact = _tables(counts[:, 0].astype(jnp.int32), dest.reshape(-1), t, n_blocks, MOE_BLOCK)
    y = _moe(blk_e, nxt_blk, nact, src, w1[0], w3[0], w2[0], h2, n_blocks)
    out = _final(x1, y, gates.T, g2, final_g.reshape(1, d), s, TOKEN_BLOCK)
    return out.reshape(b, s, d)
```

```python
import functools

import jax
import jax.numpy as jnp
from jax import lax
from jax.experimental import pallas as pl
from jax.experimental.pallas import tpu as pltpu

F32 = jnp.float32
BF16 = jnp.bfloat16

EPS = 1e-6
CONV_TAPS = 31
LRU_TAPS = 4
LRU_C = 8.0
GRID_W = 64
N_GROUPS = 4
EXPERTS_PER_GROUP = 8
N_EXPERTS = N_GROUPS * EXPERTS_PER_GROUP
TOP_K = 2

SUBLANES = 8
LANES = 128
TOKEN_BLOCK = 512
LRU_LANES = 128
ROUTE_ROWS = 128
EXPERT_ROW0 = 8
MOE_BLOCK = 256
VMEM_LIMIT = 48 * 1024 * 1024


def _cparams(sem, vmem=VMEM_LIMIT):
    return pltpu.CompilerParams(dimension_semantics=sem, vmem_limit_bytes=vmem)


def _split_bf16(a):
    hi = a.astype(BF16)
    lo = (a - hi.astype(F32)).astype(BF16)
    return hi, lo


def _dot(a, b):
    return jnp.dot(a, b, preferred_element_type=F32)


def _dot_nt(a, b):
    return lax.dot_general(a, b, (((1,), (1,)), ((), ())), preferred_element_type=F32)


def _sigmoid(x):
    return 0.5 * jnp.tanh(0.5 * x) + 0.5


def _silu(x):
    return x * _sigmoid(x)


def _gelu_tanh(x):
    c = 0.7978845608028654
    return 0.5 * x * (1.0 + jnp.tanh(c * (x + 0.044715 * (x * x * x))))


def _store_row_tiles(ref, x):
    rows = x.shape[0]
    for s in range(SUBLANES):
        ref[pl.ds(s, rows, stride=SUBLANES), :] = x[:, s * LANES:(s + 1) * LANES]


def _load_row_tiles(ref, rows):
    return jnp.concatenate([ref[pl.ds(s, rows, stride=SUBLANES), :] for s in range(SUBLANES)], axis=1)


def _rms_mod(x, g, shift, scale):
    y = x * lax.rsqrt(jnp.mean(x * x, axis=-1, keepdims=True) + EPS)
    return (y * g) * (1.0 + scale) + shift


def _ada_kernel(c_ref, w_ref, b_ref, o_ref):
    a = _silu(c_ref[...])
    a_hi, a_lo = _split_bf16(a)
    w_hi, w_lo = _split_bf16(w_ref[...])
    o_ref[...] = _dot(a_hi, w_hi) + _dot(a_lo, w_hi) + _dot(a_hi, w_lo) + b_ref[...]


def _ada(c_rows, w, b):
    m, d = c_rows.shape
    n = w.shape[1]
    bn = 768
    return pl.pallas_call(
        _ada_kernel,
        out_shape=jax.ShapeDtypeStruct((m, n), F32),
        grid=(n // bn,),
        in_specs=[pl.BlockSpec((m, d), lambda j: (0, 0)),
                  pl.BlockSpec((d, bn), lambda j: (0, j)),
                  pl.BlockSpec((1, bn), lambda j: (0, j))],
        out_specs=pl.BlockSpec((m, bn), lambda j: (0, j)),
        compiler_params=_cparams(("arbitrary",)),
        name="ada_mod",
    )(c_rows, w, b)


def _inproj_ctx_kernel(x_ref, sh_ref, sc_ref, g_ref, w_ref, o_ref):
    h = _rms_mod(x_ref[...], g_ref[...], sh_ref[0], sc_ref[0])
    o_ref[...] = _dot(h.astype(BF16), w_ref[...])


def _conformer_conv(u, cw_ref, cb_ref, lg_ref, lb_ref, o_ref, stg_ref, xt_ref, ot_ref):
    c = u.shape[1]
    slabs = range(c // LANES)
    half = CONV_TAPS // 2
    pitch = GRID_W + SUBLANES
    group = SUBLANES
    for l in slabs:
        for q in range(SUBLANES):
            stg_ref[l, q * pitch:q * pitch + GRID_W, :] = u[q * GRID_W:(q + 1) * GRID_W, l * LANES:(l + 1) * LANES]

    def conv_slab(l, carry):
        for t in range(GRID_W):
            xt_ref[l, t * SUBLANES:(t + 1) * SUBLANES, :] = stg_ref[l, pl.ds(t, SUBLANES, stride=pitch), :]
        for t0 in range(0, GRID_W, group):
            accs = [None] * group
            for k in range(CONV_TAPS):
                srcs = [t0 + j + k - half for j in range(group)]
                if not any(0 <= sidx < GRID_W for sidx in srcs):
                    continue
                wk = cw_ref[l, k * SUBLANES:(k + 1) * SUBLANES, :]
                for j, sidx in enumerate(srcs):
                    if 0 <= sidx < GRID_W:
                        term = wk * xt_ref[l, sidx * SUBLANES:(sidx + 1) * SUBLANES, :]
                        accs[j] = term if accs[j] is None else accs[j] + term
            for j in range(group):
                ot_ref[l, (t0 + j) * SUBLANES:(t0 + j + 1) * SUBLANES, :] = accs[j]
        return carry
    lax.fori_loop(0, c // LANES, conv_slab, 0)
    rows_per_pass = group * SUBLANES
    for r0 in range(0, GRID_W * SUBLANES, rows_per_pass):
        rows = slice(r0, r0 + rows_per_pass)
        acc = jnp.concatenate([ot_ref[l, rows, :] for l in slabs], axis=1) + cb_ref[...]
        mu = jnp.mean(acc, axis=-1, keepdims=True)
        cen = acc - mu
        var = jnp.mean(cen * cen, axis=-1, keepdims=True)
        y = _silu(cen * lax.rsqrt(var + EPS) * lg_ref[...] + lb_ref[...])
        for l in slabs:
            ot_ref[l, rows, :] = y[:, l * LANES:(l + 1) * LANES]
    for q in range(SUBLANES):
        o_ref[q * GRID_W:(q + 1) * GRID_W, :] = jnp.concatenate(
            [ot_ref[l, pl.ds(q, GRID_W, stride=SUBLANES), :] for l in slabs], axis=1).astype(o_ref.dtype)


def _inproj_kernel(x_ref, sh_ref, sc_ref, g_ref, w_ref, cw_ref, cb_ref, lg_ref, lb_ref,
                   cv_ref, zl_ref, gg_ref, stg_ref, xt_ref, ot_ref):
    h = _rms_mod(x_ref[...], g_ref[...], sh_ref[0], sc_ref[0])
    z = _dot(h.astype(BF16), w_ref[...])
    c = cv_ref.shape[1]
    zl_ref[...] = z[:, 2 * c:3 * c]
    gg_ref[...] = _gelu_tanh(z[:, 3 * c:])
    u = z[:, :c] * _sigmoid(z[:, c:2 * c])
    _conformer_conv(u, cw_ref, cb_ref, lg_ref, lb_ref, cv_ref, stg_ref, xt_ref, ot_ref)


def _inproj_ctx(x2, shift, scale, g, w, tm):
    t, d = x2.shape
    n = w.shape[1]
    mod_spec = pl.BlockSpec((1, 1, d), lambda i: (0, 0, 0))
    return pl.pallas_call(
        _inproj_ctx_kernel,
        out_shape=jax.ShapeDtypeStruct((t, n), F32),
        grid=(t // tm,),
        in_specs=[pl.BlockSpec((tm, d), lambda i: (i, 0)), mod_spec, mod_spec,
                  pl.BlockSpec((1, d), lambda i: (0, 0)),
                  pl.BlockSpec((d, n), lambda i: (0, 0))],
        out_specs=pl.BlockSpec((tm, n), lambda i: (i, 0)),
        compiler_params=_cparams(("arbitrary",)),
        name="in_proj_ctx",
    )(x2, shift, scale, g, w)


def _inproj(x2, shift, scale, g, w, cw, cb, lg, lb, rows_per_mod):
    t, d = x2.shape
    n = w.shape[1]
    c = n // 4
    tm = SUBLANES * GRID_W
    blocks_per_mod = rows_per_mod // tm
    mod_spec = pl.BlockSpec((1, 1, d), lambda i: (i // blocks_per_mod, 0, 0))
    const = lambda a: pl.BlockSpec(a.shape, lambda i: (0,) * a.ndim)
    rows = pl.BlockSpec((tm, c), lambda i: (i, 0))
    return pl.pallas_call(
        _inproj_kernel,
        out_shape=[jax.ShapeDtypeStruct((t, c), BF16), jax.ShapeDtypeStruct((t, c), F32),
                   jax.ShapeDtypeStruct((t, c), F32)],
        grid=(t // tm,),
        in_specs=[pl.BlockSpec((tm, d), lambda i: (i, 0)), mod_spec, mod_spec, const(g), const(w),
                  const(cw), const(cb), const(lg), const(lb)],
        out_specs=[rows, rows, rows],
        scratch_shapes=[pltpu.VMEM((c // LANES, SUBLANES * (GRID_W + SUBLANES), LANES), F32),
                        pltpu.VMEM((c // LANES, tm, LANES), F32), pltpu.VMEM((c // LANES, tm, LANES), F32)],
        compiler_params=_cparams(("arbitrary",)),
        name="in_proj",
    )(x2, shift, scale, g, w, cw, cb, lg, lb)


def _lru_kernel(zl_ref, h0_ref, cw_ref, cb_ref, wg_ref, gb_ref, lam_ref, *rest, seq, final_only):
    if final_only:
        o_ref, stage_ref, zt_ref, af_ref, bf_ref, ab_ref, bb_ref, pf_ref, hf_ref, pb_ref, hb_ref = rest
        gg_ref = None
    else:
        gg_ref, o_ref, stage_ref, zt_ref, af_ref, bf_ref, ab_ref, bb_ref, pf_ref, hf_ref, pb_ref, hb_ref = rest
    cl = seq // SUBLANES
    pitch = cl + SUBLANES
    c = zl_ref.shape[2]
    a_refs = (af_ref, ab_ref)
    b_refs = (bf_ref, bb_ref)
    halo = 2

    def tile_rows(t):
        return pl.ds(pl.multiple_of(t * SUBLANES, SUBLANES), SUBLANES)

    for j in range(SUBLANES):
        stage_ref[j * pitch:j * pitch + cl, :] = zl_ref[0, j * cl:(j + 1) * cl, :]

    def to_chunk_layout(t, carry):
        zt_ref[tile_rows(t + halo), :] = stage_ref[pl.ds(t, SUBLANES, stride=pitch), :]
        return carry
    lax.fori_loop(0, cl, to_chunk_layout, 0, unroll=8)

    sub = lax.broadcasted_iota(jnp.int32, (SUBLANES, c), 0)
    for t_src, t_dst in ((cl - 2, -2), (cl - 1, -1)):
        v = pltpu.roll(zt_ref[(t_src + halo) * SUBLANES:(t_src + halo + 1) * SUBLANES, :], 1, 0)
        zt_ref[(t_dst + halo) * SUBLANES:(t_dst + halo + 1) * SUBLANES, :] = jnp.where(sub == 0, 0.0, v)
    v = pltpu.roll(zt_ref[halo * SUBLANES:(halo + 1) * SUBLANES, :], SUBLANES - 1, 0)
    zt_ref[(cl + halo) * SUBLANES:(cl + halo + 1) * SUBLANES, :] = jnp.where(sub == SUBLANES - 1, 0.0, v)

    lam = lam_ref[...]
    nlam = -lam
    softplus = jnp.maximum(nlam, 0.0) + jnp.log1p(jnp.exp(-jnp.abs(nlam)))
    decay = -LRU_C * softplus
    wg = wg_ref[0]

    piece = cl
    for p0 in range(0, cl * SUBLANES, piece):
        ul = cb_ref[...] + jnp.zeros((piece, c), F32)
        for k in range(LRU_TAPS):
            off = p0 + k * SUBLANES
            ul = ul + cw_ref[k:k + 1, :] * zt_ref[off:off + piece, :]
        g = _dot(ul.astype(BF16), wg)
        for d in range(2):
            r = _sigmoid(g[:, (2 * d) * c:(2 * d + 1) * c] + gb_ref[2 * d:2 * d + 1, :])
            i = _sigmoid(g[:, (2 * d + 1) * c:(2 * d + 2) * c] + gb_ref[2 * d + 1:2 * d + 2, :])
            log_a = decay[d:d + 1, :] * r
            a = jnp.exp(log_a)
            m = jnp.maximum(-jnp.tanh(log_a) * (a * a + 1.0), 1e-12)
            mult = m * lax.rsqrt(m)
            a_refs[d][p0:p0 + piece, :] = a
            b_refs[d][p0:p0 + piece, :] = mult * i * ul

    def step(n, carry):
        hf, pf, hb, pb = carry
        sf = tile_rows(n)
        sb = tile_rows(cl - 1 - n)
        a_f = af_ref[sf, :]
        a_b = ab_ref[sb, :]
        hf = a_f * hf + bf_ref[sf, :]
        hb = a_b * hb + bb_ref[sb, :]
        pf = a_f * pf
        pb = a_b * pb
        hf_ref[sf, :] = hf
        pf_ref[sf, :] = pf
        hb_ref[sb, :] = hb
        pb_ref[sb, :] = pb
        return hf, pf, hb, pb

    zero = jnp.zeros((SUBLANES, c), F32)
    one = jnp.ones((SUBLANES, c), F32)
    lax.fori_loop(0, cl, step, (zero, one, zero, one), unroll=8)

    cf = [None] * SUBLANES
    cbk = [None] * SUBLANES
    s = h0_ref[0, 0]
    for j in range(SUBLANES):
        cf[j] = s
        last = (cl - 1) * SUBLANES + j
        s = hf_ref[last:last + 1, :] + pf_ref[last:last + 1, :] * s
    final_f = s
    s = h0_ref[1, 0]
    for j in reversed(range(SUBLANES)):
        cbk[j] = s
        s = hb_ref[j:j + 1, :] + pb_ref[j:j + 1, :] * s
    final_b = s

    if final_only:
        o_ref[0, 0] = final_f
        o_ref[1, 0] = final_b
        return
    carry_f = jnp.concatenate(cf, axis=0)
    carry_b = jnp.concatenate(cbk, axis=0)

    def to_natural(t, carry):
        rows = tile_rows(t)
        h = (hf_ref[rows, :] + pf_ref[rows, :] * carry_f) + (hb_ref[rows, :] + pb_ref[rows, :] * carry_b)
        stage_ref[pl.ds(t, SUBLANES, stride=pitch), :] = h
        return carry
    lax.fori_loop(0, cl, to_natural, 0, unroll=8)
    for j in range(SUBLANES):
        h = stage_ref[j * pitch:j * pitch + cl, :]
        o_ref[0, j * cl:(j + 1) * cl, :] = (h * gg_ref[0, j * cl:(j + 1) * cl, :]).astype(o_ref.dtype)


def _lru(zl, gg, h0, cw, cb, wg, gb, lam, final_only, name):
    b, seq, c = zl.shape
    cbk = LRU_LANES
    cl = seq // SUBLANES
    pitch = cl + SUBLANES
    seq_spec = pl.BlockSpec((1, seq, cbk), lambda bi, ci: (bi, 0, ci))
    st_spec = pl.BlockSpec((2, 1, 1, cbk), lambda bi, ci: (0, bi, 0, ci))
    chan = lambda rows: pl.BlockSpec((rows, cbk), lambda bi, ci: (0, ci))
    in_specs = [seq_spec, st_spec, chan(LRU_TAPS), chan(1),
                pl.BlockSpec((1, cbk, 4 * cbk), lambda bi, ci: (ci, 0, 0)), chan(4), chan(2)]
    args = [zl, h0, cw, cb, wg, gb, lam]
    if final_only:
        out_shape = jax.ShapeDtypeStruct((2, b, 1, c), F32)
        out_spec = st_spec
    else:
        in_specs.append(seq_spec)
        args.append(gg)
        out_shape = jax.ShapeDtypeStruct((b, seq, c), BF16)
        out_spec = seq_spec
    coef = pltpu.VMEM((seq, cbk), F32)
    return pl.pallas_call(
        functools.partial(_lru_kernel, seq=seq, final_only=final_only),
        out_shape=out_shape,
        grid=(b, c // cbk),
        in_specs=in_specs,
        out_specs=out_spec,
        scratch_shapes=[pltpu.VMEM((SUBLANES * pitch, cbk), F32),
                        pltpu.VMEM((seq + 2 * SUBLANES * SUBLANES, cbk), F32)] + [coef] * 8,
        compiler_params=_cparams(("arbitrary", "arbitrary")),
        name=name,
    )(*args)


def _outproj_kernel(cv_ref, lr_ref, x_ref, g1_ref, sh_ref, sc_ref, ng_ref, wo_ref, rh_ref, rl_ref, rb_ref,
                    x1_ref, h2_ref, lg_ref):
    c = cv_ref.shape[1]
    y = _dot(cv_ref[...], wo_ref[0:c, :]) + _dot(lr_ref[...], wo_ref[c:2 * c, :])
    x1 = x_ref[...] + g1_ref[0] * y
    x1_ref[...] = x1
    h2 = _rms_mod(x1, ng_ref[...], sh_ref[0], sc_ref[0])
    h_hi, h_lo = _split_bf16(h2)
    _store_row_tiles(h2_ref, h_hi.astype(F32))
    lg_ref[...] = (_dot_nt(rh_ref[...], h_hi) + _dot_nt(rl_ref[...], h_hi) + _dot_nt(rh_ref[...], h_lo)
                   + rb_ref[...])


def _outproj(conv_l, lru_l, x2, g1, sh2, sc2, ng, wo, r_hi, r_lo, r_b, rows_per_mod, tm):
    t, d = x2.shape
    c = conv_l.shape[1]
    blocks_per_mod = rows_per_mod // tm
    mod_spec = pl.BlockSpec((1, 1, d), lambda i: (i // blocks_per_mod, 0, 0))
    full = lambda a: pl.BlockSpec(a.shape, lambda i: (0, 0))
    return pl.pallas_call(
        _outproj_kernel,
        out_shape=[jax.ShapeDtypeStruct((t, d), F32), jax.ShapeDtypeStruct((t * SUBLANES, LANES), F32),
                   jax.ShapeDtypeStruct((ROUTE_ROWS, t), F32)],
        grid=(t // tm,),
        in_specs=[pl.BlockSpec((tm, c), lambda i: (i, 0)), pl.BlockSpec((tm, c), lambda i: (i, 0)),
                  pl.BlockSpec((tm, d), lambda i: (i, 0)), mod_spec, mod_spec, mod_spec,
                  full(ng), full(wo), full(r_hi), full(r_lo), full(r_b)],
        out_specs=[pl.BlockSpec((tm, d), lambda i: (i, 0)), pl.BlockSpec((tm * SUBLANES, LANES), lambda i: (i, 0)),
                   pl.BlockSpec((ROUTE_ROWS, tm), lambda i: (0, i))],
        compiler_params=_cparams(("arbitrary",)),
        name="out_proj_router",
    )(conv_l, lru_l, x2, g1, sh2, sc2, ng, wo, r_hi, r_lo, r_b)


def _route_chunk(lg_ref, lanes):
    n = lanes.size
    e = EXPERTS_PER_GROUP
    lgrp = lg_ref[0:N_GROUPS, lanes]
    gidx = lax.broadcasted_iota(jnp.int32, (N_GROUPS, n), 0)
    m = jnp.max(lgrp, axis=0, keepdims=True)
    ex = jnp.exp(lgrp - m)
    pg = ex / jnp.sum(ex, axis=0, keepdims=True)
    p_grp = jnp.max(pg, axis=0, keepdims=True)
    grp = jnp.min(jnp.where(pg == p_grp, gidx, N_GROUPS), axis=0, keepdims=True)
    le = jnp.zeros((e, n), F32)
    for g in range(N_GROUPS):
        rows = lg_ref[EXPERT_ROW0 + g * e:EXPERT_ROW0 + (g + 1) * e, lanes]
        le = jnp.where(grp == g, rows, le)
    m = jnp.max(le, axis=0, keepdims=True)
    ex = jnp.exp(le - m)
    pe = ex / jnp.sum(ex, axis=0, keepdims=True)
    eidx = lax.broadcasted_iota(jnp.int32, (e, n), 0)
    p1 = jnp.max(pe, axis=0, keepdims=True)
    i1 = jnp.min(jnp.where(pe == p1, eidx, e), axis=0, keepdims=True)
    pe2 = jnp.where(eidx == i1, -1.0, pe)
    p2 = jnp.max(pe2, axis=0, keepdims=True)
    i2 = jnp.min(jnp.where(pe2 == p2, eidx, e), axis=0, keepdims=True)
    denom = p1 + p2
    base = grp * e
    return (base + i1, base + i2), (p_grp * p1 / denom, p_grp * p2 / denom)


def _route_kernel(lg_ref, gate_ref, dest_ref, cnt_ref, run_ref, *, block_rows):
    p = pl.program_id(0)
    i = pl.program_id(1)
    tl = lg_ref.shape[1]
    chunk = 2 * LANES
    ne = N_EXPERTS

    @pl.when((p == 0) & (i == 0))
    def _():
        run_ref[...] = jnp.zeros_like(run_ref)

    @pl.when((p == 1) & (i == 0))
    def _():
        counts = run_ref[...]
        cnt_ref[...] = counts
        padded = jnp.ceil(counts * (1.0 / block_rows)) * block_rows
        rows = []
        start = jnp.zeros((1, LANES), F32)
        for ex in range(ne):
            rows.append(start)
            start = start + padded[ex:ex + 1, :]
        run_ref[...] = jnp.concatenate(rows, axis=0)

    tri = (lax.broadcasted_iota(jnp.int32, (chunk, chunk), 0)
           <= lax.broadcasted_iota(jnp.int32, (chunk, chunk), 1)).astype(BF16)
    eidx = lax.broadcasted_iota(jnp.int32, (ne, chunk), 0)
    for cix in range(tl // chunk):
        lanes = pl.ds(cix * chunk, chunk)
        eids, gates = _route_chunk(lg_ref, lanes)
        dests = []
        for k in range(TOP_K):
            onehot = eidx == eids[k]
            cum = _dot(onehot.astype(BF16), tri)
            run = run_ref[...][:, 0:1]
            dests.append(jnp.sum(jnp.where(onehot, run + cum - 1.0, 0.0), axis=0, keepdims=True))
            run_ref[...] = run_ref[...] + cum[:, chunk - 1:chunk]
        gate_ref[:, lanes] = jnp.concatenate(gates, axis=0)
        dest_ref[:, lanes] = jnp.concatenate(dests, axis=0).astype(jnp.int32)


def _route(logits_t, block_rows):
    rows, t = logits_t.shape
    tl = 2048
    blk = pl.BlockSpec((TOP_K, tl), lambda p, i: (0, i * p))
    return pl.pallas_call(
        functools.partial(_route_kernel, block_rows=block_rows),
        out_shape=[jax.ShapeDtypeStruct((TOP_K, t), F32), jax.ShapeDtypeStruct((TOP_K, t), jnp.int32),
                   jax.ShapeDtypeStruct((N_EXPERTS, LANES), F32)],
        grid=(2, t // tl),
        in_specs=[pl.BlockSpec((rows, tl), lambda p, i: (0, i))],
        out_specs=[blk, blk, pl.BlockSpec((N_EXPERTS, LANES), lambda p, i: (0, 0))],
        scratch_shapes=[pltpu.VMEM((N_EXPERTS, LANES), F32)],
        compiler_params=_cparams(("arbitrary", "arbitrary")),
        name="route",
    )(logits_t)


def _tables_kernel(cnt_ref, dest_ref, src_ref, blk_e_ref, nxt_ref, nact_ref, *, n_tok, bm):
    i = pl.program_id(0)
    cb = dest_ref.shape[0]
    n_rows = src_ref.shape[0]
    n_blocks = blk_e_ref.shape[0]
    shift = bm.bit_length() - 1

    group = 8

    @pl.when(i == 0)
    def _():
        def per_expert(ex, run):
            cnt = cnt_ref[ex]
            padded = ((cnt + (bm - 1)) >> shift) << shift
            end = run + padded
            end_blk = end >> shift

            def fill_blk(b, carry):
                blk_e_ref[b] = ex
                nxt_ref[b] = end_blk
                return carry
            lax.fori_loop(run >> shift, end_blk, fill_blk, 0)

            last_blk = jnp.maximum(end_blk - 1, 0)
            offset = TOP_K * n_tok + (last_blk & 1) * bm - (last_blk << shift)

            def fill_pad(g, carry):
                for u in range(group):
                    r = jnp.minimum(run + cnt + g * group + u, end - 1)
                    src_ref[r] = r + offset
                return carry
            lax.fori_loop(0, (padded - cnt + (group - 1)) // group, fill_pad, 0)
            return end
        total = lax.fori_loop(0, N_EXPERTS, per_expert, 0)
        nact_ref[0] = total >> shift

        def fill_blk(b, carry):
            blk_e_ref[b] = N_EXPERTS - 1
            nxt_ref[b] = n_blocks
            return carry
        lax.fori_loop(total >> shift, n_blocks, fill_blk, 0)

        def fill_tail(g, carry):
            for u in range(group):
                src_ref[total + g * group + u] = 0
            return carry
        lax.fori_loop(0, (n_rows - total) // group, fill_tail, 0)

    base = i * cb

    def invert(j, carry):
        src_ref[dest_ref[j]] = base + j
        return carry
    lax.fori_loop(0, cb, invert, 0, unroll=16)


def _tables(counts, dest_flat, n_tok, n_blocks, bm):
    cb = 4096
    smem = functools.partial(pl.BlockSpec, memory_space=pltpu.SMEM)
    return pl.pallas_call(
        functools.partial(_tables_kernel, n_tok=n_tok, bm=bm),
        out_shape=[jax.ShapeDtypeStruct((n_blocks * bm,), jnp.int32),
                   jax.ShapeDtypeStruct((n_blocks,), jnp.int32),
                   jax.ShapeDtypeStruct((n_blocks,), jnp.int32),
                   jax.ShapeDtypeStruct((1,), jnp.int32)],
        grid=(dest_flat.shape[0] // cb,),
        in_specs=[smem(), smem((cb,), lambda i: (i,))],
        out_specs=[smem(), smem(), smem(), smem()],
        compiler_params=_cparams(("arbitrary",)),
        name="dispatch_tables",
    )(counts, dest_flat)


def _moe_kernel(blk_e_ref, nxt_ref, nact_ref, src_ref,
                w1_hbm, w3_hbm, w2_hbm, h_hbm, y_hbm,
                wf1, wf3, wf2, w1b, w3b, w2b, xbuf, ybuf, gsem, ssem, wsem, *, n_tok):
    i = pl.program_id(0)
    nact = nact_ref[0]
    n_blocks = blk_e_ref.shape[0]
    bm = xbuf.shape[1] // SUBLANES
    de = w1b.shape[1]
    slot = lax.rem(i, 2)
    other = 1 - slot
    chunk = 2 * LANES
    n_chunks = de // chunk
    rows_per_chunk = 2 * bm // n_chunks

    def tile(row):
        return pl.ds(pl.multiple_of(row * SUBLANES, SUBLANES), SUBLANES)

    def token_of(n):
        if n_tok & (n_tok - 1) == 0:
            return n & (n_tok - 1)
        return lax.rem(n, n_tok)

    def gather_start(blk, sl, rows):
        for r in rows:
            tok = token_of(src_ref[blk * bm + r])
            pltpu.make_async_copy(h_hbm.at[tile(tok), :], xbuf.at[sl, tile(r), :], gsem.at[sl]).start()

    def gather_wait(sl):
        pltpu.make_async_copy(h_hbm.at[pl.ds(0, bm * SUBLANES), :], xbuf.at[sl], gsem.at[sl]).wait()

    def scatter_wait(sl):
        pltpu.make_async_copy(ybuf.at[sl], y_hbm.at[pl.ds(0, bm * SUBLANES), :], ssem.at[sl]).wait()

    def spare_row(set_id, r):
        return TOP_K * n_tok + set_id * bm + r

    @pl.when(i < nact)
    def _():
        @pl.when(i == 0)
        def _():
            gather_start(0, 0, range(bm))
            ybuf[...] = jnp.zeros_like(ybuf)
            for sl in range(2):
                spare = pltpu.make_async_copy(
                    ybuf.at[sl], y_hbm.at[pl.ds(spare_row(sl, 0) * SUBLANES, bm * SUBLANES), :], ssem.at[sl])
                spare.start()
                spare.wait()

        def weight_copies(e):
            return [pltpu.make_async_copy(w_hbm.at[e], stage, wsem.at[k])
                    for k, (w_hbm, stage) in enumerate(((w1_hbm, wf1), (w3_hbm, wf3), (w2_hbm, wf2)))]

        this_e = blk_e_ref[i]

        @pl.when(i == 0)
        def _():
            for cp in weight_copies(this_e):
                cp.start()

        @pl.when((i == 0) | (this_e != blk_e_ref[jnp.maximum(i - 1, 0)]))
        def _():
            for cp in weight_copies(this_e):
                cp.wait()
            for stage, wb in ((wf1, w1b), (wf3, w3b), (wf2, w2b)):
                def cast_rows(rb, carry, stage=stage, wb=wb):
                    rows = pl.ds(pl.multiple_of(rb * LANES, LANES), LANES)
                    wb[rows, :] = stage[rows, :].astype(BF16)
                    return carry
                lax.fori_loop(0, stage.shape[0] // LANES, cast_rows, 0)
            nxt_blk = nxt_ref[i]

            @pl.when(nxt_blk < nact)
            def _():
                for cp in weight_copies(blk_e_ref[nxt_blk]):
                    cp.start(priority=1)

        gather_wait(slot)
        xb = _load_row_tiles(xbuf.at[slot], bm).astype(BF16)
        nxt = jnp.minimum(i + 1, n_blocks - 1)
        prv = jnp.maximum(i - 1, 0)
        has_prev = i > 0
        acts = []
        for c in range(n_chunks):
            cols = slice(c * chunk, (c + 1) * chunk)
            a = _dot(xb, w1b[:, cols])
            g = _dot(xb, w3b[:, cols])
            acts.append((_silu(a) * g).astype(BF16))
            if c < n_chunks // 2:
                gather_start(nxt, other, range(c * rows_per_chunk, (c + 1) * rows_per_chunk))
            else:
                first = (c - n_chunks // 2) * rows_per_chunk
                for r in range(first, first + rows_per_chunk):
                    n = jnp.where(has_prev, src_ref[prv * bm + r], spare_row(1, r))
                    pltpu.make_async_copy(ybuf.at[other, tile(r), :], y_hbm.at[tile(n), :],
                                          ssem.at[other]).start(priority=r % 2)

        y = _dot(jnp.concatenate(acts, axis=1), w2b[...])

        @pl.when(i >= 1)
        def _():
            scatter_wait(slot)

        _store_row_tiles(ybuf.at[slot], y)

        @pl.when(i == nact - 1)
        def _():
            for r in range(bm):
                n = src_ref[i * bm + r]
                pltpu.make_async_copy(ybuf.at[slot, tile(r), :], y_hbm.at[tile(n), :], ssem.at[slot]).start()
            scatter_wait(other)
            scatter_wait(slot)
            gather_wait(other)


def _moe(blk_e, nxt_blk, nact, src, w1, w3, w2, h2_tiles, n_blocks):
    t = h2_tiles.shape[0] // SUBLANES
    d = w1.shape[1]
    de = w1.shape[2]
    bm = MOE_BLOCK
    hbm = pl.BlockSpec(memory_space=pl.ANY)
    grid_spec = pltpu.PrefetchScalarGridSpec(
        num_scalar_prefetch=4,
        grid=(n_blocks,),
        in_specs=[hbm, hbm, hbm, hbm],
        out_specs=hbm,
        scratch_shapes=[pltpu.VMEM((d, de), F32), pltpu.VMEM((d, de), F32), pltpu.VMEM((de, d), F32),
                        pltpu.VMEM((d, de), BF16), pltpu.VMEM((d, de), BF16), pltpu.VMEM((de, d), BF16),
                        pltpu.VMEM((2, bm * SUBLANES, LANES), F32), pltpu.VMEM((2, bm * SUBLANES, LANES), F32),
                        pltpu.SemaphoreType.DMA((2,)), pltpu.SemaphoreType.DMA((2,)),
                        pltpu.SemaphoreType.DMA((3,))],
    )
    return pl.pallas_call(
        functools.partial(_moe_kernel, n_tok=t),
        out_shape=jax.ShapeDtypeStruct(((TOP_K * t + 2 * bm) * SUBLANES, LANES), F32),
        grid_spec=grid_spec,
        compiler_params=_cparams(("arbitrary",)),
        name="moe_experts",
    )(blk_e, nxt_blk, nact, src, w1, w3, w2, h2_tiles)


def _final_kernel(x1_ref, y0_ref, y1_ref, gt_ref, g2_ref, fg_ref, o_ref):
    gt = gt_ref[...]
    tm = x1_ref.shape[0]
    y2 = gt[:, 0:1] * _load_row_tiles(y0_ref, tm) + gt[:, 1:2] * _load_row_tiles(y1_ref, tm)
    x = x1_ref[...] + g2_ref[0] * y2
    o_ref[...] = x * lax.rsqrt(jnp.mean(x * x, axis=-1, keepdims=True) + EPS) * fg_ref[...]


def _final(x1, y, gates_tok, g2, fg, rows_per_mod, tm):
    t, d = x1.shape
    nb = t // tm
    blocks_per_mod = rows_per_mod // tm
    return pl.pallas_call(
        _final_kernel,
        out_shape=jax.ShapeDtypeStruct((t, d), F32),
        grid=(nb,),
        in_specs=[pl.BlockSpec((tm, d), lambda i: (i, 0)),
                  pl.BlockSpec((tm * SUBLANES, LANES), lambda i: (i, 0)),
                  pl.BlockSpec((tm * SUBLANES, LANES), lambda i: (i + nb, 0)),
                  pl.BlockSpec((tm, TOP_K), lambda i: (i, 0)),
                  pl.BlockSpec((1, 1, d), lambda i: (i // blocks_per_mod, 0, 0)),
                  pl.BlockSpec((1, d), lambda i: (0, 0))],
        out_specs=pl.BlockSpec((tm, d), lambda i: (i, 0)),
        compiler_params=_cparams(("arbitrary",)),
        name="combine_final_norm",
    )(x1, y, y, gates_tok, g2, fg)


def kernel(x, c, ctx, c_ctx, w_ada, b_ada, norm1_g, norm2_g, w_in, conv_dw, conv_b, conv_ln_g, conv_ln_b,
           lru_conv_w, lru_conv_b, lru_wa, lru_ba, lru_wx, lru_bx, lru_lam, w_out,
           router_wg, router_bg, router_we, router_be, w1, w3, w2, final_g):
    assert w_ada.shape[0] == 1, "single-layer block"
    assert x.shape[2] == SUBLANES * LANES, "row-tile layout needs one (8, 128) tile per token row"
    b, s, d = x.shape
    n_ctx = ctx.shape[1]
    t = b * s
    cc = conv_dw.shape[2]
    lw = lru_conv_w.shape[2]

    c_rows = jnp.zeros((SUBLANES, d), F32).at[:b].set(c).at[b].set(c_ctx)
    mod = _ada(c_rows, w_ada[0], b_ada)
    mod_l = mod[:b].reshape(b, 6, 1, d)
    sh1, sc1, g1, sh2, sc2, g2 = (mod_l[:, k] for k in range(6))
    mod_c = mod[b].reshape(6, 1, 1, d)
    csh1, csc1 = mod_c[0], mod_c[1]

    w_in_b = w_in[0].astype(BF16)
    w_out_b = w_out[0].astype(BF16)
    heads_per_blk = LRU_LANES // lru_wa.shape[3]
    n_cblk = lw // LRU_LANES

    def blockdiag(wh):
        hd = wh.shape[1]
        wh = wh.reshape(n_cblk, heads_per_blk, hd, hd)
        eye = jnp.eye(heads_per_blk, dtype=wh.dtype)
        return jnp.einsum("chij,hg->chigj", wh, eye).reshape(n_cblk, LRU_LANES, LRU_LANES)

    wg = jnp.concatenate([blockdiag(lru_wa[0, 0]), blockdiag(lru_wx[0, 0]),
                          blockdiag(lru_wa[0, 1]), blockdiag(lru_wx[0, 1])], axis=2).astype(BF16)
    gb = jnp.stack([lru_ba[0, 0], lru_bx[0, 0], lru_ba[0, 1], lru_bx[0, 1]])
    lam = lru_lam[0]

    zc = _inproj_ctx(ctx.reshape(b * n_ctx, d), csh1, csc1, norm1_g, w_in_b[:, 2 * cc:2 * cc + lw], n_ctx)
    h0 = _lru(zc.reshape(b, n_ctx, lw), None, jnp.zeros((2, b, 1, lw), F32),
              lru_conv_w[0], lru_conv_b, wg, gb, lam, True, "rglru_ctx")

    x2 = x.reshape(t, d)
    conv_w_rep = jnp.repeat(conv_dw[0], SUBLANES, axis=0).reshape(CONV_TAPS * SUBLANES, cc // LANES, LANES)
    conv_w_rep = conv_w_rep.transpose(1, 0, 2)
    conv_l, zl, gg = _inproj(x2, sh1, sc1, norm1_g, w_in_b, conv_w_rep, conv_b, conv_ln_g, conv_ln_b, s)
    lru_l = _lru(zl.reshape(b, s, lw), gg.reshape(b, s, lw), h0,
                 lru_conv_w[0], lru_conv_b, wg, gb, lam, False, "rglru")

    wr = jnp.zeros((ROUTE_ROWS, d), F32)
    wr = wr.at[:N_GROUPS].set(router_wg[0].T)
    wr = wr.at[EXPERT_ROW0:EXPERT_ROW0 + N_EXPERTS].set(router_we[0].reshape(d, N_EXPERTS).T)
    rb = jnp.zeros((ROUTE_ROWS, 1), F32)
    rb = rb.at[:N_GROUPS, 0].set(router_bg[0])
    rb = rb.at[EXPERT_ROW0:EXPERT_ROW0 + N_EXPERTS, 0].set(router_be[0].reshape(-1))
    r_hi, r_lo = _split_bf16(wr)
    x1, h2, logits_t = _outproj(conv_l, lru_l.reshape(t, lw), x2, g1, sh2, sc2, norm2_g, w_out_b,
                                r_hi, r_lo, rb, s, TOKEN_BLOCK)

    gates, dest, counts = _route(logits_t, MOE_BLOCK)
    n_blocks = (TOP_K * t) // MOE_BLOCK + N_EXPERTS
    src, blk_e, nxt_blk, nact = _tables(counts[:, 0].astype(jnp.int32), dest.reshape(-1), t, n_blocks, MOE_BLOCK)
    y = _moe(blk_e, nxt_blk, nact, src, w1[0], w3[0], w2[0], h2, n_blocks)
    out = _final(x1, y, gates.T, g2, final_g.reshape(1, d), s, TOKEN_BLOCK)
    return out.reshape(b, s, d)
```

```python
import functools

import jax
import jax.numpy as jnp
from jax import lax
from jax.experimental import pallas as pl
from jax.experimental.pallas import tpu as pltpu

F32 = jnp.float32
BF16 = jnp.bfloat16

EPS = 1e-6
CONV_TAPS = 31
LRU_TAPS = 4
LRU_C = 8.0
GRID_W = 64
N_GROUPS = 4
EXPERTS_PER_GROUP = 8
N_EXPERTS = N_GROUPS * EXPERTS_PER_GROUP
TOP_K = 2

SUBLANES = 8
LANES = 128
TOKEN_BLOCK = 512
LRU_LANES = 128
ROUTE_ROWS = 128
EXPERT_ROW0 = 8
MOE_BLOCK = 256
VMEM_LIMIT = 48 * 1024 * 1024


def _cparams(sem, vmem=VMEM_LIMIT):
    return pltpu.CompilerParams(dimension_semantics=sem, vmem_limit_bytes=vmem)


def _split_bf16(a):
    hi = a.astype(BF16)
    lo = (a - hi.astype(F32)).astype(BF16)
    return hi, lo


def _dot(a, b):
    return jnp.dot(a, b, preferred_element_type=F32)


def _dot_nt(a, b):
    return lax.dot_general(a, b, (((1,), (1,)), ((), ())), preferred_element_type=F32)


def _sigmoid(x):
    return 0.5 * jnp.tanh(0.5 * x) + 0.5


def _silu(x):
    return x * _sigmoid(x)


def _gelu_tanh(x):
    c = 0.7978845608028654
    return 0.5 * x * (1.0 + jnp.tanh(c * (x + 0.044715 * (x * x * x))))


ROW_TILE = SUBLANES


def _store_row_tiles(ref, x):
    rows = x.shape[0]
    for s in range(ROW_TILE):
        ref[pl.ds(s, rows, stride=ROW_TILE), :] = x[:, s * LANES:(s + 1) * LANES]


def _load_row_tiles(ref, rows):
    return jnp.concatenate([ref[pl.ds(s, rows, stride=ROW_TILE), :] for s in range(ROW_TILE)], axis=1)


def _rms_mod(x, g, shift, scale):
    y = x * lax.rsqrt(jnp.mean(x * x, axis=-1, keepdims=True) + EPS)
    return (y * g) * (1.0 + scale) + shift


def _ada_kernel(c_ref, w_ref, b_ref, o_ref):
    a = _silu(c_ref[...])
    a_hi, a_lo = _split_bf16(a)
    w_hi, w_lo = _split_bf16(w_ref[...])
    o_ref[...] = _dot(a_hi, w_hi) + _dot(a_lo, w_hi) + _dot(a_hi, w_lo) + b_ref[...]


def _ada(c_rows, w, b):
    m, d = c_rows.shape
    n = w.shape[1]
    bn = 768
    return pl.pallas_call(
        _ada_kernel,
        out_shape=jax.ShapeDtypeStruct((m, n), F32),
        grid=(n // bn,),
        in_specs=[pl.BlockSpec((m, d), lambda j: (0, 0)),
                  pl.BlockSpec((d, bn), lambda j: (0, j)),
                  pl.BlockSpec((1, bn), lambda j: (0, j))],
        out_specs=pl.BlockSpec((m, bn), lambda j: (0, j)),
        compiler_params=_cparams(("arbitrary",)),
        name="ada_mod",
    )(c_rows, w, b)


def _inproj_ctx_kernel(x_ref, sh_ref, sc_ref, g_ref, w_ref, o_ref):
    h = _rms_mod(x_ref[...], g_ref[...], sh_ref[0], sc_ref[0])
    o_ref[...] = _dot(h.astype(BF16), w_ref[...])


def _conformer_conv(u, cw_ref, cb_ref, lg_ref, lb_ref, o_ref, stg_ref, xt_ref, ot_ref):
    c = u.shape[1]
    slabs = range(c // LANES)
    half = CONV_TAPS // 2
    pitch = GRID_W + SUBLANES
    group = SUBLANES
    for l in slabs:
        for q in range(SUBLANES):
            stg_ref[l, q * pitch:q * pitch + GRID_W, :] = u[q * GRID_W:(q + 1) * GRID_W, l * LANES:(l + 1) * LANES]

    def conv_slab(l, carry):
        for t in range(GRID_W):
            xt_ref[l, t * SUBLANES:(t + 1) * SUBLANES, :] = stg_ref[l, pl.ds(t, SUBLANES, stride=pitch), :]
        for t0 in range(0, GRID_W, group):
            accs = [None] * group
            for k in range(CONV_TAPS):
                srcs = [t0 + j + k - half for j in range(group)]
                if not any(0 <= sidx < GRID_W for sidx in srcs):
                    continue
                wk = cw_ref[l, k * SUBLANES:(k + 1) * SUBLANES, :]
                for j, sidx in enumerate(srcs):
                    if 0 <= sidx < GRID_W:
                        term = wk * xt_ref[l, sidx * SUBLANES:(sidx + 1) * SUBLANES, :]
                        accs[j] = term if accs[j] is None else accs[j] + term
            for j in range(group):
                ot_ref[l, (t0 + j) * SUBLANES:(t0 + j + 1) * SUBLANES, :] = accs[j]
        return carry
    lax.fori_loop(0, c // LANES, conv_slab, 0)
    rows_per_pass = group * SUBLANES
    for r0 in range(0, GRID_W * SUBLANES, rows_per_pass):
        rows = slice(r0, r0 + rows_per_pass)
        acc = jnp.concatenate([ot_ref[l, rows, :] for l in slabs], axis=1) + cb_ref[...]
        mu = jnp.mean(acc, axis=-1, keepdims=True)
        cen = acc - mu
        var = jnp.mean(cen * cen, axis=-1, keepdims=True)
        y = _silu(cen * lax.rsqrt(var + EPS) * lg_ref[...] + lb_ref[...])
        for l in slabs:
            ot_ref[l, rows, :] = y[:, l * LANES:(l + 1) * LANES]
    for q in range(SUBLANES):
        o_ref[q * GRID_W:(q + 1) * GRID_W, :] = jnp.concatenate(
            [ot_ref[l, pl.ds(q, GRID_W, stride=SUBLANES), :] for l in slabs], axis=1).astype(o_ref.dtype)


def _inproj_kernel(x_ref, sh_ref, sc_ref, g_ref, w_ref, cw_ref, cb_ref, lg_ref, lb_ref,
                   cv_ref, zl_ref, gg_ref, stg_ref, xt_ref, ot_ref):
    h = _rms_mod(x_ref[...], g_ref[...], sh_ref[0], sc_ref[0])
    z = _dot(h.astype(BF16), w_ref[...])
    c = cv_ref.shape[1]
    zl_ref[...] = z[:, 2 * c:3 * c]
    gg_ref[...] = _gelu_tanh(z[:, 3 * c:])
    u = z[:, :c] * _sigmoid(z[:, c:2 * c])
    _conformer_conv(u, cw_ref, cb_ref, lg_ref, lb_ref, cv_ref, stg_ref, xt_ref, ot_ref)


def _inproj_ctx(x2, shift, scale, g, w, tm):
    t, d = x2.shape
    n = w.shape[1]
    mod_spec = pl.BlockSpec((1, 1, d), lambda i: (0, 0, 0))
    return pl.pallas_call(
        _inproj_ctx_kernel,
        out_shape=jax.ShapeDtypeStruct((t, n), F32),
        grid=(t // tm,),
        in_specs=[pl.BlockSpec((tm, d), lambda i: (i, 0)), mod_spec, mod_spec,
                  pl.BlockSpec((1, d), lambda i: (0, 0)),
                  pl.BlockSpec((d, n), lambda i: (0, 0))],
        out_specs=pl.BlockSpec((tm, n), lambda i: (i, 0)),
        compiler_params=_cparams(("arbitrary",)),
        name="in_proj_ctx",
    )(x2, shift, scale, g, w)


def _inproj(x2, shift, scale, g, w, cw, cb, lg, lb, rows_per_mod):
    t, d = x2.shape
    n = w.shape[1]
    c = n // 4
    tm = SUBLANES * GRID_W
    blocks_per_mod = rows_per_mod // tm
    mod_spec = pl.BlockSpec((1, 1, d), lambda i: (i // blocks_per_mod, 0, 0))
    const = lambda a: pl.BlockSpec(a.shape, lambda i: (0,) * a.ndim)
    rows = pl.BlockSpec((tm, c), lambda i: (i, 0))
    return pl.pallas_call(
        _inproj_kernel,
        out_shape=[jax.ShapeDtypeStruct((t, c), BF16), jax.ShapeDtypeStruct((t, c), F32),
                   jax.ShapeDtypeStruct((t, c), F32)],
        grid=(t // tm,),
        in_specs=[pl.BlockSpec((tm, d), lambda i: (i, 0)), mod_spec, mod_spec, const(g), const(w),
                  const(cw), const(cb), const(lg), const(lb)],
        out_specs=[rows, rows, rows],
        scratch_shapes=[pltpu.VMEM((c // LANES, SUBLANES * (GRID_W + SUBLANES), LANES), F32),
                        pltpu.VMEM((c // LANES, tm, LANES), F32), pltpu.VMEM((c // LANES, tm, LANES), F32)],
        compiler_params=_cparams(("arbitrary",)),
        name="in_proj",
    )(x2, shift, scale, g, w, cw, cb, lg, lb)


def _lru_kernel(zl_ref, h0_ref, cw_ref, cb_ref, wg_ref, gb_ref, lam_ref, *rest, seq, final_only):
    if final_only:
        o_ref, stage_ref, zt_ref, af_ref, bf_ref, ab_ref, bb_ref, pf_ref, hf_ref, pb_ref, hb_ref = rest
        gg_ref = None
    else:
        gg_ref, o_ref, stage_ref, zt_ref, af_ref, bf_ref, ab_ref, bb_ref, pf_ref, hf_ref, pb_ref, hb_ref = rest
    cl = seq // SUBLANES
    pitch = cl + SUBLANES
    c = zl_ref.shape[2]
    a_refs = (af_ref, ab_ref)
    b_refs = (bf_ref, bb_ref)
    halo = 2

    def tile_rows(t):
        return pl.ds(pl.multiple_of(t * SUBLANES, SUBLANES), SUBLANES)

    for j in range(SUBLANES):
        stage_ref[j * pitch:j * pitch + cl, :] = zl_ref[0, j * cl:(j + 1) * cl, :]

    def to_chunk_layout(t, carry):
        zt_ref[tile_rows(t + halo), :] = stage_ref[pl.ds(t, SUBLANES, stride=pitch), :]
        return carry
    lax.fori_loop(0, cl, to_chunk_layout, 0, unroll=8)

    sub = lax.broadcasted_iota(jnp.int32, (SUBLANES, c), 0)
    for t_src, t_dst in ((cl - 2, -2), (cl - 1, -1)):
        v = pltpu.roll(zt_ref[(t_src + halo) * SUBLANES:(t_src + halo + 1) * SUBLANES, :], 1, 0)
        zt_ref[(t_dst + halo) * SUBLANES:(t_dst + halo + 1) * SUBLANES, :] = jnp.where(sub == 0, 0.0, v)
    v = pltpu.roll(zt_ref[halo * SUBLANES:(halo + 1) * SUBLANES, :], SUBLANES - 1, 0)
    zt_ref[(cl + halo) * SUBLANES:(cl + halo + 1) * SUBLANES, :] = jnp.where(sub == SUBLANES - 1, 0.0, v)

    lam = lam_ref[...]
    nlam = -lam
    softplus = jnp.maximum(nlam, 0.0) + jnp.log1p(jnp.exp(-jnp.abs(nlam)))
    half_decay = (-0.5 * LRU_C) * softplus
    wg = wg_ref[0]

    piece = cl
    for p0 in range(0, cl * SUBLANES, piece):
        ul = cb_ref[...] + jnp.zeros((piece, c), F32)
        for k in range(LRU_TAPS):
            off = p0 + k * SUBLANES
            ul = ul + cw_ref[k:k + 1, :] * zt_ref[off:off + piece, :]
        g = _dot(ul.astype(BF16), wg)
        for d in range(2):
            t_r = jnp.tanh(g[:, (2 * d) * c:(2 * d + 1) * c] + gb_ref[2 * d:2 * d + 1, :])
            i = 0.5 * jnp.tanh(g[:, (2 * d + 1) * c:(2 * d + 2) * c] + gb_ref[2 * d + 1:2 * d + 2, :]) + 0.5
            log_a = half_decay[d:d + 1, :] * t_r + half_decay[d:d + 1, :]
            a = jnp.exp(log_a)
            m = jnp.maximum(-jnp.tanh(log_a) * (a * a + 1.0), 1e-12)
            mult = m * lax.rsqrt(m)
            a_refs[d][p0:p0 + piece, :] = a
            b_refs[d][p0:p0 + piece, :] = mult * i * ul

    def two_steps(a_ref, b_ref, p_out, h_out, s0, s1, h, p):
        a0 = a_ref[s0, :]
        a1 = a_ref[s1, :]
        b0 = b_ref[s0, :]
        a10 = a1 * a0
        b10 = a1 * b0 + b_ref[s1, :]
        h_out[s0, :] = a0 * h + b0
        p_out[s0, :] = a0 * p
        h = a10 * h + b10
        p = a10 * p
        h_out[s1, :] = h
        p_out[s1, :] = p
        return h, p

    group = 8
    group_rows = group * SUBLANES

    def step(n, carry):
        hf, pf, hb, pb = carry
        base_f = pl.multiple_of(n * group_rows, group_rows)
        base_b = pl.multiple_of((cl // group - 1 - n) * group_rows, group_rows)
        for k in range(0, group, 2):
            tf0 = pl.ds(base_f + k * SUBLANES, SUBLANES)
            tf1 = pl.ds(base_f + (k + 1) * SUBLANES, SUBLANES)
            hf, pf = two_steps(af_ref, bf_ref, pf_ref, hf_ref, tf0, tf1, hf, pf)
            tb0 = pl.ds(base_b + (group - 1 - k) * SUBLANES, SUBLANES)
            tb1 = pl.ds(base_b + (group - 2 - k) * SUBLANES, SUBLANES)
            hb, pb = two_steps(ab_ref, bb_ref, pb_ref, hb_ref, tb0, tb1, hb, pb)
        return hf, pf, hb, pb

    zero = jnp.zeros((SUBLANES, c), F32)
    one = jnp.ones((SUBLANES, c), F32)
    lax.fori_loop(0, cl // group, step, (zero, one, zero, one))

    cf = [None] * SUBLANES
    cbk = [None] * SUBLANES
    s = h0_ref[0, 0]
    for j in range(SUBLANES):
        cf[j] = s
        last = (cl - 1) * SUBLANES + j
        s = hf_ref[last:last + 1, :] + pf_ref[last:last + 1, :] * s
    final_f = s
    s = h0_ref[1, 0]
    for j in reversed(range(SUBLANES)):
        cbk[j] = s
        s = hb_ref[j:j + 1, :] + pb_ref[j:j + 1, :] * s
    final_b = s

    if final_only:
        o_ref[0, 0] = final_f
        o_ref[1, 0] = final_b
        return
    carry_f = jnp.concatenate(cf, axis=0)
    carry_b = jnp.concatenate(cbk, axis=0)

    def to_natural(t, carry):
        rows = tile_rows(t)
        h = (hf_ref[rows, :] + pf_ref[rows, :] * carry_f) + (hb_ref[rows, :] + pb_ref[rows, :] * carry_b)
        stage_ref[pl.ds(t, SUBLANES, stride=pitch), :] = h
        return carry
    lax.fori_loop(0, cl, to_natural, 0, unroll=8)
    for j in range(SUBLANES):
        h = stage_ref[j * pitch:j * pitch + cl, :]
        o_ref[0, j * cl:(j + 1) * cl, :] = (h * gg_ref[0, j * cl:(j + 1) * cl, :]).astype(o_ref.dtype)


def _lru(zl, gg, h0, cw, cb, wg, gb, lam, final_only, name):
    b, seq, c = zl.shape
    cbk = LRU_LANES
    cl = seq // SUBLANES
    pitch = cl + SUBLANES
    seq_spec = pl.BlockSpec((1, seq, cbk), lambda bi, ci: (bi, 0, ci))
    st_spec = pl.BlockSpec((2, 1, 1, cbk), lambda bi, ci: (0, bi, 0, ci))
    chan = lambda rows: pl.BlockSpec((rows, cbk), lambda bi, ci: (0, ci))
    in_specs = [seq_spec, st_spec, chan(LRU_TAPS), chan(1),
                pl.BlockSpec((1, cbk, 4 * cbk), lambda bi, ci: (ci, 0, 0)), chan(4), chan(2)]
    args = [zl, h0, cw, cb, wg, gb, lam]
    if final_only:
        out_shape = jax.ShapeDtypeStruct((2, b, 1, c), F32)
        out_spec = st_spec
    else:
        in_specs.append(seq_spec)
        args.append(gg)
        out_shape = jax.ShapeDtypeStruct((b, seq, c), BF16)
        out_spec = seq_spec
    coef = pltpu.VMEM((seq, cbk), F32)
    return pl.pallas_call(
        functools.partial(_lru_kernel, seq=seq, final_only=final_only),
        out_shape=out_shape,
        grid=(b, c // cbk),
        in_specs=in_specs,
        out_specs=out_spec,
        scratch_shapes=[pltpu.VMEM((SUBLANES * pitch, cbk), F32),
                        pltpu.VMEM((seq + 2 * SUBLANES * SUBLANES, cbk), F32)] + [coef] * 8,
        compiler_params=_cparams(("arbitrary", "arbitrary")),
        name=name,
    )(*args)


def _outproj_kernel(cv_ref, lr_ref, x_ref, g1_ref, sh_ref, sc_ref, ng_ref, wo_ref, rh_ref, rl_ref, rb_ref,
                    x1_ref, h2_ref, lg_ref):
    c = cv_ref.shape[1]
    y = _dot(cv_ref[...], wo_ref[0:c, :]) + _dot(lr_ref[...], wo_ref[c:2 * c, :])
    x1 = x_ref[...] + g1_ref[0] * y
    x1_ref[...] = x1
    h2 = _rms_mod(x1, ng_ref[...], sh_ref[0], sc_ref[0])
    h_hi, h_lo = _split_bf16(h2)
    _store_row_tiles(h2_ref, h_hi.astype(F32))
    lg_ref[...] = (_dot_nt(rh_ref[...], h_hi) + _dot_nt(rl_ref[...], h_hi) + _dot_nt(rh_ref[...], h_lo)
                   + rb_ref[...])


def _outproj(conv_l, lru_l, x2, g1, sh2, sc2, ng, wo, r_hi, r_lo, r_b, rows_per_mod, tm):
    t, d = x2.shape
    c = conv_l.shape[1]
    blocks_per_mod = rows_per_mod // tm
    mod_spec = pl.BlockSpec((1, 1, d), lambda i: (i // blocks_per_mod, 0, 0))
    full = lambda a: pl.BlockSpec(a.shape, lambda i: (0, 0))
    return pl.pallas_call(
        _outproj_kernel,
        out_shape=[jax.ShapeDtypeStruct((t, d), F32), jax.ShapeDtypeStruct((t * ROW_TILE, LANES), F32),
                   jax.ShapeDtypeStruct((ROUTE_ROWS, t), F32)],
        grid=(t // tm,),
        in_specs=[pl.BlockSpec((tm, c), lambda i: (i, 0)), pl.BlockSpec((tm, c), lambda i: (i, 0)),
                  pl.BlockSpec((tm, d), lambda i: (i, 0)), mod_spec, mod_spec, mod_spec,
                  full(ng), full(wo), full(r_hi), full(r_lo), full(r_b)],
        out_specs=[pl.BlockSpec((tm, d), lambda i: (i, 0)), pl.BlockSpec((tm * ROW_TILE, LANES), lambda i: (i, 0)),
                   pl.BlockSpec((ROUTE_ROWS, tm), lambda i: (0, i))],
        compiler_params=_cparams(("arbitrary",)),
        name="out_proj_router",
    )(conv_l, lru_l, x2, g1, sh2, sc2, ng, wo, r_hi, r_lo, r_b)


def _route_chunk(lg_ref, lanes):
    n = lanes.size
    e = EXPERTS_PER_GROUP
    lgrp = lg_ref[0:N_GROUPS, lanes]
    gidx = lax.broadcasted_iota(jnp.int32, (N_GROUPS, n), 0)
    m = jnp.max(lgrp, axis=0, keepdims=True)
    ex = jnp.exp(lgrp - m)
    pg = ex / jnp.sum(ex, axis=0, keepdims=True)
    p_grp = jnp.max(pg, axis=0, keepdims=True)
    grp = jnp.min(jnp.where(pg == p_grp, gidx, N_GROUPS), axis=0, keepdims=True)
    le = jnp.zeros((e, n), F32)
    for g in range(N_GROUPS):
        rows = lg_ref[EXPERT_ROW0 + g * e:EXPERT_ROW0 + (g + 1) * e, lanes]
        le = jnp.where(grp == g, rows, le)
    m = jnp.max(le, axis=0, keepdims=True)
    ex = jnp.exp(le - m)
    pe = ex / jnp.sum(ex, axis=0, keepdims=True)
    eidx = lax.broadcasted_iota(jnp.int32, (e, n), 0)
    p1 = jnp.max(pe, axis=0, keepdims=True)
    i1 = jnp.min(jnp.where(pe == p1, eidx, e), axis=0, keepdims=True)
    pe2 = jnp.where(eidx == i1, -1.0, pe)
    p2 = jnp.max(pe2, axis=0, keepdims=True)
    i2 = jnp.min(jnp.where(pe2 == p2, eidx, e), axis=0, keepdims=True)
    denom = p1 + p2
    base = grp * e
    return (base + i1, base + i2), (p_grp * p1 / denom, p_grp * p2 / denom)


def _route_kernel(lg_ref, gate_ref, dest_ref, cnt_ref, run_ref, *, block_rows):
    p = pl.program_id(0)
    i = pl.program_id(1)
    tl = lg_ref.shape[1]
    chunk = 2 * LANES
    ne = N_EXPERTS

    @pl.when((p == 0) & (i == 0))
    def _():
        run_ref[...] = jnp.zeros_like(run_ref)

    @pl.when((p == 1) & (i == 0))
    def _():
        counts = run_ref[...]
        cnt_ref[...] = counts
        padded = jnp.ceil(counts * (1.0 / block_rows)) * block_rows
        rows = []
        start = jnp.zeros((1, LANES), F32)
        for ex in range(ne):
            rows.append(start)
            start = start + padded[ex:ex + 1, :]
        run_ref[...] = jnp.concatenate(rows, axis=0)

    tri = (lax.broadcasted_iota(jnp.int32, (chunk, chunk), 0)
           <= lax.broadcasted_iota(jnp.int32, (chunk, chunk), 1)).astype(BF16)
    eidx = lax.broadcasted_iota(jnp.int32, (ne, chunk), 0)
    for cix in range(tl // chunk):
        lanes = pl.ds(cix * chunk, chunk)
        eids, gates = _route_chunk(lg_ref, lanes)
        dests = []
        for k in range(TOP_K):
            onehot = eidx == eids[k]
            cum = _dot(onehot.astype(BF16), tri)
            run = run_ref[...][:, 0:1]
            dests.append(jnp.sum(jnp.where(onehot, run + cum - 1.0, 0.0), axis=0, keepdims=True))
            run_ref[...] = run_ref[...] + cum[:, chunk - 1:chunk]
        gate_ref[:, lanes] = jnp.concatenate(gates, axis=0)
        dest_ref[:, lanes] = jnp.concatenate(dests, axis=0).astype(jnp.int32)


def _route(logits_t, block_rows):
    rows, t = logits_t.shape
    tl = 2048
    blk = pl.BlockSpec((TOP_K, tl), lambda p, i: (0, i * p))
    return pl.pallas_call(
        functools.partial(_route_kernel, block_rows=block_rows),
        out_shape=[jax.ShapeDtypeStruct((TOP_K, t), F32), jax.ShapeDtypeStruct((TOP_K, t), jnp.int32),
                   jax.ShapeDtypeStruct((N_EXPERTS, LANES), F32)],
        grid=(2, t // tl),
        in_specs=[pl.BlockSpec((rows, tl), lambda p, i: (0, i))],
        out_specs=[blk, blk, pl.BlockSpec((N_EXPERTS, LANES), lambda p, i: (0, 0))],
        scratch_shapes=[pltpu.VMEM((N_EXPERTS, LANES), F32)],
        compiler_params=_cparams(("arbitrary", "arbitrary")),
        name="route",
    )(logits_t)


def _tables_kernel(cnt_ref, dest_ref, src_ref, blk_e_ref, nxt_ref, nact_ref, *, n_tok, bm):
    i = pl.program_id(0)
    cb = dest_ref.shape[0]
    n_rows = src_ref.shape[0]
    n_blocks = blk_e_ref.shape[0]
    shift = bm.bit_length() - 1

    group = 8

    @pl.when(i == 0)
    def _():
        def per_expert(ex, run):
            cnt = cnt_ref[ex]
            padded = ((cnt + (bm - 1)) >> shift) << shift
            end = run + padded
            end_blk = end >> shift

            def fill_blk(b, carry):
                blk_e_ref[b] = ex
                nxt_ref[b] = end_blk
                return carry
            lax.fori_loop(run >> shift, end_blk, fill_blk, 0)

            last_blk = jnp.maximum(end_blk - 1, 0)
            offset = TOP_K * n_tok + (last_blk & 1) * bm - (last_blk << shift)

            def fill_pad(g, carry):
                for u in range(group):
                    r = jnp.minimum(run + cnt + g * group + u, end - 1)
                    src_ref[r] = r + offset
                return carry
            lax.fori_loop(0, (padded - cnt + (group - 1)) // group, fill_pad, 0)
            return end
        total = lax.fori_loop(0, N_EXPERTS, per_expert, 0)
        nact_ref[0] = total >> shift

        def fill_blk(b, carry):
            blk_e_ref[b] = N_EXPERTS - 1
            nxt_ref[b] = n_blocks
            return carry
        lax.fori_loop(total >> shift, n_blocks, fill_blk, 0)

        def fill_tail(g, carry):
            for u in range(group):
                src_ref[total + g * group + u] = 0
            return carry
        lax.fori_loop(0, (n_rows - total) // group, fill_tail, 0)

    base = i * cb

    def invert(j, carry):
        src_ref[dest_ref[j]] = base + j
        return carry
    lax.fori_loop(0, cb, invert, 0, unroll=16)


def _tables(counts, dest_flat, n_tok, n_blocks, bm):
    cb = 4096
    smem = functools.partial(pl.BlockSpec, memory_space=pltpu.SMEM)
    return pl.pallas_call(
        functools.partial(_tables_kernel, n_tok=n_tok, bm=bm),
        out_shape=[jax.ShapeDtypeStruct((n_blocks * bm,), jnp.int32),
                   jax.ShapeDtypeStruct((n_blocks,), jnp.int32),
                   jax.ShapeDtypeStruct((n_blocks,), jnp.int32),
                   jax.ShapeDtypeStruct((1,), jnp.int32)],
        grid=(dest_flat.shape[0] // cb,),
        in_specs=[smem(), smem((cb,), lambda i: (i,))],
        out_specs=[smem(), smem(), smem(), smem()],
        compiler_params=_cparams(("arbitrary",)),
        name="dispatch_tables",
    )(counts, dest_flat)


def _moe_kernel(blk_e_ref, nxt_ref, nact_ref, src_ref,
                w1_hbm, w3_hbm, w2_hbm, h_hbm, y_hbm,
                wf1, wf3, wf2, w1b, w3b, w2b, xbuf, ybuf, gsem, ssem, wsem, *, n_tok):
    i = pl.program_id(0)
    nact = nact_ref[0]
    n_blocks = blk_e_ref.shape[0]
    bm = xbuf.shape[1] // ROW_TILE
    de = w1b.shape[1]
    slot = lax.rem(i, 2)
    other = 1 - slot
    chunk = 2 * LANES
    n_chunks = de // chunk
    rows_per_chunk = 2 * bm // n_chunks

    def tile(row):
        return pl.ds(pl.multiple_of(row * ROW_TILE, ROW_TILE), ROW_TILE)

    def token_of(n):
        if n_tok & (n_tok - 1) == 0:
            return n & (n_tok - 1)
        return lax.rem(n, n_tok)

    def gather_start(blk, sl, rows):
        for r in rows:
            tok = token_of(src_ref[blk * bm + r])
            pltpu.make_async_copy(h_hbm.at[tile(tok), :], xbuf.at[sl, tile(r), :], gsem.at[sl]).start()

    def gather_wait(sl):
        pltpu.make_async_copy(h_hbm.at[pl.ds(0, bm * ROW_TILE), :], xbuf.at[sl], gsem.at[sl]).wait()

    def scatter_wait(sl):
        pltpu.make_async_copy(ybuf.at[sl], y_hbm.at[pl.ds(0, bm * ROW_TILE), :], ssem.at[sl]).wait()

    def spare_row(set_id, r):
        return TOP_K * n_tok + set_id * bm + r

    @pl.when(i < nact)
    def _():
        @pl.when(i == 0)
        def _():
            gather_start(0, 0, range(bm))
            ybuf[...] = jnp.zeros_like(ybuf)
            for sl in range(2):
                spare = pltpu.make_async_copy(
                    ybuf.at[sl], y_hbm.at[pl.ds(spare_row(sl, 0) * ROW_TILE, bm * ROW_TILE), :], ssem.at[sl])
                spare.start()
                spare.wait()

        def weight_copies(e):
            return [pltpu.make_async_copy(w_hbm.at[e], stage, wsem.at[k])
                    for k, (w_hbm, stage) in enumerate(((w1_hbm, wf1), (w3_hbm, wf3), (w2_hbm, wf2)))]

        this_e = blk_e_ref[i]

        @pl.when(i == 0)
        def _():
            for cp in weight_copies(this_e):
                cp.start()

        @pl.when((i == 0) | (this_e != blk_e_ref[jnp.maximum(i - 1, 0)]))
        def _():
            for cp in weight_copies(this_e):
                cp.wait()
            for stage, wb in ((wf1, w1b), (wf3, w3b), (wf2, w2b)):
                def cast_rows(rb, carry, stage=stage, wb=wb):
                    rows = pl.ds(pl.multiple_of(rb * LANES, LANES), LANES)
                    wb[rows, :] = stage[rows, :].astype(BF16)
                    return carry
                lax.fori_loop(0, stage.shape[0] // LANES, cast_rows, 0)
            nxt_blk = nxt_ref[i]

            @pl.when(nxt_blk < nact)
            def _():
                for cp in weight_copies(blk_e_ref[nxt_blk]):
                    cp.start(priority=1)

        gather_wait(slot)
        xb = _load_row_tiles(xbuf.at[slot], bm).astype(BF16)
        nxt = jnp.minimum(i + 1, n_blocks - 1)
        prv = jnp.maximum(i - 1, 0)
        has_prev = i > 0
        acts = []
        for c in range(n_chunks):
            cols = slice(c * chunk, (c + 1) * chunk)
            a = _dot(xb, w1b[:, cols])
            g = _dot(xb, w3b[:, cols])
            acts.append((_silu(a) * g).astype(BF16))
            if c < n_chunks // 2:
                gather_start(nxt, other, range(c * rows_per_chunk, (c + 1) * rows_per_chunk))
            else:
                first = (c - n_chunks // 2) * rows_per_chunk
                for r in range(first, first + rows_per_chunk):
                    n = jnp.where(has_prev, src_ref[prv * bm + r], spare_row(1, r))
                    pltpu.make_async_copy(ybuf.at[other, tile(r), :], y_hbm.at[tile(n), :],
                                          ssem.at[other]).start(priority=r % 2)

        y = _dot(jnp.concatenate(acts, axis=1), w2b[...])

        @pl.when(i >= 1)
        def _():
            scatter_wait(slot)

        _store_row_tiles(ybuf.at[slot], y)

        @pl.when(i == nact - 1)
        def _():
            for r in range(bm):
                n = src_ref[i * bm + r]
                pltpu.make_async_copy(ybuf.at[slot, tile(r), :], y_hbm.at[tile(n), :], ssem.at[slot]).start()
            scatter_wait(other)
            scatter_wait(slot)
            gather_wait(other)


def _moe(blk_e, nxt_blk, nact, src, w1, w3, w2, h2_tiles, n_blocks):
    t = h2_tiles.shape[0] // ROW_TILE
    d = w1.shape[1]
    de = w1.shape[2]
    bm = MOE_BLOCK
    hbm = pl.BlockSpec(memory_space=pl.ANY)
    grid_spec = pltpu.PrefetchScalarGridSpec(
        num_scalar_prefetch=4,
        grid=(n_blocks,),
        in_specs=[hbm, hbm, hbm, hbm],
        out_specs=hbm,
        scratch_shapes=[pltpu.VMEM((d, de), F32), pltpu.VMEM((d, de), F32), pltpu.VMEM((de, d), F32),
                        pltpu.VMEM((d, de), BF16), pltpu.VMEM((d, de), BF16), pltpu.VMEM((de, d), BF16),
                        pltpu.VMEM((2, bm * ROW_TILE, LANES), F32), pltpu.VMEM((2, bm * ROW_TILE, LANES), F32),
                        pltpu.SemaphoreType.DMA((2,)), pltpu.SemaphoreType.DMA((2,)),
                        pltpu.SemaphoreType.DMA((3,))],
    )
    return pl.pallas_call(
        functools.partial(_moe_kernel, n_tok=t),
        out_shape=jax.ShapeDtypeStruct(((TOP_K * t + 2 * bm) * ROW_TILE, LANES), F32),
        grid_spec=grid_spec,
        compiler_params=_cparams(("arbitrary",)),
        name="moe_experts",
    )(blk_e, nxt_blk, nact, src, w1, w3, w2, h2_tiles)


def _final_kernel(x1_ref, y0_ref, y1_ref, gt_ref, g2_ref, fg_ref, o_ref):
    gt = gt_ref[...]
    tm = x1_ref.shape[0]
    y2 = gt[:, 0:1] * _load_row_tiles(y0_ref, tm) + gt[:, 1:2] * _load_row_tiles(y1_ref, tm)
    x = x1_ref[...] + g2_ref[0] * y2
    o_ref[...] = x * lax.rsqrt(jnp.mean(x * x, axis=-1, keepdims=True) + EPS) * fg_ref[...]


def _final(x1, y, gates_tok, g2, fg, rows_per_mod, tm):
    t, d = x1.shape
    nb = t // tm
    blocks_per_mod = rows_per_mod // tm
    return pl.pallas_call(
        _final_kernel,
        out_shape=jax.ShapeDtypeStruct((t, d), F32),
        grid=(nb,),
        in_specs=[pl.BlockSpec((tm, d), lambda i: (i, 0)),
                  pl.BlockSpec((tm * ROW_TILE, LANES), lambda i: (i, 0)),
                  pl.BlockSpec((tm * ROW_TILE, LANES), lambda i: (i + nb, 0)),
                  pl.BlockSpec((tm, TOP_K), lambda i: (i, 0)),
                  pl.BlockSpec((1, 1, d), lambda i: (i // blocks_per_mod, 0, 0)),
                  pl.BlockSpec((1, d), lambda i: (0, 0))],
        out_specs=pl.BlockSpec((tm, d), lambda i: (i, 0)),
        compiler_params=_cparams(("arbitrary",)),
        name="combine_final_norm",
    )(x1, y, y, gates_tok, g2, fg)


def kernel(x, c, ctx, c_ctx, w_ada, b_ada, norm1_g, norm2_g, w_in, conv_dw, conv_b, conv_ln_g, conv_ln_b,
           lru_conv_w, lru_conv_b, lru_wa, lru_ba, lru_wx, lru_bx, lru_lam, w_out,
           router_wg, router_bg, router_we, router_be, w1, w3, w2, final_g):
    assert w_ada.shape[0] == 1, "single-layer block"
    assert x.shape[2] == ROW_TILE * LANES, "row-tile layout: one (ROW_TILE, LANES) tile per token row"
    b, s, d = x.shape
    n_ctx = ctx.shape[1]
    t = b * s
    cc = conv_dw.shape[2]
    lw = lru_conv_w.shape[2]

    c_rows = jnp.zeros((SUBLANES, d), F32).at[:b].set(c).at[b].set(c_ctx)
    mod = _ada(c_rows, w_ada[0], b_ada)
    mod_l = mod[:b].reshape(b, 6, 1, d)
    sh1, sc1, g1, sh2, sc2, g2 = (mod_l[:, k] for k in range(6))
    mod_c = mod[b].reshape(6, 1, 1, d)
    csh1, csc1 = mod_c[0], mod_c[1]

    w_in_b = w_in[0].astype(BF16)
    w_out_b = w_out[0].astype(BF16)
    heads_per_blk = LRU_LANES // lru_wa.shape[3]
    n_cblk = lw // LRU_LANES

    def blockdiag(wh):
        hd = wh.shape[1]
        wh = wh.reshape(n_cblk, heads_per_blk, hd, hd)
        eye = jnp.eye(heads_per_blk, dtype=wh.dtype)
        return jnp.einsum("chij,hg->chigj", wh, eye).reshape(n_cblk, LRU_LANES, LRU_LANES)

    wg = (0.5 * jnp.concatenate([blockdiag(lru_wa[0, 0]), blockdiag(lru_wx[0, 0]),
                                 blockdiag(lru_wa[0, 1]), blockdiag(lru_wx[0, 1])], axis=2)).astype(BF16)
    gb = 0.5 * jnp.stack([lru_ba[0, 0], lru_bx[0, 0], lru_ba[0, 1], lru_bx[0, 1]])
    lam = lru_lam[0]

    zc = _inproj_ctx(ctx.reshape(b * n_ctx, d), csh1, csc1, norm1_g, w_in_b[:, 2 * cc:2 * cc + lw], n_ctx)
    h0 = _lru(zc.reshape(b, n_ctx, lw), None, jnp.zeros((2, b, 1, lw), F32),
              lru_conv_w[0], lru_conv_b, wg, gb, lam, True, "rglru_ctx")

    x2 = x.reshape(t, d)
    conv_w_rep = jnp.repeat(conv_dw[0], SUBLANES, axis=0).reshape(CONV_TAPS * SUBLANES, cc // LANES, LANES)
    conv_w_rep = conv_w_rep.transpose(1, 0, 2)
    conv_l, zl, gg = _inproj(x2, sh1, sc1, norm1_g, w_in_b, conv_w_rep, conv_b, conv_ln_g, conv_ln_b, s)
    lru_l = _lru(zl.reshape(b, s, lw), gg.reshape(b, s, lw), h0,
                 lru_conv_w[0], lru_conv_b, wg, gb, lam, False, "rglru")

    wr = jnp.zeros((ROUTE_ROWS, d), F32)
    wr = wr.at[:N_GROUPS].set(router_wg[0].T)
    wr = wr.at[EXPERT_ROW0:EXPERT_ROW0 + N_EXPERTS].set(router_we[0].reshape(d, N_EXPERTS).T)
    rb = jnp.zeros((ROUTE_ROWS, 1), F32)
    rb = rb.at[:N_GROUPS, 0].set(router_bg[0])
    rb = rb.at[EXPERT_ROW0:EXPERT_ROW0 + N_EXPERTS, 0].set(router_be[0].reshape(-1))
    r_hi, r_lo = _split_bf16(wr)
    x1, h2, logits_t = _outproj(conv_l, lru_l.reshape(t, lw), x2, g1, sh2, sc2, norm2_g, w_out_b,
                                r_hi, r_lo, rb, s, TOKEN_BLOCK)

    gates, dest, counts = _route(logits_t, MOE_BLOCK)
    n_blocks = (TOP_K * t) // MOE_BLOCK + N_EXPERTS
    src, blk_e, nxt_blk, nact = _tables(counts[:, 0].astype(jnp.int32), dest.reshape(-1), t, n_blocks, MOE_BLOCK)
    y = _moe(blk_e, nxt_blk, nact, src, w1[0], w3[0], w2[0], h2, n_blocks)
    out = _final(x1, y, gates.T, g2, final_g.reshape(1, d), s, TOKEN_BLOCK)
    return out.reshape(b, s, d)
```

```python
import functools

import jax
import jax.numpy as jnp
from jax import lax
from jax.experimental import pallas as pl
from jax.experimental.pallas import tpu as pltpu

F32 = jnp.float32
BF16 = jnp.bfloat16

EPS = 1e-6
CONV_TAPS = 31
LRU_TAPS = 4
LRU_C = 8.0
GRID_W = 64
N_GROUPS = 4
EXPERTS_PER_GROUP = 8
N_EXPERTS = N_GROUPS * EXPERTS_PER_GROUP
TOP_K = 2

SUBLANES = 8
LANES = 128
TOKEN_BLOCK = 512
LRU_LANES = 128
EXPERT_ROW0 = 8
ROUTE_ROWS = EXPERT_ROW0 + N_EXPERTS
MOE_BLOCK = 256
MOE_RING = 3
ROUTE_LANES = 2048
ADA_COLS = 1536
V7X_VMEM_BYTES = 64 * 1024 * 1024
VMEM_LIMIT = 3 * V7X_VMEM_BYTES // 4


def _cparams(sem, vmem=VMEM_LIMIT):
    return pltpu.CompilerParams(dimension_semantics=sem, vmem_limit_bytes=vmem)


def _split_bf16(a):
    hi = a.astype(BF16)
    lo = (a - hi.astype(F32)).astype(BF16)
    return hi, lo


def _dot(a, b):
    return jnp.dot(a, b, preferred_element_type=F32)


def _dot_nt(a, b):
    return lax.dot_general(a, b, (((1,), (1,)), ((), ())), preferred_element_type=F32)


def _times_sigmoid(v, x):
    hv = 0.5 * v
    return hv * jnp.tanh(0.5 * x) + hv


def _silu(x):
    h = 0.5 * x
    return h * jnp.tanh(h) + h


def _gelu_tanh(x):
    c = 0.7978845608028654
    h = 0.5 * x
    return h * jnp.tanh(x * ((c * 0.044715) * (x * x) + c)) + h


ROW_TILE = SUBLANES


def _store_row_tiles(ref, x):
    rows = x.shape[0]
    for s in range(ROW_TILE):
        ref[pl.ds(s, rows, stride=ROW_TILE), :] = x[:, s * LANES:(s + 1) * LANES]


def _load_row_tiles(ref, rows):
    return jnp.concatenate([ref[pl.ds(s, rows, stride=ROW_TILE), :] for s in range(ROW_TILE)], axis=1)


def _rms_mod(x, g, shift, scale):
    y = x * lax.rsqrt(jnp.mean(x * x, axis=-1, keepdims=True) + EPS)
    return y * (g * (1.0 + scale)) + shift


def _ada_kernel(c_ref, w_ref, b_ref, o_ref):
    a = _silu(c_ref[...])
    a_hi, a_lo = _split_bf16(a)
    w_hi, w_lo = _split_bf16(w_ref[...])
    o_ref[...] = _dot(a_hi, w_hi) + _dot(a_lo, w_hi) + _dot(a_hi, w_lo) + b_ref[...]


def _ada(c_rows, w, b):
    m, d = c_rows.shape
    n = w.shape[1]
    bn = ADA_COLS
    return pl.pallas_call(
        _ada_kernel,
        out_shape=jax.ShapeDtypeStruct((m, n), F32),
        grid=(n // bn,),
        in_specs=[pl.BlockSpec((m, d), lambda j: (0, 0)),
                  pl.BlockSpec((d, bn), lambda j: (0, j)),
                  pl.BlockSpec((1, bn), lambda j: (0, j))],
        out_specs=pl.BlockSpec((m, bn), lambda j: (0, j)),
        compiler_params=_cparams(("arbitrary",)),
        name="ada_mod",
    )(c_rows, w, b)


def _inproj_ctx_kernel(x_ref, sh_ref, sc_ref, g_ref, w_ref, o_ref):
    h = _rms_mod(x_ref[...], g_ref[...], sh_ref[0], sc_ref[0])
    o_ref[...] = _dot(h.astype(BF16), w_ref[...])


def _conformer_conv(u, cw_ref, cb_ref, lg_ref, lb_ref, o_ref, stg_ref, xt_ref, ot_ref):
    c = u.shape[1]
    slabs = range(c // LANES)
    half = CONV_TAPS // 2
    pitch = GRID_W + SUBLANES
    group = SUBLANES
    for l in slabs:
        for q in range(SUBLANES):
            stg_ref[l, q * pitch:q * pitch + GRID_W, :] = u[q * GRID_W:(q + 1) * GRID_W, l * LANES:(l + 1) * LANES]

    def conv_slab(l, carry):
        for t in range(GRID_W):
            xt_ref[l, t * SUBLANES:(t + 1) * SUBLANES, :] = stg_ref[l, pl.ds(t, SUBLANES, stride=pitch), :]
        for t0 in range(0, GRID_W, group):
            accs = [None] * group
            for k in range(CONV_TAPS):
                srcs = [t0 + j + k - half for j in range(group)]
                if not any(0 <= sidx < GRID_W for sidx in srcs):
                    continue
                wk = cw_ref[l, k * SUBLANES:(k + 1) * SUBLANES, :]
                for j, sidx in enumerate(srcs):
                    if 0 <= sidx < GRID_W:
                        term = wk * xt_ref[l, sidx * SUBLANES:(sidx + 1) * SUBLANES, :]
                        accs[j] = term if accs[j] is None else accs[j] + term
            for j in range(group):
                ot_ref[l, (t0 + j) * SUBLANES:(t0 + j + 1) * SUBLANES, :] = accs[j]
        return carry
    lax.fori_loop(0, c // LANES, conv_slab, 0)
    rows_per_pass = group * SUBLANES
    for r0 in range(0, GRID_W * SUBLANES, rows_per_pass):
        rows = slice(r0, r0 + rows_per_pass)
        acc = jnp.concatenate([ot_ref[l, rows, :] for l in slabs], axis=1) + cb_ref[...]
        mu = jnp.mean(acc, axis=-1, keepdims=True)
        cen = acc - mu
        var = jnp.mean(cen * cen, axis=-1, keepdims=True)
        y = _silu(cen * lax.rsqrt(var + EPS) * lg_ref[...] + lb_ref[...])
        for l in slabs:
            ot_ref[l, rows, :] = y[:, l * LANES:(l + 1) * LANES]
    for q in range(SUBLANES):
        o_ref[q * GRID_W:(q + 1) * GRID_W, :] = jnp.concatenate(
            [ot_ref[l, pl.ds(q, GRID_W, stride=SUBLANES), :] for l in slabs], axis=1).astype(o_ref.dtype)


def _inproj_kernel(x_ref, sh_ref, sc_ref, g_ref, w_ref, cw_ref, cb_ref, lg_ref, lb_ref,
                   cv_ref, zl_ref, gg_ref, stg_ref, xt_ref, ot_ref):
    h = _rms_mod(x_ref[...], g_ref[...], sh_ref[0], sc_ref[0])
    z = _dot(h.astype(BF16), w_ref[...])
    c = cv_ref.shape[1]
    zl_ref[...] = z[:, 2 * c:3 * c]
    gg_ref[...] = _gelu_tanh(z[:, 3 * c:])
    u = _times_sigmoid(z[:, :c], z[:, c:2 * c])
    _conformer_conv(u, cw_ref, cb_ref, lg_ref, lb_ref, cv_ref, stg_ref, xt_ref, ot_ref)


def _inproj_ctx(x2, shift, scale, g, w, tm):
    t, d = x2.shape
    n = w.shape[1]
    mod_spec = pl.BlockSpec((1, 1, d), lambda i: (0, 0, 0))
    return pl.pallas_call(
        _inproj_ctx_kernel,
        out_shape=jax.ShapeDtypeStruct((t, n), F32),
        grid=(t // tm,),
        in_specs=[pl.BlockSpec((tm, d), lambda i: (i, 0)), mod_spec, mod_spec,
                  pl.BlockSpec((1, d), lambda i: (0, 0)),
                  pl.BlockSpec((d, n), lambda i: (0, 0))],
        out_specs=pl.BlockSpec((tm, n), lambda i: (i, 0)),
        compiler_params=_cparams(("arbitrary",)),
        name="in_proj_ctx",
    )(x2, shift, scale, g, w)


def _inproj(x2, shift, scale, g, w, cw, cb, lg, lb, rows_per_mod):
    t, d = x2.shape
    n = w.shape[1]
    c = n // 4
    tm = SUBLANES * GRID_W
    blocks_per_mod = rows_per_mod // tm
    mod_spec = pl.BlockSpec((1, 1, d), lambda i: (i // blocks_per_mod, 0, 0))
    const = lambda a: pl.BlockSpec(a.shape, lambda i: (0,) * a.ndim)
    rows = pl.BlockSpec((tm, c), lambda i: (i, 0))
    return pl.pallas_call(
        _inproj_kernel,
        out_shape=[jax.ShapeDtypeStruct((t, c), BF16), jax.ShapeDtypeStruct((t, c), F32),
                   jax.ShapeDtypeStruct((t, c), F32)],
        grid=(t // tm,),
        in_specs=[pl.BlockSpec((tm, d), lambda i: (i, 0)), mod_spec, mod_spec, const(g), const(w),
                  const(cw), const(cb), const(lg), const(lb)],
        out_specs=[rows, rows, rows],
        scratch_shapes=[pltpu.VMEM((c // LANES, SUBLANES * (GRID_W + SUBLANES), LANES), F32),
                        pltpu.VMEM((c // LANES, tm, LANES), F32), pltpu.VMEM((c // LANES, tm, LANES), F32)],
        compiler_params=_cparams(("arbitrary",)),
        name="in_proj",
    )(x2, shift, scale, g, w, cw, cb, lg, lb)


def _lru_kernel(zl_ref, h0_ref, cw_ref, cb_ref, wg_ref, gb_ref, lam_ref, *rest, seq, final_only):
    if final_only:
        o_ref, stage_ref, zt_ref, af_ref, bf_ref, ab_ref, bb_ref, pf_ref, hf_ref, pb_ref, hb_ref = rest
        gg_ref = None
    else:
        gg_ref, o_ref, stage_ref, zt_ref, af_ref, bf_ref, ab_ref, bb_ref, pf_ref, hf_ref, pb_ref, hb_ref = rest
    cl = seq // SUBLANES
    pitch = cl + SUBLANES
    c = zl_ref.shape[2]
    a_refs = (af_ref, ab_ref)
    b_refs = (bf_ref, bb_ref)
    halo = 2

    def tile_rows(t):
        return pl.ds(pl.multiple_of(t * SUBLANES, SUBLANES), SUBLANES)

    for j in range(SUBLANES):
        stage_ref[j * pitch:j * pitch + cl, :] = zl_ref[0, j * cl:(j + 1) * cl, :]

    def to_chunk_layout(t, carry):
        zt_ref[tile_rows(t + halo), :] = stage_ref[pl.ds(t, SUBLANES, stride=pitch), :]
        return carry
    lax.fori_loop(0, cl, to_chunk_layout, 0, unroll=8)

    sub = lax.broadcasted_iota(jnp.int32, (SUBLANES, c), 0)
    for t_src, t_dst in ((cl - 2, -2), (cl - 1, -1)):
        v = pltpu.roll(zt_ref[(t_src + halo) * SUBLANES:(t_src + halo + 1) * SUBLANES, :], 1, 0)
        zt_ref[(t_dst + halo) * SUBLANES:(t_dst + halo + 1) * SUBLANES, :] = jnp.where(sub == 0, 0.0, v)
    v = pltpu.roll(zt_ref[halo * SUBLANES:(halo + 1) * SUBLANES, :], SUBLANES - 1, 0)
    zt_ref[(cl + halo) * SUBLANES:(cl + halo + 1) * SUBLANES, :] = jnp.where(sub == SUBLANES - 1, 0.0, v)

    lam = lam_ref[...]
    nlam = -lam
    softplus = jnp.maximum(nlam, 0.0) + jnp.log1p(jnp.exp(-jnp.abs(nlam)))
    half_decay = (-0.5 * LRU_C) * softplus
    wg = wg_ref[0]

    piece = cl
    for p0 in range(0, cl * SUBLANES, piece):
        ul = cb_ref[...] + jnp.zeros((piece, c), F32)
        for k in range(LRU_TAPS):
            off = p0 + k * SUBLANES
            ul = ul + cw_ref[k:k + 1, :] * zt_ref[off:off + piece, :]
        g = _dot(ul.astype(BF16), wg)
        for d in range(2):
            t_r = jnp.tanh(g[:, (2 * d) * c:(2 * d + 1) * c] + gb_ref[2 * d:2 * d + 1, :])
            i = 0.5 * jnp.tanh(g[:, (2 * d + 1) * c:(2 * d + 2) * c] + gb_ref[2 * d + 1:2 * d + 2, :]) + 0.5
            log_a = half_decay[d:d + 1, :] * t_r + half_decay[d:d + 1, :]
            a = jnp.exp(log_a)
            m = jnp.maximum(-jnp.tanh(log_a) * (a * a + 1.0), 1e-12)
            mult = m * lax.rsqrt(m)
            a_refs[d][p0:p0 + piece, :] = a
            b_refs[d][p0:p0 + piece, :] = mult * i * ul

    def two_steps(a_ref, b_ref, p_out, h_out, s0, s1, h, p):
        a0 = a_ref[s0, :]
        a1 = a_ref[s1, :]
        b0 = b_ref[s0, :]
        a10 = a1 * a0
        b10 = a1 * b0 + b_ref[s1, :]
        h_out[s0, :] = a0 * h + b0
        p_out[s0, :] = a0 * p
        h = a10 * h + b10
        p = a10 * p
        h_out[s1, :] = h
        p_out[s1, :] = p
        return h, p

    group = 8
    group_rows = group * SUBLANES

    def step(n, carry):
        hf, pf, hb, pb = carry
        base_f = pl.multiple_of(n * group_rows, group_rows)
        base_b = pl.multiple_of((cl // group - 1 - n) * group_rows, group_rows)
        for k in range(0, group, 2):
            tf0 = pl.ds(base_f + k * SUBLANES, SUBLANES)
            tf1 = pl.ds(base_f + (k + 1) * SUBLANES, SUBLANES)
            hf, pf = two_steps(af_ref, bf_ref, pf_ref, hf_ref, tf0, tf1, hf, pf)
            tb0 = pl.ds(base_b + (group - 1 - k) * SUBLANES, SUBLANES)
            tb1 = pl.ds(base_b + (group - 2 - k) * SUBLANES, SUBLANES)
            hb, pb = two_steps(ab_ref, bb_ref, pb_ref, hb_ref, tb0, tb1, hb, pb)
        return hf, pf, hb, pb

    zero = jnp.zeros((SUBLANES, c), F32)
    one = jnp.ones((SUBLANES, c), F32)
    lax.fori_loop(0, cl // group, step, (zero, one, zero, one))

    cf = [None] * SUBLANES
    cbk = [None] * SUBLANES
    s = h0_ref[0, 0]
    for j in range(SUBLANES):
        cf[j] = s
        last = (cl - 1) * SUBLANES + j
        s = hf_ref[last:last + 1, :] + pf_ref[last:last + 1, :] * s
    final_f = s
    s = h0_ref[1, 0]
    for j in reversed(range(SUBLANES)):
        cbk[j] = s
        s = hb_ref[j:j + 1, :] + pb_ref[j:j + 1, :] * s
    final_b = s

    if final_only:
        o_ref[0, 0] = final_f
        o_ref[1, 0] = final_b
        return
    carry_f = jnp.concatenate(cf, axis=0)
    carry_b = jnp.concatenate(cbk, axis=0)

    def to_natural(t, carry):
        rows = tile_rows(t)
        h = (hf_ref[rows, :] + pf_ref[rows, :] * carry_f) + (hb_ref[rows, :] + pb_ref[rows, :] * carry_b)
        stage_ref[pl.ds(t, SUBLANES, stride=pitch), :] = h
        return carry
    lax.fori_loop(0, cl, to_natural, 0, unroll=8)
    for j in range(SUBLANES):
        h = stage_ref[j * pitch:j * pitch + cl, :]
        o_ref[0, j * cl:(j + 1) * cl, :] = (h * gg_ref[0, j * cl:(j + 1) * cl, :]).astype(o_ref.dtype)


def _lru(zl, gg, h0, cw, cb, wg, gb, lam, final_only, name):
    b, seq, c = zl.shape
    cbk = LRU_LANES
    cl = seq // SUBLANES
    pitch = cl + SUBLANES
    seq_spec = pl.BlockSpec((1, seq, cbk), lambda bi, ci: (bi, 0, ci))
    st_spec = pl.BlockSpec((2, 1, 1, cbk), lambda bi, ci: (0, bi, 0, ci))
    chan = lambda rows: pl.BlockSpec((rows, cbk), lambda bi, ci: (0, ci))
    in_specs = [seq_spec, st_spec, chan(LRU_TAPS), chan(1),
                pl.BlockSpec((1, cbk, 4 * cbk), lambda bi, ci: (ci, 0, 0)), chan(4), chan(2)]
    args = [zl, h0, cw, cb, wg, gb, lam]
    if final_only:
        out_shape = jax.ShapeDtypeStruct((2, b, 1, c), F32)
        out_spec = st_spec
    else:
        in_specs.append(seq_spec)
        args.append(gg)
        out_shape = jax.ShapeDtypeStruct((b, seq, c), BF16)
        out_spec = seq_spec
    coef = pltpu.VMEM((seq, cbk), F32)
    return pl.pallas_call(
        functools.partial(_lru_kernel, seq=seq, final_only=final_only),
        out_shape=out_shape,
        grid=(b, c // cbk),
        in_specs=in_specs,
        out_specs=out_spec,
        scratch_shapes=[pltpu.VMEM((SUBLANES * pitch, cbk), F32),
                        pltpu.VMEM((seq + 2 * SUBLANES * SUBLANES, cbk), F32)] + [coef] * 8,
        compiler_params=_cparams(("arbitrary", "arbitrary")),
        name=name,
    )(*args)


def _outproj_kernel(cv_ref, lr_ref, x_ref, g1_ref, sh_ref, sc_ref, ng_ref, wo_ref, rh_ref, rc_ref, rb_ref,
                    x1_ref, h2_ref, lg_ref):
    c = cv_ref.shape[1]
    y = _dot(cv_ref[...], wo_ref[0:c, :]) + _dot(lr_ref[...], wo_ref[c:2 * c, :])
    x1 = x_ref[...] + g1_ref[0] * y
    x1_ref[...] = x1
    h2 = _rms_mod(x1, ng_ref[...], sh_ref[0], sc_ref[0])
    h_hi, h_lo = _split_bf16(h2)
    _store_row_tiles(h2_ref, h2)
    rows = rh_ref.shape[0]
    both = _dot_nt(rc_ref[...], h_hi)
    lg_ref[...] = both[:rows] + both[rows:] + _dot_nt(rh_ref[...], h_lo) + rb_ref[...]


def _outproj(conv_l, lru_l, x2, g1, sh2, sc2, ng, wo, r_hi, r_both, r_b, rows_per_mod, tm):
    t, d = x2.shape
    c = conv_l.shape[1]
    blocks_per_mod = rows_per_mod // tm
    mod_spec = pl.BlockSpec((1, 1, d), lambda i: (i // blocks_per_mod, 0, 0))
    full = lambda a: pl.BlockSpec(a.shape, lambda i: (0, 0))
    return pl.pallas_call(
        _outproj_kernel,
        out_shape=[jax.ShapeDtypeStruct((t, d), F32), jax.ShapeDtypeStruct((t * ROW_TILE, LANES), F32),
                   jax.ShapeDtypeStruct((ROUTE_ROWS, t), F32)],
        grid=(t // tm,),
        in_specs=[pl.BlockSpec((tm, c), lambda i: (i, 0)), pl.BlockSpec((tm, c), lambda i: (i, 0)),
                  pl.BlockSpec((tm, d), lambda i: (i, 0)), mod_spec, mod_spec, mod_spec,
                  full(ng), full(wo), full(r_hi), full(r_both), full(r_b)],
        out_specs=[pl.BlockSpec((tm, d), lambda i: (i, 0)), pl.BlockSpec((tm * ROW_TILE, LANES), lambda i: (i, 0)),
                   pl.BlockSpec((ROUTE_ROWS, tm), lambda i: (0, i))],
        compiler_params=_cparams(("arbitrary",)),
        name="out_proj_router",
    )(conv_l, lru_l, x2, g1, sh2, sc2, ng, wo, r_hi, r_both, r_b)


def _route_chunk(lg_ref, lanes):
    n = lanes.size
    e = EXPERTS_PER_GROUP
    lgrp = lg_ref[0:N_GROUPS, lanes]
    gidx = lax.broadcasted_iota(jnp.int32, (N_GROUPS, n), 0)
    m = jnp.max(lgrp, axis=0, keepdims=True)
    ex = jnp.exp(lgrp - m)
    pg = ex / jnp.sum(ex, axis=0, keepdims=True)
    p_grp = jnp.max(pg, axis=0, keepdims=True)
    grp = jnp.min(jnp.where(pg == p_grp, gidx, N_GROUPS), axis=0, keepdims=True)
    le = jnp.zeros((e, n), F32)
    for g in range(N_GROUPS):
        rows = lg_ref[EXPERT_ROW0 + g * e:EXPERT_ROW0 + (g + 1) * e, lanes]
        le = jnp.where(grp == g, rows, le)
    m = jnp.max(le, axis=0, keepdims=True)
    ex = jnp.exp(le - m)
    pe = ex / jnp.sum(ex, axis=0, keepdims=True)
    eidx = lax.broadcasted_iota(jnp.int32, (e, n), 0)
    p1 = jnp.max(pe, axis=0, keepdims=True)
    i1 = jnp.min(jnp.where(pe == p1, eidx, e), axis=0, keepdims=True)
    pe2 = jnp.where(eidx == i1, -1.0, pe)
    p2 = jnp.max(pe2, axis=0, keepdims=True)
    i2 = jnp.min(jnp.where(pe2 == p2, eidx, e), axis=0, keepdims=True)
    denom = p1 + p2
    base = grp * e
    return (base + i1, base + i2), (p_grp * p1 / denom, p_grp * p2 / denom)


def _route_kernel(lg_ref, gate_ref, eid_ref, cnt_ref):
    tl = lg_ref.shape[1]
    chunk = 2 * LANES
    ne = N_EXPERTS

    @pl.when(pl.program_id(0) == 0)
    def _():
        cnt_ref[...] = jnp.zeros_like(cnt_ref)

    ones = jnp.ones((chunk, LANES), BF16)
    eidx = lax.broadcasted_iota(jnp.int32, (ne, chunk), 0)
    total = jnp.zeros((ne, LANES), F32)
    for cix in range(tl // chunk):
        lanes = pl.ds(cix * chunk, chunk)
        eids, gates = _route_chunk(lg_ref, lanes)
        for k in range(TOP_K):
            total = total + _dot((eidx == eids[k]).astype(BF16), ones)
        gate_ref[:, lanes] = jnp.concatenate(gates, axis=0)
        eid_ref[:, lanes] = jnp.concatenate(eids, axis=0)
    cnt_ref[...] = cnt_ref[...] + total


def _route(logits_t):
    rows, t = logits_t.shape
    tl = ROUTE_LANES
    blk = pl.BlockSpec((TOP_K, tl), lambda i: (0, i))
    return pl.pallas_call(
        _route_kernel,
        out_shape=[jax.ShapeDtypeStruct((TOP_K, t), F32), jax.ShapeDtypeStruct((TOP_K, t), jnp.int32),
                   jax.ShapeDtypeStruct((N_EXPERTS, LANES), F32)],
        grid=(t // tl,),
        in_specs=[pl.BlockSpec((rows, tl), lambda i: (0, i))],
        out_specs=[blk, blk, pl.BlockSpec((N_EXPERTS, LANES), lambda i: (0, 0))],
        compiler_params=_cparams(("arbitrary",)),
        name="route",
    )(logits_t)


def _index_bit(j, lane, sub, tile):
    lane_bits = LANES.bit_length() - 1
    tile_bits = lane_bits + SUBLANES.bit_length() - 1
    if j < lane_bits:
        return (lane >> j) & 1
    if j < tile_bits:
        return (sub >> (j - lane_bits)) & 1
    return (tile >> (j - tile_bits)) & 1


def _bitonic_sort(tiles):
    n_t = len(tiles)
    log_n = (n_t * SUBLANES * LANES).bit_length() - 1
    lane_bits = LANES.bit_length() - 1
    tile_bits = lane_bits + SUBLANES.bit_length() - 1
    lane = lax.broadcasted_iota(jnp.int32, (SUBLANES, LANES), 1)
    sub = lax.broadcasted_iota(jnp.int32, (SUBLANES, LANES), 0)
    for k in range(1, log_n + 1):
        for j in range(k - 1, -1, -1):
            new = []
            for v in range(n_t):
                x = tiles[v]
                bj = _index_bit(j, lane, sub, v)
                if j < lane_bits:
                    d = 1 << j
                    p = jnp.where(bj == 1, pltpu.roll(x, d, 1), pltpu.roll(x, LANES - d, 1))
                elif j < tile_bits:
                    m = 1 << (j - lane_bits)
                    p = jnp.where(bj == 1, pltpu.roll(x, m, 0), pltpu.roll(x, SUBLANES - m, 0))
                else:
                    p = tiles[v ^ (1 << (j - tile_bits))]
                bk = _index_bit(k, lane, sub, v) if k < log_n else 0
                take_min = bj == bk
                lo, hi = jnp.minimum(x, p), jnp.maximum(x, p)
                if isinstance(take_min, bool):
                    new.append(lo if take_min else hi)
                else:
                    new.append(jnp.where(take_min, lo, hi))
            tiles = new
    return tiles


def _sort_kernel(eid_ref, src_ref):
    n_t = eid_ref.shape[0] // SUBLANES
    n_bits = (eid_ref.shape[0] * LANES).bit_length() - 1
    lane = lax.broadcasted_iota(jnp.int32, (SUBLANES, LANES), 1)
    sub = lax.broadcasted_iota(jnp.int32, (SUBLANES, LANES), 0)
    tiles = []
    for v in range(n_t):
        n = (v * SUBLANES + sub) * LANES + lane
        tiles.append((eid_ref[v * SUBLANES:(v + 1) * SUBLANES, :] << n_bits) | n)
    tiles = _bitonic_sort(tiles)
    for v in range(n_t):
        src_ref[v * SUBLANES:(v + 1) * SUBLANES, :] = (tiles[v] & ((1 << n_bits) - 1)) * ROW_TILE


def _sort_assignments(eid_rows):
    n = eid_rows.shape[0] * eid_rows.shape[1]
    assert n & (n - 1) == 0 and (N_EXPERTS * n) < 2 ** 31, "keys expert*n + assignment must fit an int32"
    return pl.pallas_call(
        _sort_kernel,
        out_shape=jax.ShapeDtypeStruct(eid_rows.shape, jnp.int32),
        name="dispatch_sort",
    )(eid_rows)


def _tables_kernel(cnt_ref, blk_e_ref, nxt_ref, base_ref, nvalid_ref, nact_ref, *, bm):
    n_blocks = blk_e_ref.shape[0]
    shift = bm.bit_length() - 1

    def per_expert(ex, carry):
        blk0, start = carry
        cnt = cnt_ref[ex]
        end_blk = blk0 + ((cnt + (bm - 1)) >> shift)

        def fill_blk(b, c):
            done = (b - blk0) << shift
            blk_e_ref[b] = ex
            nxt_ref[b] = end_blk
            base_ref[b] = start + done
            nvalid_ref[b] = jnp.minimum(cnt - done, bm)
            return c
        lax.fori_loop(blk0, end_blk, fill_blk, 0)
        return end_blk, start + cnt
    nact, _ = lax.fori_loop(0, N_EXPERTS, per_expert, (0, 0))
    nact_ref[0] = nact

    def fill_idle(b, c):
        blk_e_ref[b] = N_EXPERTS - 1
        nxt_ref[b] = n_blocks
        base_ref[b] = 0
        nvalid_ref[b] = 0
        return c
    lax.fori_loop(nact, n_blocks, fill_idle, 0)


def _tables(counts, n_blocks, bm):
    smem = pl.BlockSpec(memory_space=pltpu.SMEM)
    blocks = jax.ShapeDtypeStruct((n_blocks,), jnp.int32)
    return pl.pallas_call(
        functools.partial(_tables_kernel, bm=bm),
        out_shape=[blocks, blocks, blocks, blocks, jax.ShapeDtypeStruct((1,), jnp.int32)],
        in_specs=[smem],
        out_specs=[smem] * 5,
        name="dispatch_tables",
    )(counts)


def _moe_kernel(blk_e_ref, nxt_ref, base_ref, nvalid_ref, nact_ref, src_ref,
                w1_hbm, w3_hbm, w2_hbm, h_hbm, y_hbm,
                wf1, wf3, wf2, w1b, w3b, w2b, xbuf, ybuf, gsem, ssem, wsem, *, n_tok):
    i = pl.program_id(0)
    nact = nact_ref[0]
    n_blocks = blk_e_ref.shape[0]
    bm = xbuf.shape[1] // ROW_TILE
    ring = xbuf.shape[0]
    de = w1b.shape[1]
    chunk = 2 * LANES
    n_chunks = de // chunk
    rows_per_chunk = 2 * bm // n_chunks

    def tile(row):
        return pl.ds(pl.multiple_of(row * ROW_TILE, ROW_TILE), ROW_TILE)

    def tiles_at(offset):
        return pl.ds(pl.multiple_of(offset, ROW_TILE), ROW_TILE)

    def in_offset(out_offset):
        span = ROW_TILE * n_tok
        if span & (span - 1) == 0:
            return out_offset & (span - 1)
        return lax.rem(out_offset, span)

    def gather_start(blk, sl, rows):
        first = base_ref[blk]
        for r in rows:
            src = tiles_at(in_offset(src_ref[first + r]))
            pltpu.make_async_copy(h_hbm.at[src, :], xbuf.at[sl, tile(r), :], gsem.at[sl]).start()

    def scatter_start(blk, sl, rows, real, priorities):
        first = base_ref[blk]
        valid = jnp.where(real, nvalid_ref[blk], 0)
        spare0 = spare_row(jnp.where(real, lax.rem(blk, ring), ring - 1), 0) * ROW_TILE
        for r in rows:
            dst = tiles_at(jnp.where(r < valid, src_ref[first + r], spare0 + r * ROW_TILE))
            pltpu.make_async_copy(ybuf.at[sl, tile(r), :], y_hbm.at[dst, :],
                                  ssem.at[sl]).start(priority=r % priorities)

    def gather_wait(sl):
        pltpu.make_async_copy(h_hbm.at[pl.ds(0, bm * ROW_TILE), :], xbuf.at[sl], gsem.at[sl]).wait()

    def scatter_wait(sl):
        pltpu.make_async_copy(ybuf.at[sl], y_hbm.at[pl.ds(0, bm * ROW_TILE), :], ssem.at[sl]).wait()

    def spare_row(set_id, r):
        return TOP_K * n_tok + set_id * bm + r

    @pl.when(i < nact)
    def _():
        @pl.when(i == 0)
        def _():
            for blk in range(ring - 1):
                gather_start(min(blk, n_blocks - 1), blk, range(bm))
            ybuf[...] = jnp.zeros_like(ybuf)
            for sl in range(ring):
                spare = pltpu.make_async_copy(
                    ybuf.at[sl], y_hbm.at[pl.ds(spare_row(sl, 0) * ROW_TILE, bm * ROW_TILE), :], ssem.at[sl])
                spare.start()
                spare.wait()

        def weight_copies(e):
            return [pltpu.make_async_copy(w_hbm.at[e], stage, wsem.at[k])
                    for k, (w_hbm, stage) in enumerate(((w1_hbm, wf1), (w3_hbm, wf3), (w2_hbm, wf2)))]

        this_e = blk_e_ref[i]

        @pl.when(i == 0)
        def _():
            for cp in weight_copies(this_e):
                cp.start()

        @pl.when((i == 0) | (this_e != blk_e_ref[jnp.maximum(i - 1, 0)]))
        def _():
            for cp in weight_copies(this_e):
                cp.wait()
            for stage, wb in ((wf1, w1b), (wf3, w3b), (wf2, w2b)):
                def cast_rows(rb, carry, stage=stage, wb=wb):
                    rows = pl.ds(pl.multiple_of(rb * LANES, LANES), LANES)
                    wb[rows, :] = stage[rows, :].astype(BF16)
                    return carry
                lax.fori_loop(0, stage.shape[0] // LANES, cast_rows, 0)
            nxt_blk = nxt_ref[i]

            @pl.when(nxt_blk < nact)
            def _():
                for cp in weight_copies(blk_e_ref[nxt_blk]):
                    cp.start(priority=1)

        ahead = jnp.minimum(i + ring - 1, n_blocks - 1)
        prv = jnp.maximum(i - 1, 0)

        def compute_block(slot):
            far_slot = (slot - 1) % ring
            others = [(slot - 1 - k) % ring for k in range(ring - 1)]
            gather_wait(slot)
            xb = _load_row_tiles(xbuf.at[slot], bm).astype(BF16)
            acts = []
            for c in range(n_chunks):
                cols = slice(c * chunk, (c + 1) * chunk)
                a = _dot(xb, w1b[:, cols])
                g = _dot(xb, w3b[:, cols])
                acts.append((_silu(a) * g).astype(BF16))
                if c < n_chunks // 2:
                    gather_start(ahead, far_slot, range(c * rows_per_chunk, (c + 1) * rows_per_chunk))
                else:
                    first = (c - n_chunks // 2) * rows_per_chunk
                    scatter_start(prv, far_slot, range(first, first + rows_per_chunk), i > 0, 2)

            y = _dot(jnp.concatenate(acts, axis=1), w2b[...])

            @pl.when(i >= ring - 1)
            def _():
                scatter_wait(slot)

            _store_row_tiles(ybuf.at[slot], y)

            @pl.when(i == nact - 1)
            def _():
                scatter_start(i, slot, range(bm), True, 1)
                for k, sl in enumerate(others):
                    pl.when(i >= k)(functools.partial(scatter_wait, sl))
                    gather_wait(sl)
                scatter_wait(slot)

        for parity in range(ring):
            pl.when(lax.rem(i, ring) == parity)(functools.partial(compute_block, parity))


def _moe(blk_e, nxt_blk, base, nvalid, nact, src, w1, w3, w2, h2_tiles, n_blocks):
    t = h2_tiles.shape[0] // ROW_TILE
    d = w1.shape[1]
    de = w1.shape[2]
    bm = MOE_BLOCK
    hbm = pl.BlockSpec(memory_space=pl.ANY)
    grid_spec = pltpu.PrefetchScalarGridSpec(
        num_scalar_prefetch=6,
        grid=(n_blocks,),
        in_specs=[hbm, hbm, hbm, hbm],
        out_specs=hbm,
        scratch_shapes=[pltpu.VMEM((d, de), F32), pltpu.VMEM((d, de), F32), pltpu.VMEM((de, d), F32),
                        pltpu.VMEM((d, de), BF16), pltpu.VMEM((d, de), BF16), pltpu.VMEM((de, d), BF16),
                        pltpu.VMEM((MOE_RING, bm * ROW_TILE, LANES), F32),
                        pltpu.VMEM((MOE_RING, bm * ROW_TILE, LANES), F32),
                        pltpu.SemaphoreType.DMA((MOE_RING,)), pltpu.SemaphoreType.DMA((MOE_RING,)),
                        pltpu.SemaphoreType.DMA((3,))],
    )
    return pl.pallas_call(
        functools.partial(_moe_kernel, n_tok=t),
        out_shape=jax.ShapeDtypeStruct(((TOP_K * t + MOE_RING * bm) * ROW_TILE, LANES), F32),
        grid_spec=grid_spec,
        compiler_params=_cparams(("arbitrary",)),
        name="moe_experts",
    )(blk_e, nxt_blk, base, nvalid, nact, src, w1, w3, w2, h2_tiles)


def _final_kernel(x1_ref, y0_ref, y1_ref, gt_ref, g2_ref, fg_ref, o_ref):
    gt = gt_ref[...]
    tm = x1_ref.shape[0]
    y2 = gt[:, 0:1] * _load_row_tiles(y0_ref, tm) + gt[:, 1:2] * _load_row_tiles(y1_ref, tm)
    x = x1_ref[...] + g2_ref[0] * y2
    o_ref[...] = x * lax.rsqrt(jnp.mean(x * x, axis=-1, keepdims=True) + EPS) * fg_ref[...]


def _final(x1, y, gates_tok, g2, fg, rows_per_mod, tm):
    t, d = x1.shape
    nb = t // tm
    blocks_per_mod = rows_per_mod // tm
    return pl.pallas_call(
        _final_kernel,
        out_shape=jax.ShapeDtypeStruct((t, d), F32),
        grid=(nb,),
        in_specs=[pl.BlockSpec((tm, d), lambda i: (i, 0)),
                  pl.BlockSpec((tm * ROW_TILE, LANES), lambda i: (i, 0)),
                  pl.BlockSpec((tm * ROW_TILE, LANES), lambda i: (i + nb, 0)),
                  pl.BlockSpec((tm, TOP_K), lambda i: (i, 0)),
                  pl.BlockSpec((1, 1, d), lambda i: (i // blocks_per_mod, 0, 0)),
                  pl.BlockSpec((1, d), lambda i: (0, 0))],
        out_specs=pl.BlockSpec((tm, d), lambda i: (i, 0)),
        compiler_params=_cparams(("arbitrary",)),
        name="combine_final_norm",
    )(x1, y, y, gates_tok, g2, fg)


def kernel(x, c, ctx, c_ctx, w_ada, b_ada, norm1_g, norm2_g, w_in, conv_dw, conv_b, conv_ln_g, conv_ln_b,
           lru_conv_w, lru_conv_b, lru_wa, lru_ba, lru_wx, lru_bx, lru_lam, w_out,
           router_wg, router_bg, router_we, router_be, w1, w3, w2, final_g):
    assert w_ada.shape[0] == 1, "single-layer block"
    assert x.shape[2] == ROW_TILE * LANES, "row-tile layout: one (ROW_TILE, LANES) tile per token row"
    b, s, d = x.shape
    n_ctx = ctx.shape[1]
    t = b * s
    cc = conv_dw.shape[2]
    lw = lru_conv_w.shape[2]

    c_rows = jnp.zeros((SUBLANES, d), F32).at[:b].set(c).at[b].set(c_ctx)
    mod = _ada(c_rows, w_ada[0], b_ada)
    mod_l = mod[:b].reshape(b, 6, 1, d)
    sh1, sc1, g1, sh2, sc2, g2 = (mod_l[:, k] for k in range(6))
    mod_c = mod[b].reshape(6, 1, 1, d)
    csh1, csc1 = mod_c[0], mod_c[1]

    w_in_b = w_in[0].astype(BF16)
    w_out_b = w_out[0].astype(BF16)
    heads_per_blk = LRU_LANES // lru_wa.shape[3]
    n_cblk = lw // LRU_LANES

    def blockdiag(wh):
        hd = wh.shape[1]
        wh = wh.reshape(n_cblk, heads_per_blk, hd, hd)
        eye = jnp.eye(heads_per_blk, dtype=wh.dtype)
        return jnp.einsum("chij,hg->chigj", wh, eye).reshape(n_cblk, LRU_LANES, LRU_LANES)

    wg = (0.5 * jnp.concatenate([blockdiag(lru_wa[0, 0]), blockdiag(lru_wx[0, 0]),
                                 blockdiag(lru_wa[0, 1]), blockdiag(lru_wx[0, 1])], axis=2)).astype(BF16)
    gb = 0.5 * jnp.stack([lru_ba[0, 0], lru_bx[0, 0], lru_ba[0, 1], lru_bx[0, 1]])
    lam = lru_lam[0]

    zc = _inproj_ctx(ctx.reshape(b * n_ctx, d), csh1, csc1, norm1_g, w_in_b[:, 2 * cc:2 * cc + lw], n_ctx)
    h0 = _lru(zc.reshape(b, n_ctx, lw), None, jnp.zeros((2, b, 1, lw), F32),
              lru_conv_w[0], lru_conv_b, wg, gb, lam, True, "rglru_ctx")

    x2 = x.reshape(t, d)
    conv_w_rep = jnp.repeat(conv_dw[0], SUBLANES, axis=0).reshape(CONV_TAPS * SUBLANES, cc // LANES, LANES)
    conv_w_rep = conv_w_rep.transpose(1, 0, 2)
    conv_l, zl, gg = _inproj(x2, sh1, sc1, norm1_g, w_in_b, conv_w_rep, conv_b, conv_ln_g, conv_ln_b, s)
    lru_l = _lru(zl.reshape(b, s, lw), gg.reshape(b, s, lw), h0,
                 lru_conv_w[0], lru_conv_b, wg, gb, lam, False, "rglru")

    wr = jnp.zeros((ROUTE_ROWS, d), F32)
    wr = wr.at[:N_GROUPS].set(router_wg[0].T)
    wr = wr.at[EXPERT_ROW0:EXPERT_ROW0 + N_EXPERTS].set(router_we[0].reshape(d, N_EXPERTS).T)
    rb = jnp.zeros((ROUTE_ROWS, 1), F32)
    rb = rb.at[:N_GROUPS, 0].set(router_bg[0])
    rb = rb.at[EXPERT_ROW0:EXPERT_ROW0 + N_EXPERTS, 0].set(router_be[0].reshape(-1))
    r_hi, r_lo = _split_bf16(wr)
    x1, h2, logits_t = _outproj(conv_l, lru_l.reshape(t, lw), x2, g1, sh2, sc2, norm2_g, w_out_b,
                                r_hi, jnp.concatenate([r_hi, r_lo], axis=0), rb, s, TOKEN_BLOCK)

    gates, eid, counts = _route(logits_t)
    n_blocks = (TOP_K * t) // MOE_BLOCK + N_EXPERTS
    src = jnp.pad(_sort_assignments(eid.reshape(TOP_K * t // LANES, LANES)).reshape(-1), (0, MOE_BLOCK))
    blk_e, nxt_blk, base, nvalid, nact = _tables(counts[:, 0].astype(jnp.int32), n_blocks, MOE_BLOCK)
    y = _moe(blk_e, nxt_blk, base, nvalid, nact, src, w1[0], w3[0], w2[0], h2, n_blocks)
    out = _final(x1, y, gates.T, g2, final_g.reshape(1, d), s, 2 * TOKEN_BLOCK)
    return out.reshape(b, s, d)
```

```python
import functools

import jax
import jax.numpy as jnp
from jax import lax
from jax.experimental import pallas as pl
from jax.experimental.pallas import tpu as pltpu

F32 = jnp.float32
BF16 = jnp.bfloat16

EPS = 1e-6
CONV_TAPS = 31
LRU_TAPS = 4
LRU_C = 8.0
GRID_W = 64
N_GROUPS = 4
EXPERTS_PER_GROUP = 8
N_EXPERTS = N_GROUPS * EXPERTS_PER_GROUP
TOP_K = 2

SUBLANES = 8
LANES = 128
TOKEN_BLOCK = 1024
LRU_LANES = 128
EXPERT_ROW0 = 8
ROUTE_ROWS = EXPERT_ROW0 + N_EXPERTS
MOE_BLOCK = 256
MOE_RING = 3
ROUTE_LANES = 2048
ADA_COLS = 1536
V7X_VMEM_BYTES = 64 * 1024 * 1024
VMEM_LIMIT = 3 * V7X_VMEM_BYTES // 4


def _cparams(sem, vmem=VMEM_LIMIT):
    return pltpu.CompilerParams(dimension_semantics=sem, vmem_limit_bytes=vmem)


def _split_bf16(a):
    hi = a.astype(BF16)
    lo = (a - hi.astype(F32)).astype(BF16)
    return hi, lo


def _dot(a, b):
    return jnp.dot(a, b, preferred_element_type=F32)


def _dot_nt(a, b):
    return lax.dot_general(a, b, (((1,), (1,)), ((), ())), preferred_element_type=F32)


def _times_sigmoid(v, x):
    hv = 0.5 * v
    return hv * jnp.tanh(0.5 * x) + hv


def _silu(x):
    h = 0.5 * x
    return h * jnp.tanh(h) + h


def _gelu_tanh(x):
    c = 0.7978845608028654
    h = 0.5 * x
    return h * jnp.tanh(x * ((c * 0.044715) * (x * x) + c)) + h


ROW_TILE = SUBLANES


def _store_row_tiles(ref, x):
    rows = x.shape[0]
    for s in range(ROW_TILE):
        ref[pl.ds(s, rows, stride=ROW_TILE), :] = x[:, s * LANES:(s + 1) * LANES]


def _load_row_tiles(ref, rows):
    return jnp.concatenate([ref[pl.ds(s, rows, stride=ROW_TILE), :] for s in range(ROW_TILE)], axis=1)


def _rms_mod(x, g, shift, scale):
    y = x * lax.rsqrt(jnp.mean(x * x, axis=-1, keepdims=True) + EPS)
    return y * (g * (1.0 + scale)) + shift


def _ada_kernel(c_ref, w_ref, b_ref, o_ref):
    a = _silu(c_ref[...])
    a_hi, a_lo = _split_bf16(a)
    w_hi, w_lo = _split_bf16(w_ref[...])
    o_ref[...] = _dot(a_hi, w_hi) + _dot(a_lo, w_hi) + _dot(a_hi, w_lo) + b_ref[...]


def _ada(c_rows, w, b):
    m, d = c_rows.shape
    n = w.shape[1]
    bn = ADA_COLS
    return pl.pallas_call(
        _ada_kernel,
        out_shape=jax.ShapeDtypeStruct((m, n), F32),
        grid=(n // bn,),
        in_specs=[pl.BlockSpec((m, d), lambda j: (0, 0)),
                  pl.BlockSpec((d, bn), lambda j: (0, j)),
                  pl.BlockSpec((1, bn), lambda j: (0, j))],
        out_specs=pl.BlockSpec((m, bn), lambda j: (0, j)),
        compiler_params=_cparams(("arbitrary",)),
        name="ada_mod",
    )(c_rows, w, b)


def _inproj_ctx_kernel(x_ref, sh_ref, sc_ref, g_ref, w_ref, o_ref):
    h = _rms_mod(x_ref[...], g_ref[...], sh_ref[0], sc_ref[0])
    o_ref[...] = _dot(h.astype(BF16), w_ref[...])


def _conformer_conv(u, cw_ref, cb_ref, lg_ref, lb_ref, o_ref, stg_ref, xt_ref, ot_ref):
    c = u.shape[1]
    slabs = range(c // LANES)
    half = CONV_TAPS // 2
    pitch = GRID_W + SUBLANES
    group = SUBLANES
    for l in slabs:
        for q in range(SUBLANES):
            stg_ref[l, q * pitch:q * pitch + GRID_W, :] = u[q * GRID_W:(q + 1) * GRID_W, l * LANES:(l + 1) * LANES]

    def conv_slab(l, carry):
        for t in range(GRID_W):
            xt_ref[l, t * SUBLANES:(t + 1) * SUBLANES, :] = stg_ref[l, pl.ds(t, SUBLANES, stride=pitch), :]
        for t0 in range(0, GRID_W, group):
            accs = [None] * group
            for k in range(CONV_TAPS):
                srcs = [t0 + j + k - half for j in range(group)]
                if not any(0 <= sidx < GRID_W for sidx in srcs):
                    continue
                wk = cw_ref[l, k * SUBLANES:(k + 1) * SUBLANES, :]
                for j, sidx in enumerate(srcs):
                    if 0 <= sidx < GRID_W:
                        term = wk * xt_ref[l, sidx * SUBLANES:(sidx + 1) * SUBLANES, :]
                        accs[j] = term if accs[j] is None else accs[j] + term
            for j in range(group):
                ot_ref[l, (t0 + j) * SUBLANES:(t0 + j + 1) * SUBLANES, :] = accs[j]
        return carry
    lax.fori_loop(0, c // LANES, conv_slab, 0)
    rows_per_pass = group * SUBLANES
    for r0 in range(0, GRID_W * SUBLANES, rows_per_pass):
        rows = slice(r0, r0 + rows_per_pass)
        acc = jnp.concatenate([ot_ref[l, rows, :] for l in slabs], axis=1) + cb_ref[...]
        mu = jnp.mean(acc, axis=-1, keepdims=True)
        cen = acc - mu
        var = jnp.mean(cen * cen, axis=-1, keepdims=True)
        y = _silu(cen * lax.rsqrt(var + EPS) * lg_ref[...] + lb_ref[...])
        for l in slabs:
            ot_ref[l, rows, :] = y[:, l * LANES:(l + 1) * LANES]
    for q in range(SUBLANES):
        o_ref[q * GRID_W:(q + 1) * GRID_W, :] = jnp.concatenate(
            [ot_ref[l, pl.ds(q, GRID_W, stride=SUBLANES), :] for l in slabs], axis=1).astype(o_ref.dtype)


def _inproj_kernel(x_ref, sh_ref, sc_ref, g_ref, w_ref, cw_ref, cb_ref, lg_ref, lb_ref,
                   cv_ref, zl_ref, gg_ref, stg_ref, xt_ref, ot_ref):
    h = _rms_mod(x_ref[...], g_ref[...], sh_ref[0], sc_ref[0])
    z = _dot(h.astype(BF16), w_ref[...])
    c = cv_ref.shape[1]
    zl_ref[...] = z[:, 2 * c:3 * c]
    gg_ref[...] = _gelu_tanh(z[:, 3 * c:])
    u = _times_sigmoid(z[:, :c], z[:, c:2 * c])
    _conformer_conv(u, cw_ref, cb_ref, lg_ref, lb_ref, cv_ref, stg_ref, xt_ref, ot_ref)


def _inproj_ctx(x2, shift, scale, g, w, tm):
    t, d = x2.shape
    n = w.shape[1]
    mod_spec = pl.BlockSpec((1, 1, d), lambda i: (0, 0, 0))
    return pl.pallas_call(
        _inproj_ctx_kernel,
        out_shape=jax.ShapeDtypeStruct((t, n), F32),
        grid=(t // tm,),
        in_specs=[pl.BlockSpec((tm, d), lambda i: (i, 0)), mod_spec, mod_spec,
                  pl.BlockSpec((1, d), lambda i: (0, 0)),
                  pl.BlockSpec((d, n), lambda i: (0, 0))],
        out_specs=pl.BlockSpec((tm, n), lambda i: (i, 0)),
        compiler_params=_cparams(("arbitrary",)),
        name="in_proj_ctx",
    )(x2, shift, scale, g, w)


def _inproj(x2, shift, scale, g, w, cw, cb, lg, lb, rows_per_mod):
    t, d = x2.shape
    n = w.shape[1]
    c = n // 4
    tm = SUBLANES * GRID_W
    blocks_per_mod = rows_per_mod // tm
    mod_spec = pl.BlockSpec((1, 1, d), lambda i: (i // blocks_per_mod, 0, 0))
    const = lambda a: pl.BlockSpec(a.shape, lambda i: (0,) * a.ndim)
    rows = pl.BlockSpec((tm, c), lambda i: (i, 0))
    return pl.pallas_call(
        _inproj_kernel,
        out_shape=[jax.ShapeDtypeStruct((t, c), BF16), jax.ShapeDtypeStruct((t, c), F32),
                   jax.ShapeDtypeStruct((t, c), F32)],
        grid=(t // tm,),
        in_specs=[pl.BlockSpec((tm, d), lambda i: (i, 0)), mod_spec, mod_spec, const(g), const(w),
                  const(cw), const(cb), const(lg), const(lb)],
        out_specs=[rows, rows, rows],
        scratch_shapes=[pltpu.VMEM((c // LANES, SUBLANES * (GRID_W + SUBLANES), LANES), F32),
                        pltpu.VMEM((c // LANES, tm, LANES), F32), pltpu.VMEM((c // LANES, tm, LANES), F32)],
        compiler_params=_cparams(("arbitrary",)),
        name="in_proj",
    )(x2, shift, scale, g, w, cw, cb, lg, lb)


def _lru_kernel(zl_ref, h0_ref, cw_ref, cb_ref, wg_ref, gb_ref, lam_ref, *rest, seq, final_only):
    if final_only:
        o_ref, stage_ref, zt_ref, af_ref, bf_ref, ab_ref, bb_ref, pf_ref, hf_ref, pb_ref, hb_ref = rest
        gg_ref = None
    else:
        gg_ref, o_ref, stage_ref, zt_ref, af_ref, bf_ref, ab_ref, bb_ref, pf_ref, hf_ref, pb_ref, hb_ref = rest
    cl = seq // SUBLANES
    pitch = cl + SUBLANES
    c = zl_ref.shape[2]
    a_refs = (af_ref, ab_ref)
    b_refs = (bf_ref, bb_ref)
    halo = 2

    def tile_rows(t):
        return pl.ds(pl.multiple_of(t * SUBLANES, SUBLANES), SUBLANES)

    for j in range(SUBLANES):
        stage_ref[j * pitch:j * pitch + cl, :] = zl_ref[0, j * cl:(j + 1) * cl, :]

    def to_chunk_layout(t, carry):
        zt_ref[tile_rows(t + halo), :] = stage_ref[pl.ds(t, SUBLANES, stride=pitch), :]
        return carry
    lax.fori_loop(0, cl, to_chunk_layout, 0, unroll=8)

    sub = lax.broadcasted_iota(jnp.int32, (SUBLANES, c), 0)
    for t_src, t_dst in ((cl - 2, -2), (cl - 1, -1)):
        v = pltpu.roll(zt_ref[(t_src + halo) * SUBLANES:(t_src + halo + 1) * SUBLANES, :], 1, 0)
        zt_ref[(t_dst + halo) * SUBLANES:(t_dst + halo + 1) * SUBLANES, :] = jnp.where(sub == 0, 0.0, v)
    v = pltpu.roll(zt_ref[halo * SUBLANES:(halo + 1) * SUBLANES, :], SUBLANES - 1, 0)
    zt_ref[(cl + halo) * SUBLANES:(cl + halo + 1) * SUBLANES, :] = jnp.where(sub == SUBLANES - 1, 0.0, v)

    lam = lam_ref[...]
    nlam = -lam
    softplus = jnp.maximum(nlam, 0.0) + jnp.log1p(jnp.exp(-jnp.abs(nlam)))
    half_decay = (-0.5 * LRU_C) * softplus
    wg = wg_ref[0]

    piece = cl
    for p0 in range(0, cl * SUBLANES, piece):
        ul = cb_ref[...] + jnp.zeros((piece, c), F32)
        for k in range(LRU_TAPS):
            off = p0 + k * SUBLANES
            ul = ul + cw_ref[k:k + 1, :] * zt_ref[off:off + piece, :]
        g = _dot(ul.astype(BF16), wg)
        for d in range(2):
            t_r = jnp.tanh(g[:, (2 * d) * c:(2 * d + 1) * c] + gb_ref[2 * d:2 * d + 1, :])
            i = 0.5 * jnp.tanh(g[:, (2 * d + 1) * c:(2 * d + 2) * c] + gb_ref[2 * d + 1:2 * d + 2, :]) + 0.5
            log_a = half_decay[d:d + 1, :] * t_r + half_decay[d:d + 1, :]
            a = jnp.exp(log_a)
            m = jnp.maximum(-jnp.tanh(log_a) * (a * a + 1.0), 1e-12)
            mult = m * lax.rsqrt(m)
            a_refs[d][p0:p0 + piece, :] = a
            b_refs[d][p0:p0 + piece, :] = mult * i * ul

    def two_steps(a_ref, b_ref, p_out, h_out, s0, s1, h, p):
        a0 = a_ref[s0, :]
        a1 = a_ref[s1, :]
        b0 = b_ref[s0, :]
        a10 = a1 * a0
        b10 = a1 * b0 + b_ref[s1, :]
        h_out[s0, :] = a0 * h + b0
        p_out[s0, :] = a0 * p
        h = a10 * h + b10
        p = a10 * p
        h_out[s1, :] = h
        p_out[s1, :] = p
        return h, p

    group = 8
    group_rows = group * SUBLANES

    def step(n, carry):
        hf, pf, hb, pb = carry
        base_f = pl.multiple_of(n * group_rows, group_rows)
        base_b = pl.multiple_of((cl // group - 1 - n) * group_rows, group_rows)
        for k in range(0, group, 2):
            tf0 = pl.ds(base_f + k * SUBLANES, SUBLANES)
            tf1 = pl.ds(base_f + (k + 1) * SUBLANES, SUBLANES)
            hf, pf = two_steps(af_ref, bf_ref, pf_ref, hf_ref, tf0, tf1, hf, pf)
            tb0 = pl.ds(base_b + (group - 1 - k) * SUBLANES, SUBLANES)
            tb1 = pl.ds(base_b + (group - 2 - k) * SUBLANES, SUBLANES)
            hb, pb = two_steps(ab_ref, bb_ref, pb_ref, hb_ref, tb0, tb1, hb, pb)
        return hf, pf, hb, pb

    zero = jnp.zeros((SUBLANES, c), F32)
    one = jnp.ones((SUBLANES, c), F32)
    lax.fori_loop(0, cl // group, step, (zero, one, zero, one))

    cf = [None] * SUBLANES
    cbk = [None] * SUBLANES
    s = h0_ref[0, 0]
    for j in range(SUBLANES):
        cf[j] = s
        last = (cl - 1) * SUBLANES + j
        s = hf_ref[last:last + 1, :] + pf_ref[last:last + 1, :] * s
    final_f = s
    s = h0_ref[1, 0]
    for j in reversed(range(SUBLANES)):
        cbk[j] = s
        s = hb_ref[j:j + 1, :] + pb_ref[j:j + 1, :] * s
    final_b = s

    if final_only:
        o_ref[0, 0] = final_f
        o_ref[1, 0] = final_b
        return
    carry_f = jnp.concatenate(cf, axis=0)
    carry_b = jnp.concatenate(cbk, axis=0)

    def to_natural(t, carry):
        rows = tile_rows(t)
        h = (hf_ref[rows, :] + pf_ref[rows, :] * carry_f) + (hb_ref[rows, :] + pb_ref[rows, :] * carry_b)
        stage_ref[pl.ds(t, SUBLANES, stride=pitch), :] = h
        return carry
    lax.fori_loop(0, cl, to_natural, 0, unroll=8)
    for j in range(SUBLANES):
        h = stage_ref[j * pitch:j * pitch + cl, :]
        o_ref[0, j * cl:(j + 1) * cl, :] = (h * gg_ref[0, j * cl:(j + 1) * cl, :]).astype(o_ref.dtype)


def _lru(zl, gg, h0, cw, cb, wg, gb, lam, final_only, name):
    b, seq, c = zl.shape
    cbk = LRU_LANES
    cl = seq // SUBLANES
    pitch = cl + SUBLANES
    seq_spec = pl.BlockSpec((1, seq, cbk), lambda bi, ci: (bi, 0, ci))
    st_spec = pl.BlockSpec((2, 1, 1, cbk), lambda bi, ci: (0, bi, 0, ci))
    chan = lambda rows: pl.BlockSpec((rows, cbk), lambda bi, ci: (0, ci))
    in_specs = [seq_spec, st_spec, chan(LRU_TAPS), chan(1),
                pl.BlockSpec((1, cbk, 4 * cbk), lambda bi, ci: (ci, 0, 0)), chan(4), chan(2)]
    args = [zl, h0, cw, cb, wg, gb, lam]
    if final_only:
        out_shape = jax.ShapeDtypeStruct((2, b, 1, c), F32)
        out_spec = st_spec
    else:
        in_specs.append(seq_spec)
        args.append(gg)
        out_shape = jax.ShapeDtypeStruct((b, seq, c), BF16)
        out_spec = seq_spec
    coef = pltpu.VMEM((seq, cbk), F32)
    return pl.pallas_call(
        functools.partial(_lru_kernel, seq=seq, final_only=final_only),
        out_shape=out_shape,
        grid=(b, c // cbk),
        in_specs=in_specs,
        out_specs=out_spec,
        scratch_shapes=[pltpu.VMEM((SUBLANES * pitch, cbk), F32),
                        pltpu.VMEM((seq + 2 * SUBLANES * SUBLANES, cbk), F32)] + [coef] * 8,
        compiler_params=_cparams(("arbitrary", "arbitrary")),
        name=name,
    )(*args)


def _outproj_kernel(cv_ref, lr_ref, x_ref, g1_ref, sh_ref, sc_ref, ng_ref, wo_ref, rh_ref, rc_ref, rb_ref,
                    x1_ref, h2_ref, lg_ref):
    c = cv_ref.shape[1]
    y = _dot(cv_ref[...], wo_ref[0:c, :]) + _dot(lr_ref[...], wo_ref[c:2 * c, :])
    x1 = x_ref[...] + g1_ref[0] * y
    x1_ref[...] = x1
    h2 = _rms_mod(x1, ng_ref[...], sh_ref[0], sc_ref[0])
    h_hi, h_lo = _split_bf16(h2)
    _store_row_tiles(h2_ref, h2)
    rows = rh_ref.shape[0]
    both = _dot_nt(rc_ref[...], h_hi)
    lg_ref[...] = both[:rows] + both[rows:] + _dot_nt(rh_ref[...], h_lo) + rb_ref[...]


def _outproj(conv_l, lru_l, x2, g1, sh2, sc2, ng, wo, r_hi, r_both, r_b, rows_per_mod, tm):
    t, d = x2.shape
    c = conv_l.shape[1]
    blocks_per_mod = rows_per_mod // tm
    mod_spec = pl.BlockSpec((1, 1, d), lambda i: (i // blocks_per_mod, 0, 0))
    full = lambda a: pl.BlockSpec(a.shape, lambda i: (0, 0))
    return pl.pallas_call(
        _outproj_kernel,
        out_shape=[jax.ShapeDtypeStruct((t, d), F32), jax.ShapeDtypeStruct((t * ROW_TILE, LANES), F32),
                   jax.ShapeDtypeStruct((ROUTE_ROWS, t), F32)],
        grid=(t // tm,),
        in_specs=[pl.BlockSpec((tm, c), lambda i: (i, 0)), pl.BlockSpec((tm, c), lambda i: (i, 0)),
                  pl.BlockSpec((tm, d), lambda i: (i, 0)), mod_spec, mod_spec, mod_spec,
                  full(ng), full(wo), full(r_hi), full(r_both), full(r_b)],
        out_specs=[pl.BlockSpec((tm, d), lambda i: (i, 0)), pl.BlockSpec((tm * ROW_TILE, LANES), lambda i: (i, 0)),
                   pl.BlockSpec((ROUTE_ROWS, tm), lambda i: (0, i))],
        compiler_params=_cparams(("arbitrary",)),
        name="out_proj_router",
    )(conv_l, lru_l, x2, g1, sh2, sc2, ng, wo, r_hi, r_both, r_b)


def _route_chunk(lg_ref, lanes):
    n = lanes.size
    e = EXPERTS_PER_GROUP
    lgrp = lg_ref[0:N_GROUPS, lanes]
    gidx = lax.broadcasted_iota(jnp.int32, (N_GROUPS, n), 0)
    m = jnp.max(lgrp, axis=0, keepdims=True)
    ex = jnp.exp(lgrp - m)
    pg = ex / jnp.sum(ex, axis=0, keepdims=True)
    p_grp = jnp.max(pg, axis=0, keepdims=True)
    grp = jnp.min(jnp.where(pg == p_grp, gidx, N_GROUPS), axis=0, keepdims=True)
    le = jnp.zeros((e, n), F32)
    for g in range(N_GROUPS):
        rows = lg_ref[EXPERT_ROW0 + g * e:EXPERT_ROW0 + (g + 1) * e, lanes]
        le = jnp.where(grp == g, rows, le)
    m = jnp.max(le, axis=0, keepdims=True)
    ex = jnp.exp(le - m)
    pe = ex / jnp.sum(ex, axis=0, keepdims=True)
    eidx = lax.broadcasted_iota(jnp.int32, (e, n), 0)
    p1 = jnp.max(pe, axis=0, keepdims=True)
    i1 = jnp.min(jnp.where(pe == p1, eidx, e), axis=0, keepdims=True)
    pe2 = jnp.where(eidx == i1, -1.0, pe)
    p2 = jnp.max(pe2, axis=0, keepdims=True)
    i2 = jnp.min(jnp.where(pe2 == p2, eidx, e), axis=0, keepdims=True)
    denom = p1 + p2
    base = grp * e
    return (base + i1, base + i2), (p_grp * p1 / denom, p_grp * p2 / denom)


def _route_kernel(lg_ref, gate_ref, eid_ref, cnt_ref):
    tl = lg_ref.shape[1]
    chunk = 2 * LANES
    ne = N_EXPERTS

    @pl.when(pl.program_id(0) == 0)
    def _():
        cnt_ref[...] = jnp.zeros_like(cnt_ref)

    ones = jnp.ones((chunk, LANES), BF16)
    eidx = lax.broadcasted_iota(jnp.int32, (ne, chunk), 0)
    total = jnp.zeros((ne, LANES), F32)
    for cix in range(tl // chunk):
        lanes = pl.ds(cix * chunk, chunk)
        eids, gates = _route_chunk(lg_ref, lanes)
        for k in range(TOP_K):
            total = total + _dot((eidx == eids[k]).astype(BF16), ones)
        gate_ref[:, lanes] = jnp.concatenate(gates, axis=0)
        eid_ref[:, lanes] = jnp.concatenate(eids, axis=0)
    cnt_ref[...] = cnt_ref[...] + total


def _route(logits_t):
    rows, t = logits_t.shape
    tl = ROUTE_LANES
    blk = pl.BlockSpec((TOP_K, tl), lambda i: (0, i))
    return pl.pallas_call(
        _route_kernel,
        out_shape=[jax.ShapeDtypeStruct((TOP_K, t), F32), jax.ShapeDtypeStruct((TOP_K, t), jnp.int32),
                   jax.ShapeDtypeStruct((N_EXPERTS, LANES), F32)],
        grid=(t // tl,),
        in_specs=[pl.BlockSpec((rows, tl), lambda i: (0, i))],
        out_specs=[blk, blk, pl.BlockSpec((N_EXPERTS, LANES), lambda i: (0, 0))],
        compiler_params=_cparams(("arbitrary",)),
        name="route",
    )(logits_t)


def _index_bit(j, lane, sub, tile):
    lane_bits = LANES.bit_length() - 1
    tile_bits = lane_bits + SUBLANES.bit_length() - 1
    if j < lane_bits:
        return (lane >> j) & 1
    if j < tile_bits:
        return (sub >> (j - lane_bits)) & 1
    return (tile >> (j - tile_bits)) & 1


def _bitonic_sort(tiles):
    n_t = len(tiles)
    log_n = (n_t * SUBLANES * LANES).bit_length() - 1
    lane_bits = LANES.bit_length() - 1
    tile_bits = lane_bits + SUBLANES.bit_length() - 1
    lane = lax.broadcasted_iota(jnp.int32, (SUBLANES, LANES), 1)
    sub = lax.broadcasted_iota(jnp.int32, (SUBLANES, LANES), 0)
    for k in range(1, log_n + 1):
        for j in range(k - 1, -1, -1):
            new = []
            for v in range(n_t):
                x = tiles[v]
                bj = _index_bit(j, lane, sub, v)
                if j < lane_bits:
                    d = 1 << j
                    p = jnp.where(bj == 1, pltpu.roll(x, d, 1), pltpu.roll(x, LANES - d, 1))
                elif j < tile_bits:
                    m = 1 << (j - lane_bits)
                    p = jnp.where(bj == 1, pltpu.roll(x, m, 0), pltpu.roll(x, SUBLANES - m, 0))
                else:
                    p = tiles[v ^ (1 << (j - tile_bits))]
                bk = _index_bit(k, lane, sub, v) if k < log_n else 0
                take_min = bj == bk
                lo, hi = jnp.minimum(x, p), jnp.maximum(x, p)
                if isinstance(take_min, bool):
                    new.append(lo if take_min else hi)
                else:
                    new.append(jnp.where(take_min, lo, hi))
            tiles = new
    return tiles


def _sort_kernel(eid_ref, src_ref):
    n_t = eid_ref.shape[0] // SUBLANES
    n_bits = (eid_ref.shape[0] * LANES).bit_length() - 1
    lane = lax.broadcasted_iota(jnp.int32, (SUBLANES, LANES), 1)
    sub = lax.broadcasted_iota(jnp.int32, (SUBLANES, LANES), 0)
    tiles = []
    for v in range(n_t):
        n = (v * SUBLANES + sub) * LANES + lane
        tiles.append((eid_ref[v * SUBLANES:(v + 1) * SUBLANES, :] << n_bits) | n)
    tiles = _bitonic_sort(tiles)
    for v in range(n_t):
        src_ref[v * SUBLANES:(v + 1) * SUBLANES, :] = (tiles[v] & ((1 << n_bits) - 1)) * ROW_TILE


def _sort_assignments(eid_rows):
    n = eid_rows.shape[0] * eid_rows.shape[1]
    assert n & (n - 1) == 0 and (N_EXPERTS * n) < 2 ** 31, "keys expert*n + assignment must fit an int32"
    return pl.pallas_call(
        _sort_kernel,
        out_shape=jax.ShapeDtypeStruct(eid_rows.shape, jnp.int32),
        name="dispatch_sort",
    )(eid_rows)


def _tables_kernel(cnt_ref, blk_e_ref, nxt_ref, base_ref, nvalid_ref, nact_ref, *, bm):
    n_blocks = blk_e_ref.shape[0]
    shift = bm.bit_length() - 1

    def per_expert(ex, carry):
        blk0, start = carry
        cnt = cnt_ref[ex]
        end_blk = blk0 + ((cnt + (bm - 1)) >> shift)

        def fill_blk(b, c):
            done = (b - blk0) << shift
            blk_e_ref[b] = ex
            nxt_ref[b] = end_blk
            base_ref[b] = start + done
            nvalid_ref[b] = jnp.minimum(cnt - done, bm)
            return c
        lax.fori_loop(blk0, end_blk, fill_blk, 0)
        return end_blk, start + cnt
    nact, _ = lax.fori_loop(0, N_EXPERTS, per_expert, (0, 0))
    nact_ref[0] = nact

    def fill_idle(b, c):
        blk_e_ref[b] = N_EXPERTS - 1
        nxt_ref[b] = n_blocks
        base_ref[b] = 0
        nvalid_ref[b] = 0
        return c
    lax.fori_loop(nact, n_blocks, fill_idle, 0)


def _tables(counts, n_blocks, bm):
    smem = pl.BlockSpec(memory_space=pltpu.SMEM)
    blocks = jax.ShapeDtypeStruct((n_blocks,), jnp.int32)
    return pl.pallas_call(
        functools.partial(_tables_kernel, bm=bm),
        out_shape=[blocks, blocks, blocks, blocks, jax.ShapeDtypeStruct((1,), jnp.int32)],
        in_specs=[smem],
        out_specs=[smem] * 5,
        name="dispatch_tables",
    )(counts)


def _moe_kernel(blk_e_ref, nxt_ref, base_ref, nvalid_ref, nact_ref, src_ref,
                w1_hbm, w3_hbm, w2_hbm, h_hbm, y_hbm,
                wf1, wf3, wf2, w1b, w3b, w2b, xbuf, ybuf, gsem, ssem, wsem, *, n_tok):
    i = pl.program_id(0)
    nact = nact_ref[0]
    n_blocks = blk_e_ref.shape[0]
    bm = xbuf.shape[1] // ROW_TILE
    ring = xbuf.shape[0]
    de = w1b.shape[1]
    chunk = 2 * LANES
    n_chunks = de // chunk
    rows_per_chunk = 2 * bm // n_chunks

    def tile(row):
        return pl.ds(pl.multiple_of(row * ROW_TILE, ROW_TILE), ROW_TILE)

    def tiles_at(offset):
        return pl.ds(pl.multiple_of(offset, ROW_TILE), ROW_TILE)

    def in_offset(out_offset):
        span = ROW_TILE * n_tok
        if span & (span - 1) == 0:
            return out_offset & (span - 1)
        return lax.rem(out_offset, span)

    def gather_start(blk, sl, rows):
        first = base_ref[blk]
        for r in rows:
            src = tiles_at(in_offset(src_ref[first + r]))
            pltpu.make_async_copy(h_hbm.at[src, :], xbuf.at[sl, tile(r), :], gsem.at[sl]).start()

    def scatter_start(blk, sl, rows, real, priorities):
        first = base_ref[blk]
        valid = jnp.where(real, nvalid_ref[blk], 0)
        spare0 = spare_row(jnp.where(real, lax.rem(blk, ring), ring - 1), 0) * ROW_TILE
        for r in rows:
            dst = tiles_at(jnp.where(r < valid, src_ref[first + r], spare0 + r * ROW_TILE))
            pltpu.make_async_copy(ybuf.at[sl, tile(r), :], y_hbm.at[dst, :],
                                  ssem.at[sl]).start(priority=r % priorities)

    def gather_wait(sl):
        pltpu.make_async_copy(h_hbm.at[pl.ds(0, bm * ROW_TILE), :], xbuf.at[sl], gsem.at[sl]).wait()

    def scatter_wait(sl):
        pltpu.make_async_copy(ybuf.at[sl], y_hbm.at[pl.ds(0, bm * ROW_TILE), :], ssem.at[sl]).wait()

    def spare_row(set_id, r):
        return TOP_K * n_tok + set_id * bm + r

    @pl.when(i < nact)
    def _():
        @pl.when(i == 0)
        def _():
            for blk in range(ring - 1):
                gather_start(min(blk, n_blocks - 1), blk, range(bm))
            ybuf[...] = jnp.zeros_like(ybuf)
            for sl in range(ring):
                spare = pltpu.make_async_copy(
                    ybuf.at[sl], y_hbm.at[pl.ds(spare_row(sl, 0) * ROW_TILE, bm * ROW_TILE), :], ssem.at[sl])
                spare.start()
                spare.wait()

        def weight_copies(e):
            return [pltpu.make_async_copy(w_hbm.at[e], stage, wsem.at[k])
                    for k, (w_hbm, stage) in enumerate(((w1_hbm, wf1), (w3_hbm, wf3), (w2_hbm, wf2)))]

        this_e = blk_e_ref[i]

        @pl.when(i == 0)
        def _():
            for cp in weight_copies(this_e):
                cp.start()

        @pl.when((i == 0) | (this_e != blk_e_ref[jnp.maximum(i - 1, 0)]))
        def _():
            for cp in weight_copies(this_e):
                cp.wait()
            for stage, wb in ((wf1, w1b), (wf3, w3b), (wf2, w2b)):
                def cast_rows(rb, carry, stage=stage, wb=wb):
                    rows = pl.ds(pl.multiple_of(rb * LANES, LANES), LANES)
                    wb[rows, :] = stage[rows, :].astype(BF16)
                    return carry
                lax.fori_loop(0, stage.shape[0] // LANES, cast_rows, 0)
            nxt_blk = nxt_ref[i]

            @pl.when(nxt_blk < nact)
            def _():
                for cp in weight_copies(blk_e_ref[nxt_blk]):
                    cp.start(priority=1)

        ahead = jnp.minimum(i + ring - 1, n_blocks - 1)
        prv = jnp.maximum(i - 1, 0)

        def compute_block(slot):
            far_slot = (slot - 1) % ring
            others = [(slot - 1 - k) % ring for k in range(ring - 1)]
            gather_wait(slot)
            xb = _load_row_tiles(xbuf.at[slot], bm).astype(BF16)
            acts = []
            for c in range(n_chunks):
                cols = slice(c * chunk, (c + 1) * chunk)
                a = _dot(xb, w1b[:, cols])
                g = _dot(xb, w3b[:, cols])
                acts.append((_silu(a) * g).astype(BF16))
                if c < n_chunks // 2:
                    gather_start(ahead, far_slot, range(c * rows_per_chunk, (c + 1) * rows_per_chunk))
                else:
                    first = (c - n_chunks // 2) * rows_per_chunk
                    scatter_start(prv, far_slot, range(first, first + rows_per_chunk), i > 0, 2)

            y = _dot(jnp.concatenate(acts, axis=1), w2b[...])

            @pl.when(i >= ring - 1)
            def _():
                scatter_wait(slot)

            _store_row_tiles(ybuf.at[slot], y)

            @pl.when(i == nact - 1)
            def _():
                scatter_start(i, slot, range(bm), True, 1)
                for k, sl in enumerate(others):
                    pl.when(i >= k)(functools.partial(scatter_wait, sl))
                    gather_wait(sl)
                scatter_wait(slot)

        for parity in range(ring):
            pl.when(lax.rem(i, ring) == parity)(functools.partial(compute_block, parity))


def _moe(blk_e, nxt_blk, base, nvalid, nact, src, w1, w3, w2, h2_tiles, n_blocks):
    t = h2_tiles.shape[0] // ROW_TILE
    d = w1.shape[1]
    de = w1.shape[2]
    bm = MOE_BLOCK
    hbm = pl.BlockSpec(memory_space=pl.ANY)
    grid_spec = pltpu.PrefetchScalarGridSpec(
        num_scalar_prefetch=6,
        grid=(n_blocks,),
        in_specs=[hbm, hbm, hbm, hbm],
        out_specs=hbm,
        scratch_shapes=[pltpu.VMEM((d, de), F32), pltpu.VMEM((d, de), F32), pltpu.VMEM((de, d), F32),
                        pltpu.VMEM((d, de), BF16), pltpu.VMEM((d, de), BF16), pltpu.VMEM((de, d), BF16),
                        pltpu.VMEM((MOE_RING, bm * ROW_TILE, LANES), F32),
                        pltpu.VMEM((MOE_RING, bm * ROW_TILE, LANES), F32),
                        pltpu.SemaphoreType.DMA((MOE_RING,)), pltpu.SemaphoreType.DMA((MOE_RING,)),
                        pltpu.SemaphoreType.DMA((3,))],
    )
    return pl.pallas_call(
        functools.partial(_moe_kernel, n_tok=t),
        out_shape=jax.ShapeDtypeStruct(((TOP_K * t + MOE_RING * bm) * ROW_TILE, LANES), F32),
        grid_spec=grid_spec,
        compiler_params=_cparams(("arbitrary",)),
        name="moe_experts",
    )(blk_e, nxt_blk, base, nvalid, nact, src, w1, w3, w2, h2_tiles)


def _final_kernel(x1_ref, y0_ref, y1_ref, gt_ref, g2_ref, fg_ref, o_ref):
    gt = gt_ref[...]
    tm = x1_ref.shape[0]
    y2 = gt[:, 0:1] * _load_row_tiles(y0_ref, tm) + gt[:, 1:2] * _load_row_tiles(y1_ref, tm)
    x = x1_ref[...] + g2_ref[0] * y2
    o_ref[...] = x * lax.rsqrt(jnp.mean(x * x, axis=-1, keepdims=True) + EPS) * fg_ref[...]


def _final(x1, y, gates_tok, g2, fg, rows_per_mod, tm):
    t, d = x1.shape
    nb = t // tm
    blocks_per_mod = rows_per_mod // tm
    return pl.pallas_call(
        _final_kernel,
        out_shape=jax.ShapeDtypeStruct((t, d), F32),
        grid=(nb,),
        in_specs=[pl.BlockSpec((tm, d), lambda i: (i, 0)),
                  pl.BlockSpec((tm * ROW_TILE, LANES), lambda i: (i, 0)),
                  pl.BlockSpec((tm * ROW_TILE, LANES), lambda i: (i + nb, 0)),
                  pl.BlockSpec((tm, TOP_K), lambda i: (i, 0)),
                  pl.BlockSpec((1, 1, d), lambda i: (i // blocks_per_mod, 0, 0)),
                  pl.BlockSpec((1, d), lambda i: (0, 0))],
        out_specs=pl.BlockSpec((tm, d), lambda i: (i, 0)),
        compiler_params=_cparams(("arbitrary",)),
        name="combine_final_norm",
    )(x1, y, y, gates_tok, g2, fg)


def kernel(x, c, ctx, c_ctx, w_ada, b_ada, norm1_g, norm2_g, w_in, conv_dw, conv_b, conv_ln_g, conv_ln_b,
           lru_conv_w, lru_conv_b, lru_wa, lru_ba, lru_wx, lru_bx, lru_lam, w_out,
           router_wg, router_bg, router_we, router_be, w1, w3, w2, final_g):
    assert w_ada.shape[0] == 1, "single-layer block"
    assert x.shape[2] == ROW_TILE * LANES, "row-tile layout: one (ROW_TILE, LANES) tile per token row"
    b, s, d = x.shape
    n_ctx = ctx.shape[1]
    t = b * s
    cc = conv_dw.shape[2]
    lw = lru_conv_w.shape[2]

    c_rows = jnp.zeros((SUBLANES, d), F32).at[:b].set(c).at[b].set(c_ctx)
    mod = _ada(c_rows, w_ada[0], b_ada)
    mod_l = mod[:b].reshape(b, 6, 1, d)
    sh1, sc1, g1, sh2, sc2, g2 = (mod_l[:, k] for k in range(6))
    mod_c = mod[b].reshape(6, 1, 1, d)
    csh1, csc1 = mod_c[0], mod_c[1]

    w_in_b = w_in[0].astype(BF16)
    w_out_b = w_out[0].astype(BF16)
    heads_per_blk = LRU_LANES // lru_wa.shape[3]
    n_cblk = lw // LRU_LANES

    def blockdiag(wh):
        hd = wh.shape[1]
        wh = wh.reshape(n_cblk, heads_per_blk, hd, hd)
        eye = jnp.eye(heads_per_blk, dtype=wh.dtype)
        return jnp.einsum("chij,hg->chigj", wh, eye).reshape(n_cblk, LRU_LANES, LRU_LANES)

    wg = (0.5 * jnp.concatenate([blockdiag(lru_wa[0, 0]), blockdiag(lru_wx[0, 0]),
                                 blockdiag(lru_wa[0, 1]), blockdiag(lru_wx[0, 1])], axis=2)).astype(BF16)
    gb = 0.5 * jnp.stack([lru_ba[0, 0], lru_bx[0, 0], lru_ba[0, 1], lru_bx[0, 1]])
    lam = lru_lam[0]

    zc = _inproj_ctx(ctx.reshape(b * n_ctx, d), csh1, csc1, norm1_g, w_in_b[:, 2 * cc:2 * cc + lw], n_ctx)
    h0 = _lru(zc.reshape(b, n_ctx, lw), None, jnp.zeros((2, b, 1, lw), F32),
              lru_conv_w[0], lru_conv_b, wg, gb, lam, True, "rglru_ctx")

    x2 = x.reshape(t, d)
    conv_w_rep = jnp.repeat(conv_dw[0], SUBLANES, axis=0).reshape(CONV_TAPS * SUBLANES, cc // LANES, LANES)
    conv_w_rep = conv_w_rep.transpose(1, 0, 2)
    conv_l, zl, gg = _inproj(x2, sh1, sc1, norm1_g, w_in_b, conv_w_rep, conv_b, conv_ln_g, conv_ln_b, s)
    lru_l = _lru(zl.reshape(b, s, lw), gg.reshape(b, s, lw), h0,
                 lru_conv_w[0], lru_conv_b, wg, gb, lam, False, "rglru")

    wr = jnp.zeros((ROUTE_ROWS, d), F32)
    wr = wr.at[:N_GROUPS].set(router_wg[0].T)
    wr = wr.at[EXPERT_ROW0:EXPERT_ROW0 + N_EXPERTS].set(router_we[0].reshape(d, N_EXPERTS).T)
    rb = jnp.zeros((ROUTE_ROWS, 1), F32)
    rb = rb.at[:N_GROUPS, 0].set(router_bg[0])
    rb = rb.at[EXPERT_ROW0:EXPERT_ROW0 + N_EXPERTS, 0].set(router_be[0].reshape(-1))
    r_hi, r_lo = _split_bf16(wr)
    x1, h2, logits_t = _outproj(conv_l, lru_l.reshape(t, lw), x2, g1, sh2, sc2, norm2_g, w_out_b,
                                r_hi, jnp.concatenate([r_hi, r_lo], axis=0), rb, s, TOKEN_BLOCK)

    gates, eid, counts = _route(logits_t)
    n_blocks = (TOP_K * t) // MOE_BLOCK + N_EXPERTS
    src = jnp.pad(_sort_assignments(eid.reshape(TOP_K * t // LANES, LANES)).reshape(-1), (0, MOE_BLOCK))
    blk_e, nxt_blk, base, nvalid, nact = _tables(counts[:, 0].astype(jnp.int32), n_blocks, MOE_BLOCK)
    y = _moe(blk_e, nxt_blk, base, nvalid, nact, src, w1[0], w3[0], w2[0], h2, n_blocks)
    out = _final(x1, y, gates.T, g2, final_g.reshape(1, d), s, TOKEN_BLOCK)
    return out.reshape(b, s, d)
```

```python
import functools

import jax
import jax.numpy as jnp
from jax import lax
from jax.experimental import pallas as pl
from jax.experimental.pallas import tpu as pltpu

F32 = jnp.float32
BF16 = jnp.bfloat16

EPS = 1e-6
CONV_TAPS = 31
LRU_TAPS = 4
LRU_C = 8.0
GRID_W = 64
N_GROUPS = 4
EXPERTS_PER_GROUP = 8
N_EXPERTS = N_GROUPS * EXPERTS_PER_GROUP
TOP_K = 2

SUBLANES = 8
LANES = 128
TOKEN_BLOCK = 1024
LRU_LANES = 128
EXPERT_ROW0 = 8
ROUTE_ROWS = EXPERT_ROW0 + N_EXPERTS
MOE_BLOCK = 256
MOE_RING = 3
STREAM_RING = 3
ROUTE_LANES = 2048
ADA_COLS = 1536
V7X_VMEM_BYTES = 64 * 1024 * 1024
VMEM_LIMIT = 3 * V7X_VMEM_BYTES // 4


def _cparams(sem, vmem=VMEM_LIMIT):
    return pltpu.CompilerParams(dimension_semantics=sem, vmem_limit_bytes=vmem)


def _split_bf16(a):
    hi = a.astype(BF16)
    lo = (a - hi.astype(F32)).astype(BF16)
    return hi, lo


def _dot(a, b):
    return jnp.dot(a, b, preferred_element_type=F32)


def _dot_nt(a, b):
    return lax.dot_general(a, b, (((1,), (1,)), ((), ())), preferred_element_type=F32)


def _times_sigmoid(v, x):
    hv = 0.5 * v
    return hv * jnp.tanh(0.5 * x) + hv


def _silu(x):
    h = 0.5 * x
    return h * jnp.tanh(h) + h


def _gelu_tanh(x):
    c = 0.7978845608028654
    h = 0.5 * x
    return h * jnp.tanh(x * ((c * 0.044715) * (x * x) + c)) + h


ROW_TILE = SUBLANES


def _store_row_tiles(ref, x):
    rows = x.shape[0]
    for s in range(ROW_TILE):
        ref[pl.ds(s, rows, stride=ROW_TILE), :] = x[:, s * LANES:(s + 1) * LANES]


def _load_row_tiles(ref, rows):
    return jnp.concatenate([ref[pl.ds(s, rows, stride=ROW_TILE), :] for s in range(ROW_TILE)], axis=1)


def _rms_mod(x, g, shift, scale):
    y = x * lax.rsqrt(jnp.mean(x * x, axis=-1, keepdims=True) + EPS)
    return y * (g * (1.0 + scale)) + shift


def _ada_kernel(c_ref, w_ref, b_ref, o_ref):
    a = _silu(c_ref[...])
    a_hi, a_lo = _split_bf16(a)
    w_hi, w_lo = _split_bf16(w_ref[...])
    o_ref[...] = _dot(a_hi, w_hi) + _dot(a_lo, w_hi) + _dot(a_hi, w_lo) + b_ref[...]


def _ada(c_rows, w, b):
    m, d = c_rows.shape
    n = w.shape[1]
    bn = ADA_COLS
    return pl.pallas_call(
        _ada_kernel,
        out_shape=jax.ShapeDtypeStruct((m, n), F32),
        grid=(n // bn,),
        in_specs=[pl.BlockSpec((m, d), lambda j: (0, 0)),
                  pl.BlockSpec((d, bn), lambda j: (0, j)),
                  pl.BlockSpec((1, bn), lambda j: (0, j))],
        out_specs=pl.BlockSpec((m, bn), lambda j: (0, j)),
        compiler_params=_cparams(("arbitrary",)),
        name="ada_mod",
    )(c_rows, w, b)


def _inproj_ctx_kernel(x_ref, sh_ref, sc_ref, g_ref, w_ref, o_ref):
    h = _rms_mod(x_ref[...], g_ref[...], sh_ref[0], sc_ref[0])
    o_ref[...] = _dot(h.astype(BF16), w_ref[...])


def _conformer_conv(u, cw_ref, cb_ref, lg_ref, lb_ref, o_ref, stg_ref, xt_ref, ot_ref):
    c = u.shape[1]
    slabs = range(c // LANES)
    half = CONV_TAPS // 2
    pitch = GRID_W + SUBLANES
    group = SUBLANES
    for l in slabs:
        for q in range(SUBLANES):
            stg_ref[l, q * pitch:q * pitch + GRID_W, :] = u[q * GRID_W:(q + 1) * GRID_W, l * LANES:(l + 1) * LANES]

    def conv_slab(l, carry):
        for t in range(GRID_W):
            xt_ref[l, t * SUBLANES:(t + 1) * SUBLANES, :] = stg_ref[l, pl.ds(t, SUBLANES, stride=pitch), :]
        for t0 in range(0, GRID_W, group):
            accs = [None] * group
            for k in range(CONV_TAPS):
                srcs = [t0 + j + k - half for j in range(group)]
                if not any(0 <= sidx < GRID_W for sidx in srcs):
                    continue
                wk = cw_ref[l, k * SUBLANES:(k + 1) * SUBLANES, :]
                for j, sidx in enumerate(srcs):
                    if 0 <= sidx < GRID_W:
                        term = wk * xt_ref[l, sidx * SUBLANES:(sidx + 1) * SUBLANES, :]
                        accs[j] = term if accs[j] is None else accs[j] + term
            for j in range(group):
                ot_ref[l, (t0 + j) * SUBLANES:(t0 + j + 1) * SUBLANES, :] = accs[j]
        return carry
    lax.fori_loop(0, c // LANES, conv_slab, 0)
    rows_per_pass = group * SUBLANES
    for r0 in range(0, GRID_W * SUBLANES, rows_per_pass):
        rows = slice(r0, r0 + rows_per_pass)
        acc = jnp.concatenate([ot_ref[l, rows, :] for l in slabs], axis=1) + cb_ref[...]
        mu = jnp.mean(acc, axis=-1, keepdims=True)
        cen = acc - mu
        var = jnp.mean(cen * cen, axis=-1, keepdims=True)
        y = _silu(cen * lax.rsqrt(var + EPS) * lg_ref[...] + lb_ref[...])
        for l in slabs:
            ot_ref[l, rows, :] = y[:, l * LANES:(l + 1) * LANES]
    for q in range(SUBLANES):
        o_ref[q * GRID_W:(q + 1) * GRID_W, :] = jnp.concatenate(
            [ot_ref[l, pl.ds(q, GRID_W, stride=SUBLANES), :] for l in slabs], axis=1).astype(o_ref.dtype)


def _inproj_kernel(x_ref, sh_ref, sc_ref, g_ref, w_ref, cw_ref, cb_ref, lg_ref, lb_ref,
                   cv_ref, zl_ref, gg_ref, stg_ref, xt_ref, ot_ref):
    h = _rms_mod(x_ref[...], g_ref[...], sh_ref[0], sc_ref[0])
    z = _dot(h.astype(BF16), w_ref[...])
    c = cv_ref.shape[1]
    zl_ref[...] = z[:, 2 * c:3 * c]
    gg_ref[...] = _gelu_tanh(z[:, 3 * c:])
    u = _times_sigmoid(z[:, :c], z[:, c:2 * c])
    _conformer_conv(u, cw_ref, cb_ref, lg_ref, lb_ref, cv_ref, stg_ref, xt_ref, ot_ref)


def _inproj_ctx(x2, shift, scale, g, w, tm):
    t, d = x2.shape
    n = w.shape[1]
    mod_spec = pl.BlockSpec((1, 1, d), lambda i: (0, 0, 0))
    return pl.pallas_call(
        _inproj_ctx_kernel,
        out_shape=jax.ShapeDtypeStruct((t, n), F32),
        grid=(t // tm,),
        in_specs=[pl.BlockSpec((tm, d), lambda i: (i, 0)), mod_spec, mod_spec,
                  pl.BlockSpec((1, d), lambda i: (0, 0)),
                  pl.BlockSpec((d, n), lambda i: (0, 0))],
        out_specs=pl.BlockSpec((tm, n), lambda i: (i, 0)),
        compiler_params=_cparams(("arbitrary",)),
        name="in_proj_ctx",
    )(x2, shift, scale, g, w)


def _inproj(x2, shift, scale, g, w, cw, cb, lg, lb, rows_per_mod):
    t, d = x2.shape
    n = w.shape[1]
    c = n // 4
    tm = SUBLANES * GRID_W
    blocks_per_mod = rows_per_mod // tm
    mod_spec = pl.BlockSpec((1, 1, d), lambda i: (i // blocks_per_mod, 0, 0))
    const = lambda a: pl.BlockSpec(a.shape, lambda i: (0,) * a.ndim)
    rows = pl.BlockSpec((tm, c), lambda i: (i, 0))
    return pl.pallas_call(
        _inproj_kernel,
        out_shape=[jax.ShapeDtypeStruct((t, c), BF16), jax.ShapeDtypeStruct((t, c), F32),
                   jax.ShapeDtypeStruct((t, c), F32)],
        grid=(t // tm,),
        in_specs=[pl.BlockSpec((tm, d), lambda i: (i, 0)), mod_spec, mod_spec, const(g), const(w),
                  const(cw), const(cb), const(lg), const(lb)],
        out_specs=[rows, rows, rows],
        scratch_shapes=[pltpu.VMEM((c // LANES, SUBLANES * (GRID_W + SUBLANES), LANES), F32),
                        pltpu.VMEM((c // LANES, tm, LANES), F32), pltpu.VMEM((c // LANES, tm, LANES), F32)],
        compiler_params=_cparams(("arbitrary",)),
        name="in_proj",
    )(x2, shift, scale, g, w, cw, cb, lg, lb)


def _lru_kernel(zl_ref, h0_ref, cw_ref, cb_ref, wg_ref, gb_ref, lam_ref, *rest, seq, final_only):
    if final_only:
        o_ref, stage_ref, zt_ref, af_ref, bf_ref, ab_ref, bb_ref, pf_ref, hf_ref, pb_ref, hb_ref = rest
        gg_ref = None
    else:
        gg_ref, o_ref, stage_ref, zt_ref, af_ref, bf_ref, ab_ref, bb_ref, pf_ref, hf_ref, pb_ref, hb_ref = rest
    cl = seq // SUBLANES
    pitch = cl + SUBLANES
    c = zl_ref.shape[2]
    a_refs = (af_ref, ab_ref)
    b_refs = (bf_ref, bb_ref)
    halo = 2

    def tile_rows(t):
        return pl.ds(pl.multiple_of(t * SUBLANES, SUBLANES), SUBLANES)

    for j in range(SUBLANES):
        stage_ref[j * pitch:j * pitch + cl, :] = zl_ref[0, j * cl:(j + 1) * cl, :]

    def to_chunk_layout(t, carry):
        zt_ref[tile_rows(t + halo), :] = stage_ref[pl.ds(t, SUBLANES, stride=pitch), :]
        return carry
    lax.fori_loop(0, cl, to_chunk_layout, 0, unroll=8)

    sub = lax.broadcasted_iota(jnp.int32, (SUBLANES, c), 0)
    for t_src, t_dst in ((cl - 2, -2), (cl - 1, -1)):
        v = pltpu.roll(zt_ref[(t_src + halo) * SUBLANES:(t_src + halo + 1) * SUBLANES, :], 1, 0)
        zt_ref[(t_dst + halo) * SUBLANES:(t_dst + halo + 1) * SUBLANES, :] = jnp.where(sub == 0, 0.0, v)
    v = pltpu.roll(zt_ref[halo * SUBLANES:(halo + 1) * SUBLANES, :], SUBLANES - 1, 0)
    zt_ref[(cl + halo) * SUBLANES:(cl + halo + 1) * SUBLANES, :] = jnp.where(sub == SUBLANES - 1, 0.0, v)

    lam = lam_ref[...]
    nlam = -lam
    softplus = jnp.maximum(nlam, 0.0) + jnp.log1p(jnp.exp(-jnp.abs(nlam)))
    half_decay = (-0.5 * LRU_C) * softplus
    wg = wg_ref[0]

    piece = cl
    for p0 in range(0, cl * SUBLANES, piece):
        ul = cb_ref[...] + jnp.zeros((piece, c), F32)
        for k in range(LRU_TAPS):
            off = p0 + k * SUBLANES
            ul = ul + cw_ref[k:k + 1, :] * zt_ref[off:off + piece, :]
        g = _dot(ul.astype(BF16), wg)
        for d in range(2):
            t_r = jnp.tanh(g[:, (2 * d) * c:(2 * d + 1) * c] + gb_ref[2 * d:2 * d + 1, :])
            i = 0.5 * jnp.tanh(g[:, (2 * d + 1) * c:(2 * d + 2) * c] + gb_ref[2 * d + 1:2 * d + 2, :]) + 0.5
            log_a = half_decay[d:d + 1, :] * t_r + half_decay[d:d + 1, :]
            a = jnp.exp(log_a)
            m = jnp.maximum(-jnp.tanh(log_a) * (a * a + 1.0), 1e-12)
            mult = m * lax.rsqrt(m)
            a_refs[d][p0:p0 + piece, :] = a
            b_refs[d][p0:p0 + piece, :] = mult * i * ul

    def two_steps(a_ref, b_ref, p_out, h_out, s0, s1, h, p):
        a0 = a_ref[s0, :]
        a1 = a_ref[s1, :]
        b0 = b_ref[s0, :]
        a10 = a1 * a0
        b10 = a1 * b0 + b_ref[s1, :]
        h_out[s0, :] = a0 * h + b0
        p_out[s0, :] = a0 * p
        h = a10 * h + b10
        p = a10 * p
        h_out[s1, :] = h
        p_out[s1, :] = p
        return h, p

    group = 8
    group_rows = group * SUBLANES

    def step(n, carry):
        hf, pf, hb, pb = carry
        base_f = pl.multiple_of(n * group_rows, group_rows)
        base_b = pl.multiple_of((cl // group - 1 - n) * group_rows, group_rows)
        for k in range(0, group, 2):
            tf0 = pl.ds(base_f + k * SUBLANES, SUBLANES)
            tf1 = pl.ds(base_f + (k + 1) * SUBLANES, SUBLANES)
            hf, pf = two_steps(af_ref, bf_ref, pf_ref, hf_ref, tf0, tf1, hf, pf)
            tb0 = pl.ds(base_b + (group - 1 - k) * SUBLANES, SUBLANES)
            tb1 = pl.ds(base_b + (group - 2 - k) * SUBLANES, SUBLANES)
            hb, pb = two_steps(ab_ref, bb_ref, pb_ref, hb_ref, tb0, tb1, hb, pb)
        return hf, pf, hb, pb

    zero = jnp.zeros((SUBLANES, c), F32)
    one = jnp.ones((SUBLANES, c), F32)
    lax.fori_loop(0, cl // group, step, (zero, one, zero, one))

    cf = [None] * SUBLANES
    cbk = [None] * SUBLANES
    s = h0_ref[0, 0]
    for j in range(SUBLANES):
        cf[j] = s
        last = (cl - 1) * SUBLANES + j
        s = hf_ref[last:last + 1, :] + pf_ref[last:last + 1, :] * s
    final_f = s
    s = h0_ref[1, 0]
    for j in reversed(range(SUBLANES)):
        cbk[j] = s
        s = hb_ref[j:j + 1, :] + pb_ref[j:j + 1, :] * s
    final_b = s

    if final_only:
        o_ref[0, 0] = final_f
        o_ref[1, 0] = final_b
        return
    carry_f = jnp.concatenate(cf, axis=0)
    carry_b = jnp.concatenate(cbk, axis=0)

    def to_natural(t, carry):
        rows = tile_rows(t)
        h = (hf_ref[rows, :] + pf_ref[rows, :] * carry_f) + (hb_ref[rows, :] + pb_ref[rows, :] * carry_b)
        stage_ref[pl.ds(t, SUBLANES, stride=pitch), :] = h
        return carry
    lax.fori_loop(0, cl, to_natural, 0, unroll=8)
    for j in range(SUBLANES):
        h = stage_ref[j * pitch:j * pitch + cl, :]
        o_ref[0, j * cl:(j + 1) * cl, :] = (h * gg_ref[0, j * cl:(j + 1) * cl, :]).astype(o_ref.dtype)


def _lru(zl, gg, h0, cw, cb, wg, gb, lam, final_only, name):
    b, seq, c = zl.shape
    cbk = LRU_LANES
    cl = seq // SUBLANES
    pitch = cl + SUBLANES
    seq_spec = pl.BlockSpec((1, seq, cbk), lambda bi, ci: (bi, 0, ci))
    st_spec = pl.BlockSpec((2, 1, 1, cbk), lambda bi, ci: (0, bi, 0, ci))
    chan = lambda rows: pl.BlockSpec((rows, cbk), lambda bi, ci: (0, ci))
    in_specs = [seq_spec, st_spec, chan(LRU_TAPS), chan(1),
                pl.BlockSpec((1, cbk, 4 * cbk), lambda bi, ci: (ci, 0, 0)), chan(4), chan(2)]
    args = [zl, h0, cw, cb, wg, gb, lam]
    if final_only:
        out_shape = jax.ShapeDtypeStruct((2, b, 1, c), F32)
        out_spec = st_spec
    else:
        in_specs.append(seq_spec)
        args.append(gg)
        out_shape = jax.ShapeDtypeStruct((b, seq, c), BF16)
        out_spec = seq_spec
    coef = pltpu.VMEM((seq, cbk), F32)
    return pl.pallas_call(
        functools.partial(_lru_kernel, seq=seq, final_only=final_only),
        out_shape=out_shape,
        grid=(b, c // cbk),
        in_specs=in_specs,
        out_specs=out_spec,
        scratch_shapes=[pltpu.VMEM((SUBLANES * pitch, cbk), F32),
                        pltpu.VMEM((seq + 2 * SUBLANES * SUBLANES, cbk), F32)] + [coef] * 8,
        compiler_params=_cparams(("arbitrary", "arbitrary")),
        name=name,
    )(*args)


def _stream_rows(streams, sem):
    i = pl.program_id(0)
    n = pl.num_programs(0)

    def copies(step):
        slot = lax.rem(step, STREAM_RING)
        out = []
        for k, (src, buf) in enumerate(streams):
            rows = buf.shape[1]
            src_rows = pl.ds(pl.multiple_of(step * rows, rows), rows)
            out.append(pltpu.make_async_copy(src.at[src_rows], buf.at[slot], sem.at[k, slot]))
        return out

    @pl.when(i == 0)
    def _():
        for ahead in range(STREAM_RING - 1):
            for cp in copies(i + ahead):
                cp.start()

    @pl.when(i + STREAM_RING - 1 < n)
    def _():
        for cp in copies(i + STREAM_RING - 1):
            cp.start()

    for cp in copies(i):
        cp.wait()
    return lax.rem(i, STREAM_RING)


def _outproj_kernel(cv_hbm, lr_hbm, x_hbm, g1_ref, sh_ref, sc_ref, ng_ref, wo_ref, rh_ref, rc_ref, rb_ref,
                    x1_ref, h2_ref, lg_ref, cv_buf, lr_buf, x_buf, sem):
    c = cv_buf.shape[2]
    slot = _stream_rows(((cv_hbm, cv_buf), (lr_hbm, lr_buf), (x_hbm, x_buf)), sem)
    y = _dot(cv_buf[slot], wo_ref[0:c, :]) + _dot(lr_buf[slot], wo_ref[c:2 * c, :])
    x1 = x_buf[slot] + g1_ref[0] * y
    x1_ref[...] = x1
    h2 = _rms_mod(x1, ng_ref[...], sh_ref[0], sc_ref[0])
    h_hi, h_lo = _split_bf16(h2)
    _store_row_tiles(h2_ref, h2)
    rows = rh_ref.shape[0]
    both = _dot_nt(rc_ref[...], h_hi)
    lg_ref[...] = both[:rows] + both[rows:] + _dot_nt(rh_ref[...], h_lo) + rb_ref[...]


def _outproj(conv_l, lru_l, x2, g1, sh2, sc2, ng, wo, r_hi, r_both, r_b, rows_per_mod, tm):
    t, d = x2.shape
    c = conv_l.shape[1]
    blocks_per_mod = rows_per_mod // tm
    mod_spec = pl.BlockSpec((1, 1, d), lambda i: (i // blocks_per_mod, 0, 0))
    full = lambda a: pl.BlockSpec(a.shape, lambda i: (0, 0))
    hbm = pl.BlockSpec(memory_space=pl.ANY)
    return pl.pallas_call(
        _outproj_kernel,
        out_shape=[jax.ShapeDtypeStruct((t, d), F32), jax.ShapeDtypeStruct((t * ROW_TILE, LANES), F32),
                   jax.ShapeDtypeStruct((ROUTE_ROWS, t), F32)],
        grid=(t // tm,),
        in_specs=[hbm, hbm, hbm, mod_spec, mod_spec, mod_spec,
                  full(ng), full(wo), full(r_hi), full(r_both), full(r_b)],
        out_specs=[pl.BlockSpec((tm, d), lambda i: (i, 0)), pl.BlockSpec((tm * ROW_TILE, LANES), lambda i: (i, 0)),
                   pl.BlockSpec((ROUTE_ROWS, tm), lambda i: (0, i))],
        scratch_shapes=[pltpu.VMEM((STREAM_RING, tm, c), conv_l.dtype), pltpu.VMEM((STREAM_RING, tm, c), lru_l.dtype),
                        pltpu.VMEM((STREAM_RING, tm, d), F32), pltpu.SemaphoreType.DMA((3, STREAM_RING))],
        compiler_params=_cparams(("arbitrary",)),
        name="out_proj_router",
    )(conv_l, lru_l, x2, g1, sh2, sc2, ng, wo, r_hi, r_both, r_b)


def _route_chunk(lg_ref, lanes):
    n = lanes.size
    e = EXPERTS_PER_GROUP
    lgrp = lg_ref[0:N_GROUPS, lanes]
    gidx = lax.broadcasted_iota(jnp.int32, (N_GROUPS, n), 0)
    m = jnp.max(lgrp, axis=0, keepdims=True)
    ex = jnp.exp(lgrp - m)
    pg = ex / jnp.sum(ex, axis=0, keepdims=True)
    p_grp = jnp.max(pg, axis=0, keepdims=True)
    grp = jnp.min(jnp.where(pg == p_grp, gidx, N_GROUPS), axis=0, keepdims=True)
    le = jnp.zeros((e, n), F32)
    for g in range(N_GROUPS):
        rows = lg_ref[EXPERT_ROW0 + g * e:EXPERT_ROW0 + (g + 1) * e, lanes]
        le = jnp.where(grp == g, rows, le)
    m = jnp.max(le, axis=0, keepdims=True)
    ex = jnp.exp(le - m)
    pe = ex / jnp.sum(ex, axis=0, keepdims=True)
    eidx = lax.broadcasted_iota(jnp.int32, (e, n), 0)
    p1 = jnp.max(pe, axis=0, keepdims=True)
    i1 = jnp.min(jnp.where(pe == p1, eidx, e), axis=0, keepdims=True)
    pe2 = jnp.where(eidx == i1, -1.0, pe)
    p2 = jnp.max(pe2, axis=0, keepdims=True)
    i2 = jnp.min(jnp.where(pe2 == p2, eidx, e), axis=0, keepdims=True)
    denom = p1 + p2
    base = grp * e
    return (base + i1, base + i2), (p_grp * p1 / denom, p_grp * p2 / denom)


def _route_kernel(lg_ref, gate_ref, eid_ref, cnt_ref):
    tl = lg_ref.shape[1]
    chunk = 2 * LANES
    ne = N_EXPERTS

    @pl.when(pl.program_id(0) == 0)
    def _():
        cnt_ref[...] = jnp.zeros_like(cnt_ref)

    ones = jnp.ones((chunk, LANES), BF16)
    eidx = lax.broadcasted_iota(jnp.int32, (ne, chunk), 0)
    total = jnp.zeros((ne, LANES), F32)
    for cix in range(tl // chunk):
        lanes = pl.ds(cix * chunk, chunk)
        eids, gates = _route_chunk(lg_ref, lanes)
        for k in range(TOP_K):
            total = total + _dot((eidx == eids[k]).astype(BF16), ones)
        gate_ref[:, lanes] = jnp.concatenate(gates, axis=0)
        eid_ref[:, lanes] = jnp.concatenate(eids, axis=0)
    cnt_ref[...] = cnt_ref[...] + total


def _route(logits_t):
    rows, t = logits_t.shape
    tl = ROUTE_LANES
    blk = pl.BlockSpec((TOP_K, tl), lambda i: (0, i))
    return pl.pallas_call(
        _route_kernel,
        out_shape=[jax.ShapeDtypeStruct((TOP_K, t), F32), jax.ShapeDtypeStruct((TOP_K, t), jnp.int32),
                   jax.ShapeDtypeStruct((N_EXPERTS, LANES), F32)],
        grid=(t // tl,),
        in_specs=[pl.BlockSpec((rows, tl), lambda i: (0, i))],
        out_specs=[blk, blk, pl.BlockSpec((N_EXPERTS, LANES), lambda i: (0, 0))],
        compiler_params=_cparams(("arbitrary",)),
        name="route",
    )(logits_t)


def _index_bit(j, lane, sub, tile):
    lane_bits = LANES.bit_length() - 1
    tile_bits = lane_bits + SUBLANES.bit_length() - 1
    if j < lane_bits:
        return (lane >> j) & 1
    if j < tile_bits:
        return (sub >> (j - lane_bits)) & 1
    return (tile >> (j - tile_bits)) & 1


def _bitonic_sort(tiles):
    n_t = len(tiles)
    log_n = (n_t * SUBLANES * LANES).bit_length() - 1
    lane_bits = LANES.bit_length() - 1
    tile_bits = lane_bits + SUBLANES.bit_length() - 1
    lane = lax.broadcasted_iota(jnp.int32, (SUBLANES, LANES), 1)
    sub = lax.broadcasted_iota(jnp.int32, (SUBLANES, LANES), 0)
    for k in range(1, log_n + 1):
        for j in range(k - 1, -1, -1):
            new = []
            for v in range(n_t):
                x = tiles[v]
                bj = _index_bit(j, lane, sub, v)
                if j < lane_bits:
                    d = 1 << j
                    p = jnp.where(bj == 1, pltpu.roll(x, d, 1), pltpu.roll(x, LANES - d, 1))
                elif j < tile_bits:
                    m = 1 << (j - lane_bits)
                    p = jnp.where(bj == 1, pltpu.roll(x, m, 0), pltpu.roll(x, SUBLANES - m, 0))
                else:
                    p = tiles[v ^ (1 << (j - tile_bits))]
                bk = _index_bit(k, lane, sub, v) if k < log_n else 0
                take_min = bj == bk
                lo, hi = jnp.minimum(x, p), jnp.maximum(x, p)
                if isinstance(take_min, bool):
                    new.append(lo if take_min else hi)
                else:
                    new.append(jnp.where(take_min, lo, hi))
            tiles = new
    return tiles


def _sort_kernel(eid_ref, src_ref):
    n_t = eid_ref.shape[0] // SUBLANES
    n_bits = (eid_ref.shape[0] * LANES).bit_length() - 1
    lane = lax.broadcasted_iota(jnp.int32, (SUBLANES, LANES), 1)
    sub = lax.broadcasted_iota(jnp.int32, (SUBLANES, LANES), 0)
    tiles = []
    for v in range(n_t):
        n = (v * SUBLANES + sub) * LANES + lane
        tiles.append((eid_ref[v * SUBLANES:(v + 1) * SUBLANES, :] << n_bits) | n)
    tiles = _bitonic_sort(tiles)
    for v in range(n_t):
        src_ref[v * SUBLANES:(v + 1) * SUBLANES, :] = (tiles[v] & ((1 << n_bits) - 1)) * ROW_TILE


def _sort_assignments(eid_rows):
    n = eid_rows.shape[0] * eid_rows.shape[1]
    assert n & (n - 1) == 0 and (N_EXPERTS * n) < 2 ** 31, "keys expert*n + assignment must fit an int32"
    return pl.pallas_call(
        _sort_kernel,
        out_shape=jax.ShapeDtypeStruct(eid_rows.shape, jnp.int32),
        name="dispatch_sort",
    )(eid_rows)


def _tables_kernel(cnt_ref, blk_e_ref, nxt_ref, base_ref, nvalid_ref, nact_ref, *, bm):
    n_blocks = blk_e_ref.shape[0]
    shift = bm.bit_length() - 1

    def per_expert(ex, carry):
        blk0, start = carry
        cnt = cnt_ref[ex]
        end_blk = blk0 + ((cnt + (bm - 1)) >> shift)

        def fill_blk(b, c):
            done = (b - blk0) << shift
            blk_e_ref[b] = ex
            nxt_ref[b] = end_blk
            base_ref[b] = start + done
            nvalid_ref[b] = jnp.minimum(cnt - done, bm)
            return c
        lax.fori_loop(blk0, end_blk, fill_blk, 0)
        return end_blk, start + cnt
    nact, _ = lax.fori_loop(0, N_EXPERTS, per_expert, (0, 0))
    nact_ref[0] = nact

    def fill_idle(b, c):
        blk_e_ref[b] = N_EXPERTS - 1
        nxt_ref[b] = n_blocks
        base_ref[b] = 0
        nvalid_ref[b] = 0
        return c
    lax.fori_loop(nact, n_blocks, fill_idle, 0)


def _tables(counts, n_blocks, bm):
    smem = pl.BlockSpec(memory_space=pltpu.SMEM)
    blocks = jax.ShapeDtypeStruct((n_blocks,), jnp.int32)
    return pl.pallas_call(
        functools.partial(_tables_kernel, bm=bm),
        out_shape=[blocks, blocks, blocks, blocks, jax.ShapeDtypeStruct((1,), jnp.int32)],
        in_specs=[smem],
        out_specs=[smem] * 5,
        name="dispatch_tables",
    )(counts)


def _moe_kernel(blk_e_ref, nxt_ref, base_ref, nvalid_ref, nact_ref, src_ref,
                w1_hbm, w3_hbm, w2_hbm, h_hbm, y_hbm,
                wf1, wf3, wf2, w1b, w3b, w2b, xbuf, ybuf, gsem, ssem, wsem, *, n_tok):
    i = pl.program_id(0)
    nact = nact_ref[0]
    n_blocks = blk_e_ref.shape[0]
    bm = xbuf.shape[1] // ROW_TILE
    ring = xbuf.shape[0]
    de = w1b.shape[1]
    chunk = 2 * LANES
    n_chunks = de // chunk
    rows_per_chunk = 2 * bm // n_chunks

    def tile(row):
        return pl.ds(pl.multiple_of(row * ROW_TILE, ROW_TILE), ROW_TILE)

    def tiles_at(offset):
        return pl.ds(pl.multiple_of(offset, ROW_TILE), ROW_TILE)

    def in_offset(out_offset):
        span = ROW_TILE * n_tok
        if span & (span - 1) == 0:
            return out_offset & (span - 1)
        return lax.rem(out_offset, span)

    def gather_start(blk, sl, rows):
        first = base_ref[blk]
        for r in rows:
            src = tiles_at(in_offset(src_ref[first + r]))
            pltpu.make_async_copy(h_hbm.at[src, :], xbuf.at[sl, tile(r), :], gsem.at[sl]).start()

    def scatter_start(blk, sl, rows, real, priorities):
        first = base_ref[blk]
        valid = jnp.where(real, nvalid_ref[blk], 0)
        spare0 = spare_row(jnp.where(real, lax.rem(blk, ring), ring - 1), 0) * ROW_TILE
        for r in rows:
            dst = tiles_at(jnp.where(r < valid, src_ref[first + r], spare0 + r * ROW_TILE))
            pltpu.make_async_copy(ybuf.at[sl, tile(r), :], y_hbm.at[dst, :],
                                  ssem.at[sl]).start(priority=r % priorities)

    def gather_wait(sl):
        pltpu.make_async_copy(h_hbm.at[pl.ds(0, bm * ROW_TILE), :], xbuf.at[sl], gsem.at[sl]).wait()

    def scatter_wait(sl):
        pltpu.make_async_copy(ybuf.at[sl], y_hbm.at[pl.ds(0, bm * ROW_TILE), :], ssem.at[sl]).wait()

    def spare_row(set_id, r):
        return TOP_K * n_tok + set_id * bm + r

    @pl.when(i < nact)
    def _():
        @pl.when(i == 0)
        def _():
            for blk in range(ring - 1):
                gather_start(min(blk, n_blocks - 1), blk, range(bm))
            ybuf[...] = jnp.zeros_like(ybuf)
            for sl in range(ring):
                spare = pltpu.make_async_copy(
                    ybuf.at[sl], y_hbm.at[pl.ds(spare_row(sl, 0) * ROW_TILE, bm * ROW_TILE), :], ssem.at[sl])
                spare.start()
                spare.wait()

        def weight_copies(e):
            return [pltpu.make_async_copy(w_hbm.at[e], stage, wsem.at[k])
                    for k, (w_hbm, stage) in enumerate(((w1_hbm, wf1), (w3_hbm, wf3), (w2_hbm, wf2)))]

        this_e = blk_e_ref[i]

        @pl.when(i == 0)
        def _():
            for cp in weight_copies(this_e):
                cp.start()

        @pl.when((i == 0) | (this_e != blk_e_ref[jnp.maximum(i - 1, 0)]))
        def _():
            for cp in weight_copies(this_e):
                cp.wait()
            for stage, wb in ((wf1, w1b), (wf3, w3b), (wf2, w2b)):
                def cast_rows(rb, carry, stage=stage, wb=wb):
                    rows = pl.ds(pl.multiple_of(rb * LANES, LANES), LANES)
                    wb[rows, :] = stage[rows, :].astype(BF16)
                    return carry
                lax.fori_loop(0, stage.shape[0] // LANES, cast_rows, 0)
            nxt_blk = nxt_ref[i]

            @pl.when(nxt_blk < nact)
            def _():
                for cp in weight_copies(blk_e_ref[nxt_blk]):
                    cp.start(priority=1)

        ahead = jnp.minimum(i + ring - 1, n_blocks - 1)
        prv = jnp.maximum(i - 1, 0)

        def compute_block(slot):
            far_slot = (slot - 1) % ring
            others = [(slot - 1 - k) % ring for k in range(ring - 1)]
            gather_wait(slot)
            xb = _load_row_tiles(xbuf.at[slot], bm).astype(BF16)
            acts = []
            for c in range(n_chunks):
                cols = slice(c * chunk, (c + 1) * chunk)
                a = _dot(xb, w1b[:, cols])
                g = _dot(xb, w3b[:, cols])
                acts.append((_silu(a) * g).astype(BF16))
                if c < n_chunks // 2:
                    gather_start(ahead, far_slot, range(c * rows_per_chunk, (c + 1) * rows_per_chunk))
                else:
                    first = (c - n_chunks // 2) * rows_per_chunk
                    scatter_start(prv, far_slot, range(first, first + rows_per_chunk), i > 0, 2)

            y = _dot(jnp.concatenate(acts, axis=1), w2b[...])

            @pl.when(i >= ring - 1)
            def _():
                scatter_wait(slot)

            _store_row_tiles(ybuf.at[slot], y)

            @pl.when(i == nact - 1)
            def _():
                scatter_start(i, slot, range(bm), True, 1)
                for k, sl in enumerate(others):
                    pl.when(i >= k)(functools.partial(scatter_wait, sl))
                    gather_wait(sl)
                scatter_wait(slot)

        for parity in range(ring):
            pl.when(lax.rem(i, ring) == parity)(functools.partial(compute_block, parity))


def _moe(blk_e, nxt_blk, base, nvalid, nact, src, w1, w3, w2, h2_tiles, n_blocks):
    t = h2_tiles.shape[0] // ROW_TILE
    d = w1.shape[1]
    de = w1.shape[2]
    bm = MOE_BLOCK
    hbm = pl.BlockSpec(memory_space=pl.ANY)
    grid_spec = pltpu.PrefetchScalarGridSpec(
        num_scalar_prefetch=6,
        grid=(n_blocks,),
        in_specs=[hbm, hbm, hbm, hbm],
        out_specs=hbm,
        scratch_shapes=[pltpu.VMEM((d, de), F32), pltpu.VMEM((d, de), F32), pltpu.VMEM((de, d), F32),
                        pltpu.VMEM((d, de), BF16), pltpu.VMEM((d, de), BF16), pltpu.VMEM((de, d), BF16),
                        pltpu.VMEM((MOE_RING, bm * ROW_TILE, LANES), F32),
                        pltpu.VMEM((MOE_RING, bm * ROW_TILE, LANES), F32),
                        pltpu.SemaphoreType.DMA((MOE_RING,)), pltpu.SemaphoreType.DMA((MOE_RING,)),
                        pltpu.SemaphoreType.DMA((3,))],
    )
    return pl.pallas_call(
        functools.partial(_moe_kernel, n_tok=t),
        out_shape=jax.ShapeDtypeStruct(((TOP_K * t + MOE_RING * bm) * ROW_TILE, LANES), F32),
        grid_spec=grid_spec,
        compiler_params=_cparams(("arbitrary",)),
        name="moe_experts",
    )(blk_e, nxt_blk, base, nvalid, nact, src, w1, w3, w2, h2_tiles)


def _final_kernel(x1_ref, y0_ref, y1_ref, gt_ref, g2_ref, fg_ref, o_ref):
    gt = gt_ref[...]
    tm = x1_ref.shape[0]
    y2 = gt[:, 0:1] * _load_row_tiles(y0_ref, tm) + gt[:, 1:2] * _load_row_tiles(y1_ref, tm)
    x = x1_ref[...] + g2_ref[0] * y2
    o_ref[...] = x * lax.rsqrt(jnp.mean(x * x, axis=-1, keepdims=True) + EPS) * fg_ref[...]


def _final(x1, y, gates_tok, g2, fg, rows_per_mod, tm):
    t, d = x1.shape
    nb = t // tm
    blocks_per_mod = rows_per_mod // tm
    return pl.pallas_call(
        _final_kernel,
        out_shape=jax.ShapeDtypeStruct((t, d), F32),
        grid=(nb,),
        in_specs=[pl.BlockSpec((tm, d), lambda i: (i, 0)),
                  pl.BlockSpec((tm * ROW_TILE, LANES), lambda i: (i, 0)),
                  pl.BlockSpec((tm * ROW_TILE, LANES), lambda i: (i + nb, 0)),
                  pl.BlockSpec((tm, TOP_K), lambda i: (i, 0)),
                  pl.BlockSpec((1, 1, d), lambda i: (i // blocks_per_mod, 0, 0)),
                  pl.BlockSpec((1, d), lambda i: (0, 0))],
        out_specs=pl.BlockSpec((tm, d), lambda i: (i, 0)),
        compiler_params=_cparams(("arbitrary",)),
        name="combine_final_norm",
    )(x1, y, y, gates_tok, g2, fg)


def kernel(x, c, ctx, c_ctx, w_ada, b_ada, norm1_g, norm2_g, w_in, conv_dw, conv_b, conv_ln_g, conv_ln_b,
           lru_conv_w, lru_conv_b, lru_wa, lru_ba, lru_wx, lru_bx, lru_lam, w_out,
           router_wg, router_bg, router_we, router_be, w1, w3, w2, final_g):
    assert w_ada.shape[0] == 1, "single-layer block"
    assert x.shape[2] == ROW_TILE * LANES, "row-tile layout: one (ROW_TILE, LANES) tile per token row"
    b, s, d = x.shape
    n_ctx = ctx.shape[1]
    t = b * s
    cc = conv_dw.shape[2]
    lw = lru_conv_w.shape[2]

    c_rows = jnp.zeros((SUBLANES, d), F32).at[:b].set(c).at[b].set(c_ctx)
    mod = _ada(c_rows, w_ada[0], b_ada)
    mod_l = mod[:b].reshape(b, 6, 1, d)
    sh1, sc1, g1, sh2, sc2, g2 = (mod_l[:, k] for k in range(6))
    mod_c = mod[b].reshape(6, 1, 1, d)
    csh1, csc1 = mod_c[0], mod_c[1]

    w_in_b = w_in[0].astype(BF16)
    w_out_b = w_out[0].astype(BF16)
    heads_per_blk = LRU_LANES // lru_wa.shape[3]
    n_cblk = lw // LRU_LANES

    def blockdiag(wh):
        hd = wh.shape[1]
        wh = wh.reshape(n_cblk, heads_per_blk, hd, hd)
        eye = jnp.eye(heads_per_blk, dtype=wh.dtype)
        return jnp.einsum("chij,hg->chigj", wh, eye).reshape(n_cblk, LRU_LANES, LRU_LANES)

    wg = (0.5 * jnp.concatenate([blockdiag(lru_wa[0, 0]), blockdiag(lru_wx[0, 0]),
                                 blockdiag(lru_wa[0, 1]), blockdiag(lru_wx[0, 1])], axis=2)).astype(BF16)
    gb = 0.5 * jnp.stack([lru_ba[0, 0], lru_bx[0, 0], lru_ba[0, 1], lru_bx[0, 1]])
    lam = lru_lam[0]

    zc = _inproj_ctx(ctx.reshape(b * n_ctx, d), csh1, csc1, norm1_g, w_in_b[:, 2 * cc:2 * cc + lw], n_ctx)
    h0 = _lru(zc.reshape(b, n_ctx, lw), None, jnp.zeros((2, b, 1, lw), F32),
              lru_conv_w[0], lru_conv_b, wg, gb, lam, True, "rglru_ctx")

    x2 = x.reshape(t, d)
    conv_w_rep = jnp.repeat(conv_dw[0], SUBLANES, axis=0).reshape(CONV_TAPS * SUBLANES, cc // LANES, LANES)
    conv_w_rep = conv_w_rep.transpose(1, 0, 2)
    conv_l, zl, gg = _inproj(x2, sh1, sc1, norm1_g, w_in_b, conv_w_rep, conv_b, conv_ln_g, conv_ln_b, s)
    lru_l = _lru(zl.reshape(b, s, lw), gg.reshape(b, s, lw), h0,
                 lru_conv_w[0], lru_conv_b, wg, gb, lam, False, "rglru")

    wr = jnp.zeros((ROUTE_ROWS, d), F32)
    wr = wr.at[:N_GROUPS].set(router_wg[0].T)
    wr = wr.at[EXPERT_ROW0:EXPERT_ROW0 + N_EXPERTS].set(router_we[0].reshape(d, N_EXPERTS).T)
    rb = jnp.zeros((ROUTE_ROWS, 1), F32)
    rb = rb.at[:N_GROUPS, 0].set(router_bg[0])
    rb = rb.at[EXPERT_ROW0:EXPERT_ROW0 + N_EXPERTS, 0].set(router_be[0].reshape(-1))
    r_hi, r_lo = _split_bf16(wr)
    x1, h2, logits_t = _outproj(conv_l, lru_l.reshape(t, lw), x2, g1, sh2, sc2, norm2_g, w_out_b,
                                r_hi, jnp.concatenate([r_hi, r_lo], axis=0), rb, s, TOKEN_BLOCK)

    gates, eid, counts = _route(logits_t)
    n_blocks = (TOP_K * t) // MOE_BLOCK + N_EXPERTS
    src = jnp.pad(_sort_assignments(eid.reshape(TOP_K * t // LANES, LANES)).reshape(-1), (0, MOE_BLOCK))
    blk_e, nxt_blk, base, nvalid, nact = _tables(counts[:, 0].astype(jnp.int32), n_blocks, MOE_BLOCK)
    y = _moe(blk_e, nxt_blk, base, nvalid, nact, src, w1[0], w3[0], w2[0], h2, n_blocks)
    out = _final(x1, y, gates.T, g2, final_g.reshape(1, d), s, TOKEN_BLOCK)
    return out.reshape(b, s, d)
```

```python
import functools

import jax
import jax.numpy as jnp
from jax import lax
from jax.experimental import pallas as pl
from jax.experimental.pallas import tpu as pltpu

F32 = jnp.float32
BF16 = jnp.bfloat16

EPS = 1e-6
CONV_TAPS = 31
LRU_TAPS = 4
LRU_C = 8.0
GRID_W = 64
N_GROUPS = 4
EXPERTS_PER_GROUP = 8
N_EXPERTS = N_GROUPS * EXPERTS_PER_GROUP
TOP_K = 2

SUBLANES = 8
LANES = 128
TOKEN_BLOCK = 1024
LRU_LANES = 128
EXPERT_ROW0 = 8
ROUTE_ROWS = EXPERT_ROW0 + N_EXPERTS
MOE_BLOCK = 256
MOE_RING = 3
STREAM_RING = 3
ROUTE_LANES = 2048
ADA_COLS = 1536
V7X_VMEM_BYTES = 64 * 1024 * 1024
VMEM_LIMIT = 3 * V7X_VMEM_BYTES // 4


def _cparams(sem, vmem=VMEM_LIMIT):
    return pltpu.CompilerParams(dimension_semantics=sem, vmem_limit_bytes=vmem)


def _split_bf16(a):
    hi = a.astype(BF16)
    lo = (a - hi.astype(F32)).astype(BF16)
    return hi, lo


def _dot(a, b):
    return jnp.dot(a, b, preferred_element_type=F32)


def _dot_nt(a, b):
    return lax.dot_general(a, b, (((1,), (1,)), ((), ())), preferred_element_type=F32)


def _times_sigmoid(v, x):
    hv = 0.5 * v
    return hv * jnp.tanh(0.5 * x) + hv


def _silu(x):
    h = 0.5 * x
    return h * jnp.tanh(h) + h


def _gelu_tanh(x):
    c = 0.7978845608028654
    h = 0.5 * x
    return h * jnp.tanh(x * ((c * 0.044715) * (x * x) + c)) + h


ROW_TILE = SUBLANES


def _store_row_tiles(ref, x):
    rows = x.shape[0]
    for s in range(ROW_TILE):
        ref[pl.ds(s, rows, stride=ROW_TILE), :] = x[:, s * LANES:(s + 1) * LANES]


def _load_row_tiles(ref, rows):
    return jnp.concatenate([ref[pl.ds(s, rows, stride=ROW_TILE), :] for s in range(ROW_TILE)], axis=1)


def _rms_mod(x, g, shift, scale):
    y = x * lax.rsqrt(jnp.mean(x * x, axis=-1, keepdims=True) + EPS)
    return y * (g * (1.0 + scale)) + shift


def _ada_kernel(c_ref, w_ref, b_ref, o_ref):
    a = _silu(c_ref[...])
    a_hi, a_lo = _split_bf16(a)
    w_hi, w_lo = _split_bf16(w_ref[...])
    o_ref[...] = _dot(a_hi, w_hi) + _dot(a_lo, w_hi) + _dot(a_hi, w_lo) + b_ref[...]


def _ada(c_rows, w, b):
    m, d = c_rows.shape
    n = w.shape[1]
    bn = ADA_COLS
    return pl.pallas_call(
        _ada_kernel,
        out_shape=jax.ShapeDtypeStruct((m, n), F32),
        grid=(n // bn,),
        in_specs=[pl.BlockSpec((m, d), lambda j: (0, 0)),
                  pl.BlockSpec((d, bn), lambda j: (0, j)),
                  pl.BlockSpec((1, bn), lambda j: (0, j))],
        out_specs=pl.BlockSpec((m, bn), lambda j: (0, j)),
        compiler_params=_cparams(("arbitrary",)),
        name="ada_mod",
    )(c_rows, w, b)


def _inproj_ctx_kernel(x_ref, sh_ref, sc_ref, g_ref, w_ref, o_ref):
    h = _rms_mod(x_ref[...], g_ref[...], sh_ref[0], sc_ref[0])
    o_ref[...] = _dot(h.astype(BF16), w_ref[...])


def _conformer_conv(u, cw_ref, cb_ref, lg_ref, lb_ref, o_ref, stg_ref, xt_ref, ot_ref):
    c = u.shape[1]
    slabs = range(c // LANES)
    half = CONV_TAPS // 2
    pitch = GRID_W + SUBLANES
    group = SUBLANES
    for l in slabs:
        for q in range(SUBLANES):
            stg_ref[l, q * pitch:q * pitch + GRID_W, :] = u[q * GRID_W:(q + 1) * GRID_W, l * LANES:(l + 1) * LANES]

    def conv_slab(l, carry):
        for t in range(GRID_W):
            xt_ref[l, t * SUBLANES:(t + 1) * SUBLANES, :] = stg_ref[l, pl.ds(t, SUBLANES, stride=pitch), :]
        for t0 in range(0, GRID_W, group):
            accs = [None] * group
            for k in range(CONV_TAPS):
                srcs = [t0 + j + k - half for j in range(group)]
                if not any(0 <= sidx < GRID_W for sidx in srcs):
                    continue
                wk = cw_ref[l, k * SUBLANES:(k + 1) * SUBLANES, :]
                for j, sidx in enumerate(srcs):
                    if 0 <= sidx < GRID_W:
                        term = wk * xt_ref[l, sidx * SUBLANES:(sidx + 1) * SUBLANES, :]
                        accs[j] = term if accs[j] is None else accs[j] + term
            for j in range(group):
                ot_ref[l, (t0 + j) * SUBLANES:(t0 + j + 1) * SUBLANES, :] = accs[j]
        return carry
    lax.fori_loop(0, c // LANES, conv_slab, 0)
    rows_per_pass = group * SUBLANES
    for r0 in range(0, GRID_W * SUBLANES, rows_per_pass):
        rows = slice(r0, r0 + rows_per_pass)
        acc = jnp.concatenate([ot_ref[l, rows, :] for l in slabs], axis=1) + cb_ref[...]
        mu = jnp.mean(acc, axis=-1, keepdims=True)
        cen = acc - mu
        var = jnp.mean(cen * cen, axis=-1, keepdims=True)
        y = _silu(cen * lax.rsqrt(var + EPS) * lg_ref[...] + lb_ref[...])
        for l in slabs:
            ot_ref[l, rows, :] = y[:, l * LANES:(l + 1) * LANES]
    for q in range(SUBLANES):
        o_ref[q * GRID_W:(q + 1) * GRID_W, :] = jnp.concatenate(
            [ot_ref[l, pl.ds(q, GRID_W, stride=SUBLANES), :] for l in slabs], axis=1).astype(o_ref.dtype)


def _inproj_kernel(x_ref, sh_ref, sc_ref, g_ref, w_ref, cw_ref, cb_ref, lg_ref, lb_ref,
                   cv_ref, zl_ref, gg_ref, stg_ref, xt_ref, ot_ref):
    h = _rms_mod(x_ref[...], g_ref[...], sh_ref[0], sc_ref[0])
    z = _dot(h.astype(BF16), w_ref[...])
    c = cv_ref.shape[1]
    zl_ref[...] = z[:, 2 * c:3 * c]
    gg_ref[...] = _gelu_tanh(z[:, 3 * c:])
    u = _times_sigmoid(z[:, :c], z[:, c:2 * c])
    _conformer_conv(u, cw_ref, cb_ref, lg_ref, lb_ref, cv_ref, stg_ref, xt_ref, ot_ref)


def _inproj_ctx(x2, shift, scale, g, w, tm):
    t, d = x2.shape
    n = w.shape[1]
    mod_spec = pl.BlockSpec((1, 1, d), lambda i: (0, 0, 0))
    return pl.pallas_call(
        _inproj_ctx_kernel,
        out_shape=jax.ShapeDtypeStruct((t, n), F32),
        grid=(t // tm,),
        in_specs=[pl.BlockSpec((tm, d), lambda i: (i, 0)), mod_spec, mod_spec,
                  pl.BlockSpec((1, d), lambda i: (0, 0)),
                  pl.BlockSpec((d, n), lambda i: (0, 0))],
        out_specs=pl.BlockSpec((tm, n), lambda i: (i, 0)),
        compiler_params=_cparams(("arbitrary",)),
        name="in_proj_ctx",
    )(x2, shift, scale, g, w)


def _inproj(x2, shift, scale, g, w, cw, cb, lg, lb, rows_per_mod):
    t, d = x2.shape
    n = w.shape[1]
    c = n // 4
    tm = SUBLANES * GRID_W
    blocks_per_mod = rows_per_mod // tm
    mod_spec = pl.BlockSpec((1, 1, d), lambda i: (i // blocks_per_mod, 0, 0))
    const = lambda a: pl.BlockSpec(a.shape, lambda i: (0,) * a.ndim)
    rows = pl.BlockSpec((tm, c), lambda i: (i, 0))
    return pl.pallas_call(
        _inproj_kernel,
        out_shape=[jax.ShapeDtypeStruct((t, c), BF16), jax.ShapeDtypeStruct((t, c), F32),
                   jax.ShapeDtypeStruct((t, c), F32)],
        grid=(t // tm,),
        in_specs=[pl.BlockSpec((tm, d), lambda i: (i, 0)), mod_spec, mod_spec, const(g), const(w),
                  const(cw), const(cb), const(lg), const(lb)],
        out_specs=[rows, rows, rows],
        scratch_shapes=[pltpu.VMEM((c // LANES, SUBLANES * (GRID_W + SUBLANES), LANES), F32),
                        pltpu.VMEM((c // LANES, tm, LANES), F32), pltpu.VMEM((c // LANES, tm, LANES), F32)],
        compiler_params=_cparams(("arbitrary",)),
        name="in_proj",
    )(x2, shift, scale, g, w, cw, cb, lg, lb)


def _lru_kernel(zl_ref, h0_ref, cw_ref, cb_ref, wg_ref, gb_ref, lam_ref, *rest, seq, final_only):
    if final_only:
        o_ref, stage_ref, zt_ref, af_ref, bf_ref, ab_ref, bb_ref, pf_ref, hf_ref, pb_ref, hb_ref = rest
        gg_ref = None
    else:
        gg_ref, o_ref, stage_ref, zt_ref, af_ref, bf_ref, ab_ref, bb_ref, pf_ref, hf_ref, pb_ref, hb_ref = rest
    cl = seq // SUBLANES
    pitch = cl + SUBLANES
    c = zl_ref.shape[2]
    a_refs = (af_ref, ab_ref)
    b_refs = (bf_ref, bb_ref)
    halo = 2

    def tile_rows(t):
        return pl.ds(pl.multiple_of(t * SUBLANES, SUBLANES), SUBLANES)

    for j in range(SUBLANES):
        stage_ref[j * pitch:j * pitch + cl, :] = zl_ref[0, j * cl:(j + 1) * cl, :]

    def to_chunk_layout(t, carry):
        zt_ref[tile_rows(t + halo), :] = stage_ref[pl.ds(t, SUBLANES, stride=pitch), :]
        return carry
    lax.fori_loop(0, cl, to_chunk_layout, 0, unroll=8)

    sub = lax.broadcasted_iota(jnp.int32, (SUBLANES, c), 0)
    for t_src, t_dst in ((cl - 2, -2), (cl - 1, -1)):
        v = pltpu.roll(zt_ref[(t_src + halo) * SUBLANES:(t_src + halo + 1) * SUBLANES, :], 1, 0)
        zt_ref[(t_dst + halo) * SUBLANES:(t_dst + halo + 1) * SUBLANES, :] = jnp.where(sub == 0, 0.0, v)
    v = pltpu.roll(zt_ref[halo * SUBLANES:(halo + 1) * SUBLANES, :], SUBLANES - 1, 0)
    zt_ref[(cl + halo) * SUBLANES:(cl + halo + 1) * SUBLANES, :] = jnp.where(sub == SUBLANES - 1, 0.0, v)

    lam = lam_ref[...]
    nlam = -lam
    softplus = jnp.maximum(nlam, 0.0) + jnp.log1p(jnp.exp(-jnp.abs(nlam)))
    half_decay = (-0.5 * LRU_C) * softplus
    wg = wg_ref[0]
    cw_half = 0.5 * cw_ref[...]
    cb_half = 0.5 * cb_ref[...]

    piece = cl
    for p0 in range(0, cl * SUBLANES, piece):
        ul_half = cb_half + jnp.zeros((piece, c), F32)
        for k in range(LRU_TAPS):
            off = p0 + k * SUBLANES
            ul_half = ul_half + cw_half[k:k + 1, :] * zt_ref[off:off + piece, :]
        g = _dot(ul_half.astype(BF16), wg)
        for d in range(2):
            t_r = jnp.tanh(g[:, (2 * d) * c:(2 * d + 1) * c] + gb_ref[2 * d:2 * d + 1, :])
            t_i = jnp.tanh(g[:, (2 * d + 1) * c:(2 * d + 2) * c] + gb_ref[2 * d + 1:2 * d + 2, :])
            log_a = half_decay[d:d + 1, :] * t_r + half_decay[d:d + 1, :]
            a = jnp.exp(log_a)
            m = jnp.maximum(jnp.tanh(log_a) * (-1.0 - a * a), 1e-12)
            mult = m * lax.rsqrt(m)
            a_refs[d][p0:p0 + piece, :] = a
            b_refs[d][p0:p0 + piece, :] = mult * (t_i * ul_half + ul_half)

    def two_steps(a_ref, b_ref, p_out, h_out, s0, s1, h, p):
        a0 = a_ref[s0, :]
        a1 = a_ref[s1, :]
        b0 = b_ref[s0, :]
        a10 = a1 * a0
        b10 = a1 * b0 + b_ref[s1, :]
        h_out[s0, :] = a0 * h + b0
        p_out[s0, :] = a0 * p
        h = a10 * h + b10
        p = a10 * p
        h_out[s1, :] = h
        p_out[s1, :] = p
        return h, p

    group = 8
    group_rows = group * SUBLANES

    def step(n, carry):
        hf, pf, hb, pb = carry
        base_f = pl.multiple_of(n * group_rows, group_rows)
        base_b = pl.multiple_of((cl // group - 1 - n) * group_rows, group_rows)
        for k in range(0, group, 2):
            tf0 = pl.ds(base_f + k * SUBLANES, SUBLANES)
            tf1 = pl.ds(base_f + (k + 1) * SUBLANES, SUBLANES)
            hf, pf = two_steps(af_ref, bf_ref, pf_ref, hf_ref, tf0, tf1, hf, pf)
            tb0 = pl.ds(base_b + (group - 1 - k) * SUBLANES, SUBLANES)
            tb1 = pl.ds(base_b + (group - 2 - k) * SUBLANES, SUBLANES)
            hb, pb = two_steps(ab_ref, bb_ref, pb_ref, hb_ref, tb0, tb1, hb, pb)
        return hf, pf, hb, pb

    zero = jnp.zeros((SUBLANES, c), F32)
    one = jnp.ones((SUBLANES, c), F32)
    lax.fori_loop(0, cl // group, step, (zero, one, zero, one))

    cf = [None] * SUBLANES
    cbk = [None] * SUBLANES
    s = h0_ref[0, 0]
    for j in range(SUBLANES):
        cf[j] = s
        last = (cl - 1) * SUBLANES + j
        s = hf_ref[last:last + 1, :] + pf_ref[last:last + 1, :] * s
    final_f = s
    s = h0_ref[1, 0]
    for j in reversed(range(SUBLANES)):
        cbk[j] = s
        s = hb_ref[j:j + 1, :] + pb_ref[j:j + 1, :] * s
    final_b = s

    if final_only:
        o_ref[0, 0] = final_f
        o_ref[1, 0] = final_b
        return
    carry_f = jnp.concatenate(cf, axis=0)
    carry_b = jnp.concatenate(cbk, axis=0)

    def to_natural(t, carry):
        rows = tile_rows(t)
        h = (hf_ref[rows, :] + pf_ref[rows, :] * carry_f) + (hb_ref[rows, :] + pb_ref[rows, :] * carry_b)
        stage_ref[pl.ds(t, SUBLANES, stride=pitch), :] = h
        return carry
    lax.fori_loop(0, cl, to_natural, 0, unroll=8)
    for j in range(SUBLANES):
        h = stage_ref[j * pitch:j * pitch + cl, :]
        o_ref[0, j * cl:(j + 1) * cl, :] = (h * gg_ref[0, j * cl:(j + 1) * cl, :]).astype(o_ref.dtype)


def _lru(zl, gg, h0, cw, cb, wg, gb, lam, final_only, name):
    b, seq, c = zl.shape
    cbk = LRU_LANES
    cl = seq // SUBLANES
    pitch = cl + SUBLANES
    seq_spec = pl.BlockSpec((1, seq, cbk), lambda bi, ci: (bi, 0, ci))
    st_spec = pl.BlockSpec((2, 1, 1, cbk), lambda bi, ci: (0, bi, 0, ci))
    chan = lambda rows: pl.BlockSpec((rows, cbk), lambda bi, ci: (0, ci))
    in_specs = [seq_spec, st_spec, chan(LRU_TAPS), chan(1),
                pl.BlockSpec((1, cbk, 4 * cbk), lambda bi, ci: (ci, 0, 0)), chan(4), chan(2)]
    args = [zl, h0, cw, cb, wg, gb, lam]
    if final_only:
        out_shape = jax.ShapeDtypeStruct((2, b, 1, c), F32)
        out_spec = st_spec
    else:
        in_specs.append(seq_spec)
        args.append(gg)
        out_shape = jax.ShapeDtypeStruct((b, seq, c), BF16)
        out_spec = seq_spec
    coef = pltpu.VMEM((seq, cbk), F32)
    return pl.pallas_call(
        functools.partial(_lru_kernel, seq=seq, final_only=final_only),
        out_shape=out_shape,
        grid=(b, c // cbk),
        in_specs=in_specs,
        out_specs=out_spec,
        scratch_shapes=[pltpu.VMEM((SUBLANES * pitch, cbk), F32),
                        pltpu.VMEM((seq + 2 * SUBLANES * SUBLANES, cbk), F32)] + [coef] * 8,
        compiler_params=_cparams(("arbitrary", "arbitrary")),
        name=name,
    )(*args)


def _stream_rows(streams, sem):
    i = pl.program_id(0)
    n = pl.num_programs(0)

    def copies(step):
        slot = lax.rem(step, STREAM_RING)
        out = []
        for k, (src, buf) in enumerate(streams):
            rows = buf.shape[1]
            src_rows = pl.ds(pl.multiple_of(step * rows, rows), rows)
            out.append(pltpu.make_async_copy(src.at[src_rows], buf.at[slot], sem.at[k, slot]))
        return out

    @pl.when(i == 0)
    def _():
        for ahead in range(STREAM_RING - 1):
            for cp in copies(i + ahead):
                cp.start()

    @pl.when(i + STREAM_RING - 1 < n)
    def _():
        for cp in copies(i + STREAM_RING - 1):
            cp.start()

    for cp in copies(i):
        cp.wait()
    return lax.rem(i, STREAM_RING)


def _outproj_kernel(cv_hbm, lr_hbm, x_hbm, g1_ref, sh_ref, sc_ref, ng_ref, wo_ref, rh_ref, rc_ref, rb_ref,
                    x1_ref, h2_ref, lg_ref, cv_buf, lr_buf, x_buf, sem):
    c = cv_buf.shape[2]
    slot = _stream_rows(((cv_hbm, cv_buf), (lr_hbm, lr_buf), (x_hbm, x_buf)), sem)
    y = _dot(cv_buf[slot], wo_ref[0:c, :]) + _dot(lr_buf[slot], wo_ref[c:2 * c, :])
    x1 = x_buf[slot] + g1_ref[0] * y
    x1_ref[...] = x1
    h2 = _rms_mod(x1, ng_ref[...], sh_ref[0], sc_ref[0])
    h_hi, h_lo = _split_bf16(h2)
    _store_row_tiles(h2_ref, h2)
    rows = rh_ref.shape[0]
    both = _dot_nt(rc_ref[...], h_hi)
    lg_ref[...] = both[:rows] + both[rows:] + _dot_nt(rh_ref[...], h_lo) + rb_ref[...]


def _outproj(conv_l, lru_l, x2, g1, sh2, sc2, ng, wo, r_hi, r_both, r_b, rows_per_mod, tm):
    t, d = x2.shape
    c = conv_l.shape[1]
    blocks_per_mod = rows_per_mod // tm
    mod_spec = pl.BlockSpec((1, 1, d), lambda i: (i // blocks_per_mod, 0, 0))
    full = lambda a: pl.BlockSpec(a.shape, lambda i: (0, 0))
    hbm = pl.BlockSpec(memory_space=pl.ANY)
    return pl.pallas_call(
        _outproj_kernel,
        out_shape=[jax.ShapeDtypeStruct((t, d), F32), jax.ShapeDtypeStruct((t * ROW_TILE, LANES), F32),
                   jax.ShapeDtypeStruct((ROUTE_ROWS, t), F32)],
        grid=(t // tm,),
        in_specs=[hbm, hbm, hbm, mod_spec, mod_spec, mod_spec,
                  full(ng), full(wo), full(r_hi), full(r_both), full(r_b)],
        out_specs=[pl.BlockSpec((tm, d), lambda i: (i, 0)), pl.BlockSpec((tm * ROW_TILE, LANES), lambda i: (i, 0)),
                   pl.BlockSpec((ROUTE_ROWS, tm), lambda i: (0, i))],
        scratch_shapes=[pltpu.VMEM((STREAM_RING, tm, c), conv_l.dtype), pltpu.VMEM((STREAM_RING, tm, c), lru_l.dtype),
                        pltpu.VMEM((STREAM_RING, tm, d), F32), pltpu.SemaphoreType.DMA((3, STREAM_RING))],
        compiler_params=_cparams(("arbitrary",)),
        name="out_proj_router",
    )(conv_l, lru_l, x2, g1, sh2, sc2, ng, wo, r_hi, r_both, r_b)


def _route_chunk(lg_ref, lanes):
    n = lanes.size
    e = EXPERTS_PER_GROUP
    lgrp = lg_ref[0:N_GROUPS, lanes]
    gidx = lax.broadcasted_iota(jnp.int32, (N_GROUPS, n), 0)
    m = jnp.max(lgrp, axis=0, keepdims=True)
    ex = jnp.exp(lgrp - m)
    pg = ex / jnp.sum(ex, axis=0, keepdims=True)
    p_grp = jnp.max(pg, axis=0, keepdims=True)
    grp = jnp.min(jnp.where(pg == p_grp, gidx, N_GROUPS), axis=0, keepdims=True)
    le = jnp.zeros((e, n), F32)
    for g in range(N_GROUPS):
        rows = lg_ref[EXPERT_ROW0 + g * e:EXPERT_ROW0 + (g + 1) * e, lanes]
        le = jnp.where(grp == g, rows, le)
    m = jnp.max(le, axis=0, keepdims=True)
    ex = jnp.exp(le - m)
    pe = ex / jnp.sum(ex, axis=0, keepdims=True)
    eidx = lax.broadcasted_iota(jnp.int32, (e, n), 0)
    p1 = jnp.max(pe, axis=0, keepdims=True)
    i1 = jnp.min(jnp.where(pe == p1, eidx, e), axis=0, keepdims=True)
    pe2 = jnp.where(eidx == i1, -1.0, pe)
    p2 = jnp.max(pe2, axis=0, keepdims=True)
    i2 = jnp.min(jnp.where(pe2 == p2, eidx, e), axis=0, keepdims=True)
    denom = p1 + p2
    base = grp * e
    return (base + i1, base + i2), (p_grp * p1 / denom, p_grp * p2 / denom)


def _route_kernel(lg_ref, gate_ref, eid_ref, cnt_ref):
    tl = lg_ref.shape[1]
    chunk = 2 * LANES
    ne = N_EXPERTS

    @pl.when(pl.program_id(0) == 0)
    def _():
        cnt_ref[...] = jnp.zeros_like(cnt_ref)

    ones = jnp.ones((chunk, LANES), BF16)
    eidx = lax.broadcasted_iota(jnp.int32, (ne, chunk), 0)
    total = jnp.zeros((ne, LANES), F32)
    for cix in range(tl // chunk):
        lanes = pl.ds(cix * chunk, chunk)
        eids, gates = _route_chunk(lg_ref, lanes)
        for k in range(TOP_K):
            total = total + _dot((eidx == eids[k]).astype(BF16), ones)
        gate_ref[:, lanes] = jnp.concatenate(gates, axis=0)
        eid_ref[:, lanes] = jnp.concatenate(eids, axis=0)
    cnt_ref[...] = cnt_ref[...] + total


def _route(logits_t):
    rows, t = logits_t.shape
    tl = ROUTE_LANES
    blk = pl.BlockSpec((TOP_K, tl), lambda i: (0, i))
    return pl.pallas_call(
        _route_kernel,
        out_shape=[jax.ShapeDtypeStruct((TOP_K, t), F32), jax.ShapeDtypeStruct((TOP_K, t), jnp.int32),
                   jax.ShapeDtypeStruct((N_EXPERTS, LANES), F32)],
        grid=(t // tl,),
        in_specs=[pl.BlockSpec((rows, tl), lambda i: (0, i))],
        out_specs=[blk, blk, pl.BlockSpec((N_EXPERTS, LANES), lambda i: (0, 0))],
        compiler_params=_cparams(("arbitrary",)),
        name="route",
    )(logits_t)


def _index_bit(j, lane, sub, tile):
    lane_bits = LANES.bit_length() - 1
    tile_bits = lane_bits + SUBLANES.bit_length() - 1
    if j < lane_bits:
        return (lane >> j) & 1
    if j < tile_bits:
        return (sub >> (j - lane_bits)) & 1
    return (tile >> (j - tile_bits)) & 1


def _bitonic_sort(tiles):
    n_t = len(tiles)
    log_n = (n_t * SUBLANES * LANES).bit_length() - 1
    lane_bits = LANES.bit_length() - 1
    tile_bits = lane_bits + SUBLANES.bit_length() - 1
    lane = lax.broadcasted_iota(jnp.int32, (SUBLANES, LANES), 1)
    sub = lax.broadcasted_iota(jnp.int32, (SUBLANES, LANES), 0)
    for k in range(1, log_n + 1):
        for j in range(k - 1, -1, -1):
            new = []
            for v in range(n_t):
                x = tiles[v]
                bj = _index_bit(j, lane, sub, v)
                if j < lane_bits:
                    d = 1 << j
                    p = jnp.where(bj == 1, pltpu.roll(x, d, 1), pltpu.roll(x, LANES - d, 1))
                elif j < tile_bits:
                    m = 1 << (j - lane_bits)
                    p = jnp.where(bj == 1, pltpu.roll(x, m, 0), pltpu.roll(x, SUBLANES - m, 0))
                else:
                    p = tiles[v ^ (1 << (j - tile_bits))]
                bk = _index_bit(k, lane, sub, v) if k < log_n else 0
                take_min = bj == bk
                lo, hi = jnp.minimum(x, p), jnp.maximum(x, p)
                if isinstance(take_min, bool):
                    new.append(lo if take_min else hi)
                else:
                    new.append(jnp.where(take_min, lo, hi))
            tiles = new
    return tiles


def _sort_kernel(eid_ref, src_ref):
    n_t = eid_ref.shape[0] // SUBLANES
    n_bits = (eid_ref.shape[0] * LANES).bit_length() - 1
    lane = lax.broadcasted_iota(jnp.int32, (SUBLANES, LANES), 1)
    sub = lax.broadcasted_iota(jnp.int32, (SUBLANES, LANES), 0)
    tiles = []
    for v in range(n_t):
        n = (v * SUBLANES + sub) * LANES + lane
        tiles.append((eid_ref[v * SUBLANES:(v + 1) * SUBLANES, :] << n_bits) | n)
    tiles = _bitonic_sort(tiles)
    for v in range(n_t):
        src_ref[v * SUBLANES:(v + 1) * SUBLANES, :] = (tiles[v] & ((1 << n_bits) - 1)) * ROW_TILE


def _sort_assignments(eid_rows):
    n = eid_rows.shape[0] * eid_rows.shape[1]
    assert n & (n - 1) == 0 and (N_EXPERTS * n) < 2 ** 31, "keys expert*n + assignment must fit an int32"
    return pl.pallas_call(
        _sort_kernel,
        out_shape=jax.ShapeDtypeStruct(eid_rows.shape, jnp.int32),
        name="dispatch_sort",
    )(eid_rows)


def _tables_kernel(cnt_ref, blk_e_ref, nxt_ref, base_ref, nvalid_ref, nact_ref, *, bm):
    n_blocks = blk_e_ref.shape[0]
    shift = bm.bit_length() - 1

    def per_expert(ex, carry):
        blk0, start = carry
        cnt = cnt_ref[ex]
        end_blk = blk0 + ((cnt + (bm - 1)) >> shift)

        def fill_blk(b, c):
            done = (b - blk0) << shift
            blk_e_ref[b] = ex
            nxt_ref[b] = end_blk
            base_ref[b] = start + done
            nvalid_ref[b] = jnp.minimum(cnt - done, bm)
            return c
        lax.fori_loop(blk0, end_blk, fill_blk, 0)
        return end_blk, start + cnt
    nact, _ = lax.fori_loop(0, N_EXPERTS, per_expert, (0, 0))
    nact_ref[0] = nact

    def fill_idle(b, c):
        blk_e_ref[b] = N_EXPERTS - 1
        nxt_ref[b] = n_blocks
        base_ref[b] = 0
        nvalid_ref[b] = 0
        return c
    lax.fori_loop(nact, n_blocks, fill_idle, 0)


def _tables(counts, n_blocks, bm):
    smem = pl.BlockSpec(memory_space=pltpu.SMEM)
    blocks = jax.ShapeDtypeStruct((n_blocks,), jnp.int32)
    return pl.pallas_call(
        functools.partial(_tables_kernel, bm=bm),
        out_shape=[blocks, blocks, blocks, blocks, jax.ShapeDtypeStruct((1,), jnp.int32)],
        in_specs=[smem],
        out_specs=[smem] * 5,
        name="dispatch_tables",
    )(counts)


def _moe_kernel(blk_e_ref, nxt_ref, base_ref, nvalid_ref, nact_ref, src_ref,
                w1_hbm, w3_hbm, w2_hbm, h_hbm, y_hbm,
                wf1, wf3, wf2, w1b, w3b, w2b, xbuf, ybuf, gsem, ssem, wsem, *, n_tok):
    i = pl.program_id(0)
    nact = nact_ref[0]
    n_blocks = blk_e_ref.shape[0]
    bm = xbuf.shape[1] // ROW_TILE
    ring = xbuf.shape[0]
    de = w1b.shape[1]
    chunk = 2 * LANES
    n_chunks = de // chunk
    rows_per_chunk = 2 * bm // n_chunks

    def tile(row):
        return pl.ds(pl.multiple_of(row * ROW_TILE, ROW_TILE), ROW_TILE)

    def tiles_at(offset):
        return pl.ds(pl.multiple_of(offset, ROW_TILE), ROW_TILE)

    def in_offset(out_offset):
        span = ROW_TILE * n_tok
        if span & (span - 1) == 0:
            return out_offset & (span - 1)
        return lax.rem(out_offset, span)

    def gather_start(blk, sl, rows):
        first = base_ref[blk]
        for r in rows:
            src = tiles_at(in_offset(src_ref[first + r]))
            pltpu.make_async_copy(h_hbm.at[src, :], xbuf.at[sl, tile(r), :], gsem.at[sl]).start()

    def scatter_start(blk, sl, rows, real, priorities):
        first = base_ref[blk]
        valid = jnp.where(real, nvalid_ref[blk], 0)
        spare0 = spare_row(jnp.where(real, lax.rem(blk, ring), ring - 1), 0) * ROW_TILE
        for r in rows:
            dst = tiles_at(jnp.where(r < valid, src_ref[first + r], spare0 + r * ROW_TILE))
            pltpu.make_async_copy(ybuf.at[sl, tile(r), :], y_hbm.at[dst, :],
                                  ssem.at[sl]).start(priority=r % priorities)

    def gather_wait(sl):
        pltpu.make_async_copy(h_hbm.at[pl.ds(0, bm * ROW_TILE), :], xbuf.at[sl], gsem.at[sl]).wait()

    def scatter_wait(sl):
        pltpu.make_async_copy(ybuf.at[sl], y_hbm.at[pl.ds(0, bm * ROW_TILE), :], ssem.at[sl]).wait()

    def spare_row(set_id, r):
        return TOP_K * n_tok + set_id * bm + r

    @pl.when(i < nact)
    def _():
        @pl.when(i == 0)
        def _():
            for blk in range(ring - 1):
                gather_start(min(blk, n_blocks - 1), blk, range(bm))
            ybuf[...] = jnp.zeros_like(ybuf)
            for sl in range(ring):
                spare = pltpu.make_async_copy(
                    ybuf.at[sl], y_hbm.at[pl.ds(spare_row(sl, 0) * ROW_TILE, bm * ROW_TILE), :], ssem.at[sl])
                spare.start()
                spare.wait()

        def weight_copies(e):
            return [pltpu.make_async_copy(w_hbm.at[e], stage, wsem.at[k])
                    for k, (w_hbm, stage) in enumerate(((w1_hbm, wf1), (w3_hbm, wf3), (w2_hbm, wf2)))]

        this_e = blk_e_ref[i]

        @pl.when(i == 0)
        def _():
            for cp in weight_copies(this_e):
                cp.start()

        @pl.when((i == 0) | (this_e != blk_e_ref[jnp.maximum(i - 1, 0)]))
        def _():
            for cp in weight_copies(this_e):
                cp.wait()
            for stage, wb in ((wf1, w1b), (wf3, w3b), (wf2, w2b)):
                def cast_rows(rb, carry, stage=stage, wb=wb):
                    rows = pl.ds(pl.multiple_of(rb * LANES, LANES), LANES)
                    wb[rows, :] = stage[rows, :].astype(BF16)
                    return carry
                lax.fori_loop(0, stage.shape[0] // LANES, cast_rows, 0)
            nxt_blk = nxt_ref[i]

            @pl.when(nxt_blk < nact)
            def _():
                for cp in weight_copies(blk_e_ref[nxt_blk]):
                    cp.start(priority=1)

        ahead = jnp.minimum(i + ring - 1, n_blocks - 1)
        prv = jnp.maximum(i - 1, 0)

        def compute_block(slot):
            far_slot = (slot - 1) % ring
            others = [(slot - 1 - k) % ring for k in range(ring - 1)]
            gather_wait(slot)
            xb = _load_row_tiles(xbuf.at[slot], bm).astype(BF16)
            acts = []
            for c in range(n_chunks):
                cols = slice(c * chunk, (c + 1) * chunk)
                a = _dot(xb, w1b[:, cols])
                g = _dot(xb, w3b[:, cols])
                acts.append((_silu(a) * g).astype(BF16))
                if c < n_chunks // 2:
                    gather_start(ahead, far_slot, range(c * rows_per_chunk, (c + 1) * rows_per_chunk))
                else:
                    first = (c - n_chunks // 2) * rows_per_chunk
                    scatter_start(prv, far_slot, range(first, first + rows_per_chunk), i > 0, 2)

            y = _dot(jnp.concatenate(acts, axis=1), w2b[...])

            @pl.when(i >= ring - 1)
            def _():
                scatter_wait(slot)

            _store_row_tiles(ybuf.at[slot], y)

            @pl.when(i == nact - 1)
            def _():
                scatter_start(i, slot, range(bm), True, 1)
                for k, sl in enumerate(others):
                    pl.when(i >= k)(functools.partial(scatter_wait, sl))
                    gather_wait(sl)
                scatter_wait(slot)

        for parity in range(ring):
            pl.when(lax.rem(i, ring) == parity)(functools.partial(compute_block, parity))


def _moe(blk_e, nxt_blk, base, nvalid, nact, src, w1, w3, w2, h2_tiles, n_blocks):
    t = h2_tiles.shape[0] // ROW_TILE
    d = w1.shape[1]
    de = w1.shape[2]
    bm = MOE_BLOCK
    hbm = pl.BlockSpec(memory_space=pl.ANY)
    grid_spec = pltpu.PrefetchScalarGridSpec(
        num_scalar_prefetch=6,
        grid=(n_blocks,),
        in_specs=[hbm, hbm, hbm, hbm],
        out_specs=hbm,
        scratch_shapes=[pltpu.VMEM((d, de), F32), pltpu.VMEM((d, de), F32), pltpu.VMEM((de, d), F32),
                        pltpu.VMEM((d, de), BF16), pltpu.VMEM((d, de), BF16), pltpu.VMEM((de, d), BF16),
                        pltpu.VMEM((MOE_RING, bm * ROW_TILE, LANES), F32),
                        pltpu.VMEM((MOE_RING, bm * ROW_TILE, LANES), F32),
                        pltpu.SemaphoreType.DMA((MOE_RING,)), pltpu.SemaphoreType.DMA((MOE_RING,)),
                        pltpu.SemaphoreType.DMA((3,))],
    )
    return pl.pallas_call(
        functools.partial(_moe_kernel, n_tok=t),
        out_shape=jax.ShapeDtypeStruct(((TOP_K * t + MOE_RING * bm) * ROW_TILE, LANES), F32),
        grid_spec=grid_spec,
        compiler_params=_cparams(("arbitrary",)),
        name="moe_experts",
    )(blk_e, nxt_blk, base, nvalid, nact, src, w1, w3, w2, h2_tiles)


def _final_kernel(x1_ref, y0_ref, y1_ref, gt_ref, g2_ref, fg_ref, o_ref):
    gt = gt_ref[...]
    tm = x1_ref.shape[0]
    y2 = gt[:, 0:1] * _load_row_tiles(y0_ref, tm) + gt[:, 1:2] * _load_row_tiles(y1_ref, tm)
    x = x1_ref[...] + g2_ref[0] * y2
    o_ref[...] = x * lax.rsqrt(jnp.mean(x * x, axis=-1, keepdims=True) + EPS) * fg_ref[...]


def _final(x1, y, gates_tok, g2, fg, rows_per_mod, tm):
    t, d = x1.shape
    nb = t // tm
    blocks_per_mod = rows_per_mod // tm
    return pl.pallas_call(
        _final_kernel,
        out_shape=jax.ShapeDtypeStruct((t, d), F32),
        grid=(nb,),
        in_specs=[pl.BlockSpec((tm, d), lambda i: (i, 0)),
                  pl.BlockSpec((tm * ROW_TILE, LANES), lambda i: (i, 0)),
                  pl.BlockSpec((tm * ROW_TILE, LANES), lambda i: (i + nb, 0)),
                  pl.BlockSpec((tm, TOP_K), lambda i: (i, 0)),
                  pl.BlockSpec((1, 1, d), lambda i: (i // blocks_per_mod, 0, 0)),
                  pl.BlockSpec((1, d), lambda i: (0, 0))],
        out_specs=pl.BlockSpec((tm, d), lambda i: (i, 0)),
        compiler_params=_cparams(("arbitrary",)),
        name="combine_final_norm",
    )(x1, y, y, gates_tok, g2, fg)


def kernel(x, c, ctx, c_ctx, w_ada, b_ada, norm1_g, norm2_g, w_in, conv_dw, conv_b, conv_ln_g, conv_ln_b,
           lru_conv_w, lru_conv_b, lru_wa, lru_ba, lru_wx, lru_bx, lru_lam, w_out,
           router_wg, router_bg, router_we, router_be, w1, w3, w2, final_g):
    assert w_ada.shape[0] == 1, "single-layer block"
    assert x.shape[2] == ROW_TILE * LANES, "row-tile layout: one (ROW_TILE, LANES) tile per token row"
    b, s, d = x.shape
    n_ctx = ctx.shape[1]
    t = b * s
    cc = conv_dw.shape[2]
    lw = lru_conv_w.shape[2]

    c_rows = jnp.zeros((SUBLANES, d), F32).at[:b].set(c).at[b].set(c_ctx)
    mod = _ada(c_rows, w_ada[0], b_ada)
    mod_l = mod[:b].reshape(b, 6, 1, d)
    sh1, sc1, g1, sh2, sc2, g2 = (mod_l[:, k] for k in range(6))
    mod_c = mod[b].reshape(6, 1, 1, d)
    csh1, csc1 = mod_c[0], mod_c[1]

    w_in_b = w_in[0].astype(BF16)
    w_out_b = w_out[0].astype(BF16)
    heads_per_blk = LRU_LANES // lru_wa.shape[3]
    n_cblk = lw // LRU_LANES

    def blockdiag(wh):
        hd = wh.shape[1]
        wh = wh.reshape(n_cblk, heads_per_blk, hd, hd)
        eye = jnp.eye(heads_per_blk, dtype=wh.dtype)
        return jnp.einsum("chij,hg->chigj", wh, eye).reshape(n_cblk, LRU_LANES, LRU_LANES)

    wg = jnp.concatenate([blockdiag(lru_wa[0, 0]), blockdiag(lru_wx[0, 0]),
                          blockdiag(lru_wa[0, 1]), blockdiag(lru_wx[0, 1])], axis=2).astype(BF16)
    gb = 0.5 * jnp.stack([lru_ba[0, 0], lru_bx[0, 0], lru_ba[0, 1], lru_bx[0, 1]])
    lam = lru_lam[0]

    zc = _inproj_ctx(ctx.reshape(b * n_ctx, d), csh1, csc1, norm1_g, w_in_b[:, 2 * cc:2 * cc + lw], n_ctx)
    h0 = _lru(zc.reshape(b, n_ctx, lw), None, jnp.zeros((2, b, 1, lw), F32),
              lru_conv_w[0], lru_conv_b, wg, gb, lam, True, "rglru_ctx")

    x2 = x.reshape(t, d)
    conv_w_rep = jnp.repeat(conv_dw[0], SUBLANES, axis=0).reshape(CONV_TAPS * SUBLANES, cc // LANES, LANES)
    conv_w_rep = conv_w_rep.transpose(1, 0, 2)
    conv_l, zl, gg = _inproj(x2, sh1, sc1, norm1_g, w_in_b, conv_w_rep, conv_b, conv_ln_g, conv_ln_b, s)
    lru_l = _lru(zl.reshape(b, s, lw), gg.reshape(b, s, lw), h0,
                 lru_conv_w[0], lru_conv_b, wg, gb, lam, False, "rglru")

    wr = jnp.zeros((ROUTE_ROWS, d), F32)
    wr = wr.at[:N_GROUPS].set(router_wg[0].T)
    wr = wr.at[EXPERT_ROW0:EXPERT_ROW0 + N_EXPERTS].set(router_we[0].reshape(d, N_EXPERTS).T)
    rb = jnp.zeros((ROUTE_ROWS, 1), F32)
    rb = rb.at[:N_GROUPS, 0].set(router_bg[0])
    rb = rb.at[EXPERT_ROW0:EXPERT_ROW0 + N_EXPERTS, 0].set(router_be[0].reshape(-1))
    r_hi, r_lo = _split_bf16(wr)
    x1, h2, logits_t = _outproj(conv_l, lru_l.reshape(t, lw), x2, g1, sh2, sc2, norm2_g, w_out_b,
                                r_hi, jnp.concatenate([r_hi, r_lo], axis=0), rb, s, TOKEN_BLOCK)

    gates, eid, counts = _route(logits_t)
    n_blocks = (TOP_K * t) // MOE_BLOCK + N_EXPERTS
    src = jnp.pad(_sort_assignments(eid.reshape(TOP_K * t // LANES, LANES)).reshape(-1), (0, MOE_BLOCK))
    blk_e, nxt_blk, base, nvalid, nact = _tables(counts[:, 0].astype(jnp.int32), n_blocks, MOE_BLOCK)
    y = _moe(blk_e, nxt_blk, base, nvalid, nact, src, w1[0], w3[0], w2[0], h2, n_blocks)
    out = _final(x1, y, gates.T, g2, final_g.reshape(1, d), s, TOKEN_BLOCK)
    return out.reshape(b, s, d)
```

```python
import functools

import jax
import jax.numpy as jnp
from jax import lax
from jax.experimental import pallas as pl
from jax.experimental.pallas import tpu as pltpu

F32 = jnp.float32
BF16 = jnp.bfloat16

EPS = 1e-6
CONV_TAPS = 31
LRU_TAPS = 4
LRU_C = 8.0
GRID_W = 64
N_GROUPS = 4
EXPERTS_PER_GROUP = 8
N_EXPERTS = N_GROUPS * EXPERTS_PER_GROUP
TOP_K = 2

SUBLANES = 8
LANES = 128
TOKEN_BLOCK = 1024
LRU_LANES = 128
EXPERT_ROW0 = 8
ROUTE_ROWS = EXPERT_ROW0 + N_EXPERTS
MOE_BLOCK = 256
MOE_RING = 3
STREAM_RING = 3
ROUTE_LANES = 2048
ADA_COLS = 1536
V7X_VMEM_BYTES = 64 * 1024 * 1024
VMEM_LIMIT = 3 * V7X_VMEM_BYTES // 4


def _cparams(sem, vmem=VMEM_LIMIT):
    return pltpu.CompilerParams(dimension_semantics=sem, vmem_limit_bytes=vmem)


def _split_bf16(a):
    hi = a.astype(BF16)
    lo = (a - hi.astype(F32)).astype(BF16)
    return hi, lo


def _dot(a, b):
    return jnp.dot(a, b, preferred_element_type=F32)


def _dot_nt(a, b):
    return lax.dot_general(a, b, (((1,), (1,)), ((), ())), preferred_element_type=F32)


def _times_sigmoid(v, x):
    hv = 0.5 * v
    return hv * jnp.tanh(0.5 * x) + hv


def _silu(x):
    h = 0.5 * x
    return h * jnp.tanh(h) + h


def _gelu_tanh(x):
    c = 0.7978845608028654
    h = 0.5 * x
    return h * jnp.tanh(x * ((c * 0.044715) * (x * x) + c)) + h


ROW_TILE = SUBLANES


def _store_row_tiles(ref, x):
    rows = x.shape[0]
    for s in range(ROW_TILE):
        ref[pl.ds(s, rows, stride=ROW_TILE), :] = x[:, s * LANES:(s + 1) * LANES]


def _load_row_tiles(ref, rows):
    return jnp.concatenate([ref[pl.ds(s, rows, stride=ROW_TILE), :] for s in range(ROW_TILE)], axis=1)


def _rms_mod(x, g, shift, scale):
    y = x * lax.rsqrt(jnp.mean(x * x, axis=-1, keepdims=True) + EPS)
    return y * (g * (1.0 + scale)) + shift


def _ada_kernel(c_ref, w_ref, b_ref, o_ref):
    a = _silu(c_ref[...])
    a_hi, a_lo = _split_bf16(a)
    w_hi, w_lo = _split_bf16(w_ref[...])
    o_ref[...] = _dot(a_hi, w_hi) + _dot(a_lo, w_hi) + _dot(a_hi, w_lo) + b_ref[...]


def _ada(c_rows, w, b):
    m, d = c_rows.shape
    n = w.shape[1]
    bn = ADA_COLS
    return pl.pallas_call(
        _ada_kernel,
        out_shape=jax.ShapeDtypeStruct((m, n), F32),
        grid=(n // bn,),
        in_specs=[pl.BlockSpec((m, d), lambda j: (0, 0)),
                  pl.BlockSpec((d, bn), lambda j: (0, j)),
                  pl.BlockSpec((1, bn), lambda j: (0, j))],
        out_specs=pl.BlockSpec((m, bn), lambda j: (0, j)),
        compiler_params=_cparams(("arbitrary",)),
        name="ada_mod",
    )(c_rows, w, b)


def _inproj_ctx_kernel(x_ref, sh_ref, sc_ref, g_ref, w_ref, o_ref):
    h = _rms_mod(x_ref[...], g_ref[...], sh_ref[0], sc_ref[0])
    o_ref[...] = _dot(h.astype(BF16), w_ref[...])


def _conformer_conv(u, cw_ref, cb_ref, lg_ref, lb_ref, o_ref, stg_ref, xt_ref, ot_ref):
    c = u.shape[1]
    slabs = range(c // LANES)
    half = CONV_TAPS // 2
    pitch = GRID_W + SUBLANES
    group = SUBLANES
    for l in slabs:
        for q in range(SUBLANES):
            stg_ref[l, q * pitch:q * pitch + GRID_W, :] = u[q * GRID_W:(q + 1) * GRID_W, l * LANES:(l + 1) * LANES]

    def conv_slab(l, carry):
        for t in range(GRID_W):
            xt_ref[l, t * SUBLANES:(t + 1) * SUBLANES, :] = stg_ref[l, pl.ds(t, SUBLANES, stride=pitch), :]
        for t0 in range(0, GRID_W, group):
            accs = [None] * group
            for k in range(CONV_TAPS):
                srcs = [t0 + j + k - half for j in range(group)]
                if not any(0 <= sidx < GRID_W for sidx in srcs):
                    continue
                wk = cw_ref[l, k * SUBLANES:(k + 1) * SUBLANES, :]
                for j, sidx in enumerate(srcs):
                    if 0 <= sidx < GRID_W:
                        term = wk * xt_ref[l, sidx * SUBLANES:(sidx + 1) * SUBLANES, :]
                        accs[j] = term if accs[j] is None else accs[j] + term
            for j in range(group):
                ot_ref[l, (t0 + j) * SUBLANES:(t0 + j + 1) * SUBLANES, :] = accs[j]
        return carry
    lax.fori_loop(0, c // LANES, conv_slab, 0)
    rows_per_pass = group * SUBLANES
    for r0 in range(0, GRID_W * SUBLANES, rows_per_pass):
        rows = slice(r0, r0 + rows_per_pass)
        acc = jnp.concatenate([ot_ref[l, rows, :] for l in slabs], axis=1) + cb_ref[...]
        mu = jnp.mean(acc, axis=-1, keepdims=True)
        cen = acc - mu
        var = jnp.mean(cen * cen, axis=-1, keepdims=True)
        y = _silu(cen * lax.rsqrt(var + EPS) * lg_ref[...] + lb_ref[...])
        for l in slabs:
            ot_ref[l, rows, :] = y[:, l * LANES:(l + 1) * LANES]
    for q in range(SUBLANES):
        o_ref[q * GRID_W:(q + 1) * GRID_W, :] = jnp.concatenate(
            [ot_ref[l, pl.ds(q, GRID_W, stride=SUBLANES), :] for l in slabs], axis=1).astype(o_ref.dtype)


def _inproj_kernel(x_ref, sh_ref, sc_ref, g_ref, w_ref, cw_ref, cb_ref, lg_ref, lb_ref,
                   cv_ref, zl_ref, gg_ref, stg_ref, xt_ref, ot_ref):
    h = _rms_mod(x_ref[...], g_ref[...], sh_ref[0], sc_ref[0])
    z = _dot(h.astype(BF16), w_ref[...])
    c = cv_ref.shape[1]
    zl_ref[...] = z[:, 2 * c:3 * c]
    gg_ref[...] = _gelu_tanh(z[:, 3 * c:])
    u = _times_sigmoid(z[:, :c], z[:, c:2 * c])
    _conformer_conv(u, cw_ref, cb_ref, lg_ref, lb_ref, cv_ref, stg_ref, xt_ref, ot_ref)


def _inproj_ctx(x2, shift, scale, g, w, tm):
    t, d = x2.shape
    n = w.shape[1]
    mod_spec = pl.BlockSpec((1, 1, d), lambda i: (0, 0, 0))
    return pl.pallas_call(
        _inproj_ctx_kernel,
        out_shape=jax.ShapeDtypeStruct((t, n), F32),
        grid=(t // tm,),
        in_specs=[pl.BlockSpec((tm, d), lambda i: (i, 0)), mod_spec, mod_spec,
                  pl.BlockSpec((1, d), lambda i: (0, 0)),
                  pl.BlockSpec((d, n), lambda i: (0, 0))],
        out_specs=pl.BlockSpec((tm, n), lambda i: (i, 0)),
        compiler_params=_cparams(("arbitrary",)),
        name="in_proj_ctx",
    )(x2, shift, scale, g, w)


def _inproj(x2, shift, scale, g, w, cw, cb, lg, lb, rows_per_mod):
    t, d = x2.shape
    n = w.shape[1]
    c = n // 4
    tm = SUBLANES * GRID_W
    blocks_per_mod = rows_per_mod // tm
    mod_spec = pl.BlockSpec((1, 1, d), lambda i: (i // blocks_per_mod, 0, 0))
    const = lambda a: pl.BlockSpec(a.shape, lambda i: (0,) * a.ndim)
    rows = pl.BlockSpec((tm, c), lambda i: (i, 0))
    return pl.pallas_call(
        _inproj_kernel,
        out_shape=[jax.ShapeDtypeStruct((t, c), BF16), jax.ShapeDtypeStruct((t, c), F32),
                   jax.ShapeDtypeStruct((t, c), F32)],
        grid=(t // tm,),
        in_specs=[pl.BlockSpec((tm, d), lambda i: (i, 0)), mod_spec, mod_spec, const(g), const(w),
                  const(cw), const(cb), const(lg), const(lb)],
        out_specs=[rows, rows, rows],
        scratch_shapes=[pltpu.VMEM((c // LANES, SUBLANES * (GRID_W + SUBLANES), LANES), F32),
                        pltpu.VMEM((c // LANES, tm, LANES), F32), pltpu.VMEM((c // LANES, tm, LANES), F32)],
        compiler_params=_cparams(("arbitrary",)),
        name="in_proj",
    )(x2, shift, scale, g, w, cw, cb, lg, lb)


def _lru_kernel(zl_ref, h0_ref, cw_ref, cb_ref, wg_ref, gb_ref, lam_ref, *rest, seq, final_only):
    if final_only:
        o_ref, stage_ref, zt_ref, af_ref, bf_ref, ab_ref, bb_ref, pf_ref, hf_ref, pb_ref, hb_ref = rest
        gg_ref = None
    else:
        gg_ref, o_ref, stage_ref, zt_ref, af_ref, bf_ref, ab_ref, bb_ref, pf_ref, hf_ref, pb_ref, hb_ref = rest
    cl = seq // SUBLANES
    pitch = cl + SUBLANES
    c = zl_ref.shape[2]
    a_refs = (af_ref, ab_ref)
    b_refs = (bf_ref, bb_ref)
    halo = 2

    def tile_rows(t):
        return pl.ds(pl.multiple_of(t * SUBLANES, SUBLANES), SUBLANES)

    for j in range(SUBLANES):
        stage_ref[j * pitch:j * pitch + cl, :] = zl_ref[0, j * cl:(j + 1) * cl, :]

    def to_chunk_layout(t, carry):
        zt_ref[tile_rows(t + halo), :] = stage_ref[pl.ds(t, SUBLANES, stride=pitch), :]
        return carry
    lax.fori_loop(0, cl, to_chunk_layout, 0, unroll=8)

    sub = lax.broadcasted_iota(jnp.int32, (SUBLANES, c), 0)
    for t_src, t_dst in ((cl - 2, -2), (cl - 1, -1)):
        v = pltpu.roll(zt_ref[(t_src + halo) * SUBLANES:(t_src + halo + 1) * SUBLANES, :], 1, 0)
        zt_ref[(t_dst + halo) * SUBLANES:(t_dst + halo + 1) * SUBLANES, :] = jnp.where(sub == 0, 0.0, v)
    v = pltpu.roll(zt_ref[halo * SUBLANES:(halo + 1) * SUBLANES, :], SUBLANES - 1, 0)
    zt_ref[(cl + halo) * SUBLANES:(cl + halo + 1) * SUBLANES, :] = jnp.where(sub == SUBLANES - 1, 0.0, v)

    lam = lam_ref[...]
    nlam = -lam
    softplus = jnp.maximum(nlam, 0.0) + jnp.log1p(jnp.exp(-jnp.abs(nlam)))
    half_decay = (-0.5 * LRU_C) * softplus
    cw_half = 0.5 * cw_ref[...]
    cb_half = 0.5 * cb_ref[...]
    piece = cl

    gb_row = jnp.concatenate([gb_ref[k:k + 1, :] for k in range(4)], axis=1)
    gb_hi = gb_row.astype(BF16).astype(F32)
    gb_mid = (gb_row - gb_hi).astype(BF16).astype(F32)
    gb_lo = ((gb_row - gb_hi) - gb_mid).astype(BF16).astype(F32)
    row = lax.broadcasted_iota(jnp.int32, (c, 4 * c), 0)
    bias_rows = jnp.where(row == 0, gb_hi, jnp.where(row == 1, gb_mid, jnp.where(row == 2, gb_lo, 0.0)))
    wg = jnp.concatenate([wg_ref[0], bias_rows.astype(BF16)], axis=0)
    ones_cols = (lax.broadcasted_iota(jnp.int32, (piece, c), 1) < 3).astype(BF16)

    for p0 in range(0, cl * SUBLANES, piece):
        ul_half = cb_half + jnp.zeros((piece, c), F32)
        for k in range(LRU_TAPS):
            off = p0 + k * SUBLANES
            ul_half = ul_half + cw_half[k:k + 1, :] * zt_ref[off:off + piece, :]
        g = _dot(jnp.concatenate([ul_half.astype(BF16), ones_cols], axis=1), wg)
        for d in range(2):
            t_r = jnp.tanh(g[:, (2 * d) * c:(2 * d + 1) * c])
            t_i = jnp.tanh(g[:, (2 * d + 1) * c:(2 * d + 2) * c])
            log_a = half_decay[d:d + 1, :] * t_r + half_decay[d:d + 1, :]
            a = jnp.exp(log_a)
            m = jnp.maximum(jnp.tanh(log_a) * (-1.0 - a * a), 1e-12)
            mult = m * lax.rsqrt(m)
            a_refs[d][p0:p0 + piece, :] = a
            b_refs[d][p0:p0 + piece, :] = mult * (t_i * ul_half + ul_half)

    def two_steps(a_ref, b_ref, p_out, h_out, s0, s1, h, p):
        a0 = a_ref[s0, :]
        a1 = a_ref[s1, :]
        b0 = b_ref[s0, :]
        a10 = a1 * a0
        b10 = a1 * b0 + b_ref[s1, :]
        h_out[s0, :] = a0 * h + b0
        p_out[s0, :] = a0 * p
        h = a10 * h + b10
        p = a10 * p
        h_out[s1, :] = h
        p_out[s1, :] = p
        return h, p

    group = 8
    group_rows = group * SUBLANES

    def step(n, carry):
        hf, pf, hb, pb = carry
        base_f = pl.multiple_of(n * group_rows, group_rows)
        base_b = pl.multiple_of((cl // group - 1 - n) * group_rows, group_rows)
        for k in range(0, group, 2):
            tf0 = pl.ds(base_f + k * SUBLANES, SUBLANES)
            tf1 = pl.ds(base_f + (k + 1) * SUBLANES, SUBLANES)
            hf, pf = two_steps(af_ref, bf_ref, pf_ref, hf_ref, tf0, tf1, hf, pf)
            tb0 = pl.ds(base_b + (group - 1 - k) * SUBLANES, SUBLANES)
            tb1 = pl.ds(base_b + (group - 2 - k) * SUBLANES, SUBLANES)
            hb, pb = two_steps(ab_ref, bb_ref, pb_ref, hb_ref, tb0, tb1, hb, pb)
        return hf, pf, hb, pb

    zero = jnp.zeros((SUBLANES, c), F32)
    one = jnp.ones((SUBLANES, c), F32)
    lax.fori_loop(0, cl // group, step, (zero, one, zero, one))

    cf = [None] * SUBLANES
    cbk = [None] * SUBLANES
    s = h0_ref[0, 0]
    for j in range(SUBLANES):
        cf[j] = s
        last = (cl - 1) * SUBLANES + j
        s = hf_ref[last:last + 1, :] + pf_ref[last:last + 1, :] * s
    final_f = s
    s = h0_ref[1, 0]
    for j in reversed(range(SUBLANES)):
        cbk[j] = s
        s = hb_ref[j:j + 1, :] + pb_ref[j:j + 1, :] * s
    final_b = s

    if final_only:
        o_ref[0, 0] = final_f
        o_ref[1, 0] = final_b
        return
    carry_f = jnp.concatenate(cf, axis=0)
    carry_b = jnp.concatenate(cbk, axis=0)

    def to_natural(t, carry):
        rows = tile_rows(t)
        h = (hf_ref[rows, :] + pf_ref[rows, :] * carry_f) + (hb_ref[rows, :] + pb_ref[rows, :] * carry_b)
        stage_ref[pl.ds(t, SUBLANES, stride=pitch), :] = h
        return carry
    lax.fori_loop(0, cl, to_natural, 0, unroll=8)
    for j in range(SUBLANES):
        h = stage_ref[j * pitch:j * pitch + cl, :]
        o_ref[0, j * cl:(j + 1) * cl, :] = (h * gg_ref[0, j * cl:(j + 1) * cl, :]).astype(o_ref.dtype)


def _lru(zl, gg, h0, cw, cb, wg, gb, lam, final_only, name):
    b, seq, c = zl.shape
    cbk = LRU_LANES
    cl = seq // SUBLANES
    pitch = cl + SUBLANES
    seq_spec = pl.BlockSpec((1, seq, cbk), lambda bi, ci: (bi, 0, ci))
    st_spec = pl.BlockSpec((2, 1, 1, cbk), lambda bi, ci: (0, bi, 0, ci))
    chan = lambda rows: pl.BlockSpec((rows, cbk), lambda bi, ci: (0, ci))
    in_specs = [seq_spec, st_spec, chan(LRU_TAPS), chan(1),
                pl.BlockSpec((1, cbk, 4 * cbk), lambda bi, ci: (ci, 0, 0)), chan(4), chan(2)]
    args = [zl, h0, cw, cb, wg, gb, lam]
    if final_only:
        out_shape = jax.ShapeDtypeStruct((2, b, 1, c), F32)
        out_spec = st_spec
    else:
        in_specs.append(seq_spec)
        args.append(gg)
        out_shape = jax.ShapeDtypeStruct((b, seq, c), BF16)
        out_spec = seq_spec
    coef = pltpu.VMEM((seq, cbk), F32)
    return pl.pallas_call(
        functools.partial(_lru_kernel, seq=seq, final_only=final_only),
        out_shape=out_shape,
        grid=(b, c // cbk),
        in_specs=in_specs,
        out_specs=out_spec,
        scratch_shapes=[pltpu.VMEM((SUBLANES * pitch, cbk), F32),
                        pltpu.VMEM((seq + 2 * SUBLANES * SUBLANES, cbk), F32)] + [coef] * 8,
        compiler_params=_cparams(("arbitrary", "arbitrary")),
        name=name,
    )(*args)


def _stream_rows(streams, sem):
    i = pl.program_id(0)
    n = pl.num_programs(0)

    def copies(step):
        slot = lax.rem(step, STREAM_RING)
        out = []
        for k, (src, buf) in enumerate(streams):
            rows = buf.shape[1]
            src_rows = pl.ds(pl.multiple_of(step * rows, rows), rows)
            out.append(pltpu.make_async_copy(src.at[src_rows], buf.at[slot], sem.at[k, slot]))
        return out

    @pl.when(i == 0)
    def _():
        for ahead in range(STREAM_RING - 1):
            for cp in copies(i + ahead):
                cp.start()

    @pl.when(i + STREAM_RING - 1 < n)
    def _():
        for cp in copies(i + STREAM_RING - 1):
            cp.start()

    for cp in copies(i):
        cp.wait()
    return lax.rem(i, STREAM_RING)


def _outproj_kernel(cv_hbm, lr_hbm, x_hbm, g1_ref, sh_ref, sc_ref, ng_ref, wo_ref, rh_ref, rc_ref, rb_ref,
                    x1_ref, h2_ref, lg_ref, cv_buf, lr_buf, x_buf, sem):
    c = cv_buf.shape[2]
    slot = _stream_rows(((cv_hbm, cv_buf), (lr_hbm, lr_buf), (x_hbm, x_buf)), sem)
    y = _dot(cv_buf[slot], wo_ref[0:c, :]) + _dot(lr_buf[slot], wo_ref[c:2 * c, :])
    x1 = x_buf[slot] + g1_ref[0] * y
    x1_ref[...] = x1
    h2 = _rms_mod(x1, ng_ref[...], sh_ref[0], sc_ref[0])
    h_hi, h_lo = _split_bf16(h2)
    _store_row_tiles(h2_ref, h2)
    rows = rh_ref.shape[0]
    both = _dot_nt(rc_ref[...], h_hi)
    lg_ref[...] = both[:rows] + both[rows:] + _dot_nt(rh_ref[...], h_lo) + rb_ref[...]


def _outproj(conv_l, lru_l, x2, g1, sh2, sc2, ng, wo, r_hi, r_both, r_b, rows_per_mod, tm):
    t, d = x2.shape
    c = conv_l.shape[1]
    blocks_per_mod = rows_per_mod // tm
    mod_spec = pl.BlockSpec((1, 1, d), lambda i: (i // blocks_per_mod, 0, 0))
    full = lambda a: pl.BlockSpec(a.shape, lambda i: (0, 0))
    hbm = pl.BlockSpec(memory_space=pl.ANY)
    return pl.pallas_call(
        _outproj_kernel,
        out_shape=[jax.ShapeDtypeStruct((t, d), F32), jax.ShapeDtypeStruct((t * ROW_TILE, LANES), F32),
                   jax.ShapeDtypeStruct((ROUTE_ROWS, t), F32)],
        grid=(t // tm,),
        in_specs=[hbm, hbm, hbm, mod_spec, mod_spec, mod_spec,
                  full(ng), full(wo), full(r_hi), full(r_both), full(r_b)],
        out_specs=[pl.BlockSpec((tm, d), lambda i: (i, 0)), pl.BlockSpec((tm * ROW_TILE, LANES), lambda i: (i, 0)),
                   pl.BlockSpec((ROUTE_ROWS, tm), lambda i: (0, i))],
        scratch_shapes=[pltpu.VMEM((STREAM_RING, tm, c), conv_l.dtype), pltpu.VMEM((STREAM_RING, tm, c), lru_l.dtype),
                        pltpu.VMEM((STREAM_RING, tm, d), F32), pltpu.SemaphoreType.DMA((3, STREAM_RING))],
        compiler_params=_cparams(("arbitrary",)),
        name="out_proj_router",
    )(conv_l, lru_l, x2, g1, sh2, sc2, ng, wo, r_hi, r_both, r_b)


def _route_chunk(lg_ref, lanes):
    n = lanes.size
    e = EXPERTS_PER_GROUP
    lgrp = lg_ref[0:N_GROUPS, lanes]
    gidx = lax.broadcasted_iota(jnp.int32, (N_GROUPS, n), 0)
    m = jnp.max(lgrp, axis=0, keepdims=True)
    ex = jnp.exp(lgrp - m)
    pg = ex / jnp.sum(ex, axis=0, keepdims=True)
    p_grp = jnp.max(pg, axis=0, keepdims=True)
    grp = jnp.min(jnp.where(pg == p_grp, gidx, N_GROUPS), axis=0, keepdims=True)
    le = jnp.zeros((e, n), F32)
    for g in range(N_GROUPS):
        rows = lg_ref[EXPERT_ROW0 + g * e:EXPERT_ROW0 + (g + 1) * e, lanes]
        le = jnp.where(grp == g, rows, le)
    m = jnp.max(le, axis=0, keepdims=True)
    ex = jnp.exp(le - m)
    pe = ex / jnp.sum(ex, axis=0, keepdims=True)
    eidx = lax.broadcasted_iota(jnp.int32, (e, n), 0)
    p1 = jnp.max(pe, axis=0, keepdims=True)
    i1 = jnp.min(jnp.where(pe == p1, eidx, e), axis=0, keepdims=True)
    pe2 = jnp.where(eidx == i1, -1.0, pe)
    p2 = jnp.max(pe2, axis=0, keepdims=True)
    i2 = jnp.min(jnp.where(pe2 == p2, eidx, e), axis=0, keepdims=True)
    denom = p1 + p2
    base = grp * e
    return (base + i1, base + i2), (p_grp * p1 / denom, p_grp * p2 / denom)


def _route_kernel(lg_ref, gate_ref, eid_ref, cnt_ref):
    tl = lg_ref.shape[1]
    chunk = 2 * LANES
    ne = N_EXPERTS

    @pl.when(pl.program_id(0) == 0)
    def _():
        cnt_ref[...] = jnp.zeros_like(cnt_ref)

    ones = jnp.ones((chunk, LANES), BF16)
    eidx = lax.broadcasted_iota(jnp.int32, (ne, chunk), 0)
    total = jnp.zeros((ne, LANES), F32)
    for cix in range(tl // chunk):
        lanes = pl.ds(cix * chunk, chunk)
        eids, gates = _route_chunk(lg_ref, lanes)
        for k in range(TOP_K):
            total = total + _dot((eidx == eids[k]).astype(BF16), ones)
        gate_ref[:, lanes] = jnp.concatenate(gates, axis=0)
        eid_ref[:, lanes] = jnp.concatenate(eids, axis=0)
    cnt_ref[...] = cnt_ref[...] + total


def _route(logits_t):
    rows, t = logits_t.shape
    tl = ROUTE_LANES
    blk = pl.BlockSpec((TOP_K, tl), lambda i: (0, i))
    return pl.pallas_call(
        _route_kernel,
        out_shape=[jax.ShapeDtypeStruct((TOP_K, t), F32), jax.ShapeDtypeStruct((TOP_K, t), jnp.int32),
                   jax.ShapeDtypeStruct((N_EXPERTS, LANES), F32)],
        grid=(t // tl,),
        in_specs=[pl.BlockSpec((rows, tl), lambda i: (0, i))],
        out_specs=[blk, blk, pl.BlockSpec((N_EXPERTS, LANES), lambda i: (0, 0))],
        compiler_params=_cparams(("arbitrary",)),
        name="route",
    )(logits_t)


def _index_bit(j, lane, sub, tile):
    lane_bits = LANES.bit_length() - 1
    tile_bits = lane_bits + SUBLANES.bit_length() - 1
    if j < lane_bits:
        return (lane >> j) & 1
    if j < tile_bits:
        return (sub >> (j - lane_bits)) & 1
    return (tile >> (j - tile_bits)) & 1


def _bitonic_sort(tiles):
    n_t = len(tiles)
    log_n = (n_t * SUBLANES * LANES).bit_length() - 1
    lane_bits = LANES.bit_length() - 1
    tile_bits = lane_bits + SUBLANES.bit_length() - 1
    lane = lax.broadcasted_iota(jnp.int32, (SUBLANES, LANES), 1)
    sub = lax.broadcasted_iota(jnp.int32, (SUBLANES, LANES), 0)
    for k in range(1, log_n + 1):
        for j in range(k - 1, -1, -1):
            new = []
            for v in range(n_t):
                x = tiles[v]
                bj = _index_bit(j, lane, sub, v)
                if j < lane_bits:
                    d = 1 << j
                    p = jnp.where(bj == 1, pltpu.roll(x, d, 1), pltpu.roll(x, LANES - d, 1))
                elif j < tile_bits:
                    m = 1 << (j - lane_bits)
                    p = jnp.where(bj == 1, pltpu.roll(x, m, 0), pltpu.roll(x, SUBLANES - m, 0))
                else:
                    p = tiles[v ^ (1 << (j - tile_bits))]
                bk = _index_bit(k, lane, sub, v) if k < log_n else 0
                take_min = bj == bk
                lo, hi = jnp.minimum(x, p), jnp.maximum(x, p)
                if isinstance(take_min, bool):
                    new.append(lo if take_min else hi)
                else:
                    new.append(jnp.where(take_min, lo, hi))
            tiles = new
    return tiles


def _sort_kernel(eid_ref, src_ref):
    n_t = eid_ref.shape[0] // SUBLANES
    n_bits = (eid_ref.shape[0] * LANES).bit_length() - 1
    lane = lax.broadcasted_iota(jnp.int32, (SUBLANES, LANES), 1)
    sub = lax.broadcasted_iota(jnp.int32, (SUBLANES, LANES), 0)
    tiles = []
    for v in range(n_t):
        n = (v * SUBLANES + sub) * LANES + lane
        tiles.append((eid_ref[v * SUBLANES:(v + 1) * SUBLANES, :] << n_bits) | n)
    tiles = _bitonic_sort(tiles)
    for v in range(n_t):
        src_ref[v * SUBLANES:(v + 1) * SUBLANES, :] = (tiles[v] & ((1 << n_bits) - 1)) * ROW_TILE


def _sort_assignments(eid_rows):
    n = eid_rows.shape[0] * eid_rows.shape[1]
    assert n & (n - 1) == 0 and (N_EXPERTS * n) < 2 ** 31, "keys expert*n + assignment must fit an int32"
    return pl.pallas_call(
        _sort_kernel,
        out_shape=jax.ShapeDtypeStruct(eid_rows.shape, jnp.int32),
        name="dispatch_sort",
    )(eid_rows)


def _tables_kernel(cnt_ref, blk_e_ref, nxt_ref, base_ref, nvalid_ref, nact_ref, *, bm):
    n_blocks = blk_e_ref.shape[0]
    shift = bm.bit_length() - 1

    def per_expert(ex, carry):
        blk0, start = carry
        cnt = cnt_ref[ex]
        end_blk = blk0 + ((cnt + (bm - 1)) >> shift)

        def fill_blk(b, c):
            done = (b - blk0) << shift
            blk_e_ref[b] = ex
            nxt_ref[b] = end_blk
            base_ref[b] = start + done
            nvalid_ref[b] = jnp.minimum(cnt - done, bm)
            return c
        lax.fori_loop(blk0, end_blk, fill_blk, 0)
        return end_blk, start + cnt
    nact, _ = lax.fori_loop(0, N_EXPERTS, per_expert, (0, 0))
    nact_ref[0] = nact

    def fill_idle(b, c):
        blk_e_ref[b] = N_EXPERTS - 1
        nxt_ref[b] = n_blocks
        base_ref[b] = 0
        nvalid_ref[b] = 0
        return c
    lax.fori_loop(nact, n_blocks, fill_idle, 0)


def _tables(counts, n_blocks, bm):
    smem = pl.BlockSpec(memory_space=pltpu.SMEM)
    blocks = jax.ShapeDtypeStruct((n_blocks,), jnp.int32)
    return pl.pallas_call(
        functools.partial(_tables_kernel, bm=bm),
        out_shape=[blocks, blocks, blocks, blocks, jax.ShapeDtypeStruct((1,), jnp.int32)],
        in_specs=[smem],
        out_specs=[smem] * 5,
        name="dispatch_tables",
    )(counts)


def _moe_kernel(blk_e_ref, nxt_ref, base_ref, nvalid_ref, nact_ref, src_ref,
                w1_hbm, w3_hbm, w2_hbm, h_hbm, y_hbm,
                wf1, wf3, wf2, w1b, w3b, w2b, xbuf, ybuf, gsem, ssem, wsem, *, n_tok):
    i = pl.program_id(0)
    nact = nact_ref[0]
    n_blocks = blk_e_ref.shape[0]
    bm = xbuf.shape[1] // ROW_TILE
    ring = xbuf.shape[0]
    de = w1b.shape[1]
    chunk = 2 * LANES
    n_chunks = de // chunk
    rows_per_chunk = 2 * bm // n_chunks

    def tile(row):
        return pl.ds(pl.multiple_of(row * ROW_TILE, ROW_TILE), ROW_TILE)

    def tiles_at(offset):
        return pl.ds(pl.multiple_of(offset, ROW_TILE), ROW_TILE)

    def in_offset(out_offset):
        span = ROW_TILE * n_tok
        if span & (span - 1) == 0:
            return out_offset & (span - 1)
        return lax.rem(out_offset, span)

    def gather_start(blk, sl, rows):
        first = base_ref[blk]
        for r in rows:
            src = tiles_at(in_offset(src_ref[first + r]))
            pltpu.make_async_copy(h_hbm.at[src, :], xbuf.at[sl, tile(r), :], gsem.at[sl]).start()

    def scatter_start(blk, sl, rows, real, priorities):
        first = base_ref[blk]
        valid = jnp.where(real, nvalid_ref[blk], 0)
        spare0 = spare_row(jnp.where(real, lax.rem(blk, ring), ring - 1), 0) * ROW_TILE
        for r in rows:
            dst = tiles_at(jnp.where(r < valid, src_ref[first + r], spare0 + r * ROW_TILE))
            pltpu.make_async_copy(ybuf.at[sl, tile(r), :], y_hbm.at[dst, :],
                                  ssem.at[sl]).start(priority=r % priorities)

    def gather_wait(sl):
        pltpu.make_async_copy(h_hbm.at[pl.ds(0, bm * ROW_TILE), :], xbuf.at[sl], gsem.at[sl]).wait()

    def scatter_wait(sl):
        pltpu.make_async_copy(ybuf.at[sl], y_hbm.at[pl.ds(0, bm * ROW_TILE), :], ssem.at[sl]).wait()

    def spare_row(set_id, r):
        return TOP_K * n_tok + set_id * bm + r

    @pl.when(i < nact)
    def _():
        @pl.when(i == 0)
        def _():
            for blk in range(ring - 1):
                gather_start(min(blk, n_blocks - 1), blk, range(bm))
            ybuf[...] = jnp.zeros_like(ybuf)
            for sl in range(ring):
                spare = pltpu.make_async_copy(
                    ybuf.at[sl], y_hbm.at[pl.ds(spare_row(sl, 0) * ROW_TILE, bm * ROW_TILE), :], ssem.at[sl])
                spare.start()
                spare.wait()

        def weight_copies(e):
            return [pltpu.make_async_copy(w_hbm.at[e], stage, wsem.at[k])
                    for k, (w_hbm, stage) in enumerate(((w1_hbm, wf1), (w3_hbm, wf3), (w2_hbm, wf2)))]

        this_e = blk_e_ref[i]

        @pl.when(i == 0)
        def _():
            for cp in weight_copies(this_e):
                cp.start()

        @pl.when((i == 0) | (this_e != blk_e_ref[jnp.maximum(i - 1, 0)]))
        def _():
            for cp in weight_copies(this_e):
                cp.wait()
            for stage, wb in ((wf1, w1b), (wf3, w3b), (wf2, w2b)):
                def cast_rows(rb, carry, stage=stage, wb=wb):
                    rows = pl.ds(pl.multiple_of(rb * LANES, LANES), LANES)
                    wb[rows, :] = stage[rows, :].astype(BF16)
                    return carry
                lax.fori_loop(0, stage.shape[0] // LANES, cast_rows, 0)
            nxt_blk = nxt_ref[i]

            @pl.when(nxt_blk < nact)
            def _():
                for cp in weight_copies(blk_e_ref[nxt_blk]):
                    cp.start(priority=1)

        ahead = jnp.minimum(i + ring - 1, n_blocks - 1)
        prv = jnp.maximum(i - 1, 0)

        def compute_block(slot):
            far_slot = (slot - 1) % ring
            others = [(slot - 1 - k) % ring for k in range(ring - 1)]
            gather_wait(slot)
            xb = _load_row_tiles(xbuf.at[slot], bm).astype(BF16)
            acts = []
            for c in range(n_chunks):
                cols = slice(c * chunk, (c + 1) * chunk)
                a = _dot(xb, w1b[:, cols])
                g = _dot(xb, w3b[:, cols])
                acts.append((_silu(a) * g).astype(BF16))
                if c < n_chunks // 2:
                    gather_start(ahead, far_slot, range(c * rows_per_chunk, (c + 1) * rows_per_chunk))
                else:
                    first = (c - n_chunks // 2) * rows_per_chunk
                    scatter_start(prv, far_slot, range(first, first + rows_per_chunk), i > 0, 2)

            y = _dot(jnp.concatenate(acts, axis=1), w2b[...])

            @pl.when(i >= ring - 1)
            def _():
                scatter_wait(slot)

            _store_row_tiles(ybuf.at[slot], y)

            @pl.when(i == nact - 1)
            def _():
                scatter_start(i, slot, range(bm), True, 1)
                for k, sl in enumerate(others):
                    pl.when(i >= k)(functools.partial(scatter_wait, sl))
                    gather_wait(sl)
                scatter_wait(slot)

        for parity in range(ring):
            pl.when(lax.rem(i, ring) == parity)(functools.partial(compute_block, parity))


def _moe(blk_e, nxt_blk, base, nvalid, nact, src, w1, w3, w2, h2_tiles, n_blocks):
    t = h2_tiles.shape[0] // ROW_TILE
    d = w1.shape[1]
    de = w1.shape[2]
    bm = MOE_BLOCK
    hbm = pl.BlockSpec(memory_space=pl.ANY)
    grid_spec = pltpu.PrefetchScalarGridSpec(
        num_scalar_prefetch=6,
        grid=(n_blocks,),
        in_specs=[hbm, hbm, hbm, hbm],
        out_specs=hbm,
        scratch_shapes=[pltpu.VMEM((d, de), F32), pltpu.VMEM((d, de), F32), pltpu.VMEM((de, d), F32),
                        pltpu.VMEM((d, de), BF16), pltpu.VMEM((d, de), BF16), pltpu.VMEM((de, d), BF16),
                        pltpu.VMEM((MOE_RING, bm * ROW_TILE, LANES), F32),
                        pltpu.VMEM((MOE_RING, bm * ROW_TILE, LANES), F32),
                        pltpu.SemaphoreType.DMA((MOE_RING,)), pltpu.SemaphoreType.DMA((MOE_RING,)),
                        pltpu.SemaphoreType.DMA((3,))],
    )
    return pl.pallas_call(
        functools.partial(_moe_kernel, n_tok=t),
        out_shape=jax.ShapeDtypeStruct(((TOP_K * t + MOE_RING * bm) * ROW_TILE, LANES), F32),
        grid_spec=grid_spec,
        compiler_params=_cparams(("arbitrary",)),
        name="moe_experts",
    )(blk_e, nxt_blk, base, nvalid, nact, src, w1, w3, w2, h2_tiles)


def _final_kernel(x1_ref, y0_ref, y1_ref, gt_ref, g2_ref, fg_ref, o_ref):
    gt = gt_ref[...]
    tm = x1_ref.shape[0]
    y2 = gt[:, 0:1] * _load_row_tiles(y0_ref, tm) + gt[:, 1:2] * _load_row_tiles(y1_ref, tm)
    x = x1_ref[...] + g2_ref[0] * y2
    o_ref[...] = x * lax.rsqrt(jnp.mean(x * x, axis=-1, keepdims=True) + EPS) * fg_ref[...]


def _final(x1, y, gates_tok, g2, fg, rows_per_mod, tm):
    t, d = x1.shape
    nb = t // tm
    blocks_per_mod = rows_per_mod // tm
    return pl.pallas_call(
        _final_kernel,
        out_shape=jax.ShapeDtypeStruct((t, d), F32),
        grid=(nb,),
        in_specs=[pl.BlockSpec((tm, d), lambda i: (i, 0)),
                  pl.BlockSpec((tm * ROW_TILE, LANES), lambda i: (i, 0)),
                  pl.BlockSpec((tm * ROW_TILE, LANES), lambda i: (i + nb, 0)),
                  pl.BlockSpec((tm, TOP_K), lambda i: (i, 0)),
                  pl.BlockSpec((1, 1, d), lambda i: (i // blocks_per_mod, 0, 0)),
                  pl.BlockSpec((1, d), lambda i: (0, 0))],
        out_specs=pl.BlockSpec((tm, d), lambda i: (i, 0)),
        compiler_params=_cparams(("arbitrary",)),
        name="combine_final_norm",
    )(x1, y, y, gates_tok, g2, fg)


def kernel(x, c, ctx, c_ctx, w_ada, b_ada, norm1_g, norm2_g, w_in, conv_dw, conv_b, conv_ln_g, conv_ln_b,
           lru_conv_w, lru_conv_b, lru_wa, lru_ba, lru_wx, lru_bx, lru_lam, w_out,
           router_wg, router_bg, router_we, router_be, w1, w3, w2, final_g):
    assert w_ada.shape[0] == 1, "single-layer block"
    assert x.shape[2] == ROW_TILE * LANES, "row-tile layout: one (ROW_TILE, LANES) tile per token row"
    b, s, d = x.shape
    n_ctx = ctx.shape[1]
    t = b * s
    cc = conv_dw.shape[2]
    lw = lru_conv_w.shape[2]

    c_rows = jnp.zeros((SUBLANES, d), F32).at[:b].set(c).at[b].set(c_ctx)
    mod = _ada(c_rows, w_ada[0], b_ada)
    mod_l = mod[:b].reshape(b, 6, 1, d)
    sh1, sc1, g1, sh2, sc2, g2 = (mod_l[:, k] for k in range(6))
    mod_c = mod[b].reshape(6, 1, 1, d)
    csh1, csc1 = mod_c[0], mod_c[1]

    w_in_b = w_in[0].astype(BF16)
    w_out_b = w_out[0].astype(BF16)
    heads_per_blk = LRU_LANES // lru_wa.shape[3]
    n_cblk = lw // LRU_LANES

    def blockdiag(wh):
        hd = wh.shape[1]
        wh = wh.reshape(n_cblk, heads_per_blk, hd, hd)
        eye = jnp.eye(heads_per_blk, dtype=wh.dtype)
        return jnp.einsum("chij,hg->chigj", wh, eye).reshape(n_cblk, LRU_LANES, LRU_LANES)

    wg = jnp.concatenate([blockdiag(lru_wa[0, 0]), blockdiag(lru_wx[0, 0]),
                          blockdiag(lru_wa[0, 1]), blockdiag(lru_wx[0, 1])], axis=2).astype(BF16)
    gb = 0.5 * jnp.stack([lru_ba[0, 0], lru_bx[0, 0], lru_ba[0, 1], lru_bx[0, 1]])
    lam = lru_lam[0]

    zc = _inproj_ctx(ctx.reshape(b * n_ctx, d), csh1, csc1, norm1_g, w_in_b[:, 2 * cc:2 * cc + lw], n_ctx)
    h0 = _lru(zc.reshape(b, n_ctx, lw), None, jnp.zeros((2, b, 1, lw), F32),
              lru_conv_w[0], lru_conv_b, wg, gb, lam, True, "rglru_ctx")

    x2 = x.reshape(t, d)
    conv_w_rep = jnp.repeat(conv_dw[0], SUBLANES, axis=0).reshape(CONV_TAPS * SUBLANES, cc // LANES, LANES)
    conv_w_rep = conv_w_rep.transpose(1, 0, 2)
    conv_l, zl, gg = _inproj(x2, sh1, sc1, norm1_g, w_in_b, conv_w_rep, conv_b, conv_ln_g, conv_ln_b, s)
    lru_l = _lru(zl.reshape(b, s, lw), gg.reshape(b, s, lw), h0,
                 lru_conv_w[0], lru_conv_b, wg, gb, lam, False, "rglru")

    wr = jnp.zeros((ROUTE_ROWS, d), F32)
    wr = wr.at[:N_GROUPS].set(router_wg[0].T)
    wr = wr.at[EXPERT_ROW0:EXPERT_ROW0 + N_EXPERTS].set(router_we[0].reshape(d, N_EXPERTS).T)
    rb = jnp.zeros((ROUTE_ROWS, 1), F32)
    rb = rb.at[:N_GROUPS, 0].set(router_bg[0])
    rb = rb.at[EXPERT_ROW0:EXPERT_ROW0 + N_EXPERTS, 0].set(router_be[0].reshape(-1))
    r_hi, r_lo = _split_bf16(wr)
    x1, h2, logits_t = _outproj(conv_l, lru_l.reshape(t, lw), x2, g1, sh2, sc2, norm2_g, w_out_b,
                                r_hi, jnp.concatenate([r_hi, r_lo], axis=0), rb, s, TOKEN_BLOCK)

    gates, eid, counts = _route(logits_t)
    n_blocks = (TOP_K * t) // MOE_BLOCK + N_EXPERTS
    src = jnp.pad(_sort_assignments(eid.reshape(TOP_K * t // LANES, LANES)).reshape(-1), (0, MOE_BLOCK))
    blk_e, nxt_blk, base, nvalid, nact = _tables(counts[:, 0].astype(jnp.int32), n_blocks, MOE_BLOCK)
    y = _moe(blk_e, nxt_blk, base, nvalid, nact, src, w1[0], w3[0], w2[0], h2, n_blocks)
    out = _final(x1, y, gates.T, g2, final_g.reshape(1, d), s, TOKEN_BLOCK)
    return out.reshape(b, s, d)
```

```python
import functools

import jax
import jax.numpy as jnp
from jax import lax
from jax.experimental import pallas as pl
from jax.experimental.pallas import tpu as pltpu

F32 = jnp.float32
BF16 = jnp.bfloat16

EPS = 1e-6
CONV_TAPS = 31
LRU_TAPS = 4
LRU_C = 8.0
GRID_W = 64
N_GROUPS = 4
EXPERTS_PER_GROUP = 8
N_EXPERTS = N_GROUPS * EXPERTS_PER_GROUP
TOP_K = 2

SUBLANES = 8
LANES = 128
TOKEN_BLOCK = 1024
LRU_LANES = 128
EXPERT_ROW0 = 8
ROUTE_ROWS = EXPERT_ROW0 + N_EXPERTS
MOE_BLOCK = 256
MOE_RING = 3
STREAM_RING = 3
ROUTE_LANES = 2048
ADA_COLS = 1536
V7X_VMEM_BYTES = 64 * 1024 * 1024
VMEM_LIMIT = 3 * V7X_VMEM_BYTES // 4


def _cparams(sem, vmem=VMEM_LIMIT):
    return pltpu.CompilerParams(dimension_semantics=sem, vmem_limit_bytes=vmem)


def _split_bf16(a):
    hi = a.astype(BF16)
    lo = (a - hi.astype(F32)).astype(BF16)
    return hi, lo


def _dot(a, b):
    return jnp.dot(a, b, preferred_element_type=F32)


def _dot_nt(a, b):
    return lax.dot_general(a, b, (((1,), (1,)), ((), ())), preferred_element_type=F32)


def _times_sigmoid_of_halves(hv, hx):
    return hv * jnp.tanh(hx) + hv


def _silu(x):
    h = 0.5 * x
    return h * jnp.tanh(h) + h


def _gelu_tanh(x):
    c = 0.7978845608028654
    h = 0.5 * x
    return h * jnp.tanh(x * ((c * 0.044715) * (x * x) + c)) + h


ROW_TILE = SUBLANES


def _store_row_tiles(ref, x):
    rows = x.shape[0]
    for s in range(ROW_TILE):
        ref[pl.ds(s, rows, stride=ROW_TILE), :] = x[:, s * LANES:(s + 1) * LANES]


def _load_row_tiles(ref, rows):
    return jnp.concatenate([ref[pl.ds(s, rows, stride=ROW_TILE), :] for s in range(ROW_TILE)], axis=1)


def _rms_mod(x, g, shift, scale):
    y = x * lax.rsqrt(jnp.mean(x * x, axis=-1, keepdims=True) + EPS)
    return y * (g * (1.0 + scale)) + shift


def _ada_kernel(c_ref, w_ref, b_ref, o_ref):
    a = _silu(c_ref[...])
    a_hi, a_lo = _split_bf16(a)
    w_hi, w_lo = _split_bf16(w_ref[...])
    o_ref[...] = _dot(a_hi, w_hi) + _dot(a_lo, w_hi) + _dot(a_hi, w_lo) + b_ref[...]


def _ada(c_rows, w, b):
    m, d = c_rows.shape
    n = w.shape[1]
    bn = ADA_COLS
    return pl.pallas_call(
        _ada_kernel,
        out_shape=jax.ShapeDtypeStruct((m, n), F32),
        grid=(n // bn,),
        in_specs=[pl.BlockSpec((m, d), lambda j: (0, 0)),
                  pl.BlockSpec((d, bn), lambda j: (0, j)),
                  pl.BlockSpec((1, bn), lambda j: (0, j))],
        out_specs=pl.BlockSpec((m, bn), lambda j: (0, j)),
        compiler_params=_cparams(("arbitrary",)),
        name="ada_mod",
    )(c_rows, w, b)


def _inproj_ctx_kernel(x_ref, sh_ref, sc_ref, g_ref, w_ref, o_ref):
    h = _rms_mod(x_ref[...], g_ref[...], sh_ref[0], sc_ref[0])
    o_ref[...] = _dot(h.astype(BF16), w_ref[...])


def _conformer_conv(u, cw_ref, cb_ref, lg_ref, lb_ref, o_ref, stg_ref, xt_ref, ot_ref):
    c = u.shape[1]
    slabs = range(c // LANES)
    half = CONV_TAPS // 2
    pitch = GRID_W + SUBLANES
    group = SUBLANES
    for l in slabs:
        for q in range(SUBLANES):
            stg_ref[l, q * pitch:q * pitch + GRID_W, :] = u[q * GRID_W:(q + 1) * GRID_W, l * LANES:(l + 1) * LANES]

    def conv_slab(l, carry):
        for t in range(GRID_W):
            xt_ref[l, t * SUBLANES:(t + 1) * SUBLANES, :] = stg_ref[l, pl.ds(t, SUBLANES, stride=pitch), :]
        for t0 in range(0, GRID_W, group):
            accs = [None] * group
            for k in range(CONV_TAPS):
                srcs = [t0 + j + k - half for j in range(group)]
                if not any(0 <= sidx < GRID_W for sidx in srcs):
                    continue
                wk = cw_ref[l, k * SUBLANES:(k + 1) * SUBLANES, :]
                for j, sidx in enumerate(srcs):
                    if 0 <= sidx < GRID_W:
                        term = wk * xt_ref[l, sidx * SUBLANES:(sidx + 1) * SUBLANES, :]
                        accs[j] = term if accs[j] is None else accs[j] + term
            for j in range(group):
                ot_ref[l, (t0 + j) * SUBLANES:(t0 + j + 1) * SUBLANES, :] = accs[j]
        return carry
    lax.fori_loop(0, c // LANES, conv_slab, 0)
    rows_per_pass = group * SUBLANES
    for r0 in range(0, GRID_W * SUBLANES, rows_per_pass):
        rows = slice(r0, r0 + rows_per_pass)
        acc = jnp.concatenate([ot_ref[l, rows, :] for l in slabs], axis=1) + cb_ref[...]
        mu = jnp.mean(acc, axis=-1, keepdims=True)
        cen = acc - mu
        var = jnp.mean(cen * cen, axis=-1, keepdims=True)
        y = _silu(cen * lax.rsqrt(var + EPS) * lg_ref[...] + lb_ref[...])
        for l in slabs:
            ot_ref[l, rows, :] = y[:, l * LANES:(l + 1) * LANES]
    for q in range(SUBLANES):
        o_ref[q * GRID_W:(q + 1) * GRID_W, :] = jnp.concatenate(
            [ot_ref[l, pl.ds(q, GRID_W, stride=SUBLANES), :] for l in slabs], axis=1).astype(o_ref.dtype)


def _inproj_kernel(x_ref, sh_ref, sc_ref, g_ref, w_ref, cw_ref, cb_ref, lg_ref, lb_ref,
                   cv_ref, zl_ref, gg_ref, stg_ref, xt_ref, ot_ref):
    h = _rms_mod(x_ref[...], g_ref[...], sh_ref[0], sc_ref[0])
    z = _dot(h.astype(BF16), w_ref[...])
    c = cv_ref.shape[1]
    zl_ref[...] = z[:, 2 * c:3 * c]
    gg_ref[...] = _gelu_tanh(z[:, 3 * c:])
    u = _times_sigmoid_of_halves(z[:, :c], z[:, c:2 * c])
    _conformer_conv(u, cw_ref, cb_ref, lg_ref, lb_ref, cv_ref, stg_ref, xt_ref, ot_ref)


def _inproj_ctx(x2, shift, scale, g, w, tm):
    t, d = x2.shape
    n = w.shape[1]
    mod_spec = pl.BlockSpec((1, 1, d), lambda i: (0, 0, 0))
    return pl.pallas_call(
        _inproj_ctx_kernel,
        out_shape=jax.ShapeDtypeStruct((t, n), F32),
        grid=(t // tm,),
        in_specs=[pl.BlockSpec((tm, d), lambda i: (i, 0)), mod_spec, mod_spec,
                  pl.BlockSpec((1, d), lambda i: (0, 0)),
                  pl.BlockSpec((d, n), lambda i: (0, 0))],
        out_specs=pl.BlockSpec((tm, n), lambda i: (i, 0)),
        compiler_params=_cparams(("arbitrary",)),
        name="in_proj_ctx",
    )(x2, shift, scale, g, w)


def _inproj(x2, shift, scale, g, w, cw, cb, lg, lb, rows_per_mod):
    t, d = x2.shape
    n = w.shape[1]
    c = n // 4
    tm = SUBLANES * GRID_W
    blocks_per_mod = rows_per_mod // tm
    mod_spec = pl.BlockSpec((1, 1, d), lambda i: (i // blocks_per_mod, 0, 0))
    const = lambda a: pl.BlockSpec(a.shape, lambda i: (0,) * a.ndim)
    rows = pl.BlockSpec((tm, c), lambda i: (i, 0))
    return pl.pallas_call(
        _inproj_kernel,
        out_shape=[jax.ShapeDtypeStruct((t, c), BF16), jax.ShapeDtypeStruct((t, c), F32),
                   jax.ShapeDtypeStruct((t, c), F32)],
        grid=(t // tm,),
        in_specs=[pl.BlockSpec((tm, d), lambda i: (i, 0)), mod_spec, mod_spec, const(g), const(w),
                  const(cw), const(cb), const(lg), const(lb)],
        out_specs=[rows, rows, rows],
        scratch_shapes=[pltpu.VMEM((c // LANES, SUBLANES * (GRID_W + SUBLANES), LANES), F32),
                        pltpu.VMEM((c // LANES, tm, LANES), F32), pltpu.VMEM((c // LANES, tm, LANES), F32)],
        compiler_params=_cparams(("arbitrary",)),
        name="in_proj",
    )(x2, shift, scale, g, w, cw, cb, lg, lb)


def _lru_kernel(zl_ref, h0_ref, cw_ref, cb_ref, wg_ref, gb_ref, lam_ref, *rest, seq, final_only):
    if final_only:
        o_ref, stage_ref, zt_ref, af_ref, bf_ref, ab_ref, bb_ref, pf_ref, hf_ref, pb_ref, hb_ref = rest
        gg_ref = None
    else:
        gg_ref, o_ref, stage_ref, zt_ref, af_ref, bf_ref, ab_ref, bb_ref, pf_ref, hf_ref, pb_ref, hb_ref = rest
    cl = seq // SUBLANES
    pitch = cl + SUBLANES
    c = zl_ref.shape[2]
    a_refs = (af_ref, ab_ref)
    b_refs = (bf_ref, bb_ref)
    halo = 2

    def tile_rows(t):
        return pl.ds(pl.multiple_of(t * SUBLANES, SUBLANES), SUBLANES)

    for j in range(SUBLANES):
        stage_ref[j * pitch:j * pitch + cl, :] = zl_ref[0, j * cl:(j + 1) * cl, :]

    def to_chunk_layout(t, carry):
        zt_ref[tile_rows(t + halo), :] = stage_ref[pl.ds(t, SUBLANES, stride=pitch), :]
        return carry
    lax.fori_loop(0, cl, to_chunk_layout, 0, unroll=8)

    sub = lax.broadcasted_iota(jnp.int32, (SUBLANES, c), 0)
    for t_src, t_dst in ((cl - 2, -2), (cl - 1, -1)):
        v = pltpu.roll(zt_ref[(t_src + halo) * SUBLANES:(t_src + halo + 1) * SUBLANES, :], 1, 0)
        zt_ref[(t_dst + halo) * SUBLANES:(t_dst + halo + 1) * SUBLANES, :] = jnp.where(sub == 0, 0.0, v)
    v = pltpu.roll(zt_ref[halo * SUBLANES:(halo + 1) * SUBLANES, :], SUBLANES - 1, 0)
    zt_ref[(cl + halo) * SUBLANES:(cl + halo + 1) * SUBLANES, :] = jnp.where(sub == SUBLANES - 1, 0.0, v)

    lam = lam_ref[...]
    nlam = -lam
    softplus = jnp.maximum(nlam, 0.0) + jnp.log1p(jnp.exp(-jnp.abs(nlam)))
    half_decay = (-0.5 * LRU_C) * softplus
    cw_half = 0.5 * cw_ref[...]
    cb_half = 0.5 * cb_ref[...]
    piece = cl

    gb_row = jnp.concatenate([gb_ref[k:k + 1, :] for k in range(4)], axis=1)
    gb_hi = gb_row.astype(BF16).astype(F32)
    gb_mid = (gb_row - gb_hi).astype(BF16).astype(F32)
    gb_lo = ((gb_row - gb_hi) - gb_mid).astype(BF16).astype(F32)
    row = lax.broadcasted_iota(jnp.int32, (c, 4 * c), 0)
    bias_rows = jnp.where(row == 0, gb_hi, jnp.where(row == 1, gb_mid, jnp.where(row == 2, gb_lo, 0.0)))
    wg = jnp.concatenate([wg_ref[0], bias_rows.astype(BF16)], axis=0)
    ones_cols = (lax.broadcasted_iota(jnp.int32, (piece, c), 1) < 3).astype(BF16)

    for p0 in range(0, cl * SUBLANES, piece):
        ul_half = cb_half + jnp.zeros((piece, c), F32)
        for k in range(LRU_TAPS):
            off = p0 + k * SUBLANES
            ul_half = ul_half + cw_half[k:k + 1, :] * zt_ref[off:off + piece, :]
        g = _dot(jnp.concatenate([ul_half.astype(BF16), ones_cols], axis=1), wg)
        for d in range(2):
            t_r = jnp.tanh(g[:, (2 * d) * c:(2 * d + 1) * c])
            t_i = jnp.tanh(g[:, (2 * d + 1) * c:(2 * d + 2) * c])
            log_a = half_decay[d:d + 1, :] * t_r + half_decay[d:d + 1, :]
            a = jnp.exp(log_a)
            m = jnp.maximum(jnp.tanh(log_a) * (-1.0 - a * a), 1e-12)
            mult = m * lax.rsqrt(m)
            a_refs[d][p0:p0 + piece, :] = a
            b_refs[d][p0:p0 + piece, :] = mult * (t_i * ul_half + ul_half)

    def two_steps(a_ref, b_ref, p_out, h_out, s0, s1, h, p):
        a0 = a_ref[s0, :]
        a1 = a_ref[s1, :]
        b0 = b_ref[s0, :]
        a10 = a1 * a0
        b10 = a1 * b0 + b_ref[s1, :]
        h_out[s0, :] = a0 * h + b0
        p_out[s0, :] = a0 * p
        h = a10 * h + b10
        p = a10 * p
        h_out[s1, :] = h
        p_out[s1, :] = p
        return h, p

    group = 8
    group_rows = group * SUBLANES

    def step(n, carry):
        hf, pf, hb, pb = carry
        base_f = pl.multiple_of(n * group_rows, group_rows)
        base_b = pl.multiple_of((cl // group - 1 - n) * group_rows, group_rows)
        for k in range(0, group, 2):
            tf0 = pl.ds(base_f + k * SUBLANES, SUBLANES)
            tf1 = pl.ds(base_f + (k + 1) * SUBLANES, SUBLANES)
            hf, pf = two_steps(af_ref, bf_ref, pf_ref, hf_ref, tf0, tf1, hf, pf)
            tb0 = pl.ds(base_b + (group - 1 - k) * SUBLANES, SUBLANES)
            tb1 = pl.ds(base_b + (group - 2 - k) * SUBLANES, SUBLANES)
            hb, pb = two_steps(ab_ref, bb_ref, pb_ref, hb_ref, tb0, tb1, hb, pb)
        return hf, pf, hb, pb

    zero = jnp.zeros((SUBLANES, c), F32)
    one = jnp.ones((SUBLANES, c), F32)
    lax.fori_loop(0, cl // group, step, (zero, one, zero, one))

    cf = [None] * SUBLANES
    cbk = [None] * SUBLANES
    s = h0_ref[0, 0]
    for j in range(SUBLANES):
        cf[j] = s
        last = (cl - 1) * SUBLANES + j
        s = hf_ref[last:last + 1, :] + pf_ref[last:last + 1, :] * s
    final_f = s
    s = h0_ref[1, 0]
    for j in reversed(range(SUBLANES)):
        cbk[j] = s
        s = hb_ref[j:j + 1, :] + pb_ref[j:j + 1, :] * s
    final_b = s

    if final_only:
        o_ref[0, 0] = final_f
        o_ref[1, 0] = final_b
        return
    carry_f = jnp.concatenate(cf, axis=0)
    carry_b = jnp.concatenate(cbk, axis=0)

    def to_natural(t, carry):
        rows = tile_rows(t)
        h = (hf_ref[rows, :] + pf_ref[rows, :] * carry_f) + (hb_ref[rows, :] + pb_ref[rows, :] * carry_b)
        stage_ref[pl.ds(t, SUBLANES, stride=pitch), :] = h
        return carry
    lax.fori_loop(0, cl, to_natural, 0, unroll=8)
    for j in range(SUBLANES):
        h = stage_ref[j * pitch:j * pitch + cl, :]
        o_ref[0, j * cl:(j + 1) * cl, :] = (h * gg_ref[0, j * cl:(j + 1) * cl, :]).astype(o_ref.dtype)


def _lru(zl, gg, h0, cw, cb, wg, gb, lam, final_only, name):
    b, seq, c = zl.shape
    cbk = LRU_LANES
    cl = seq // SUBLANES
    pitch = cl + SUBLANES
    seq_spec = pl.BlockSpec((1, seq, cbk), lambda bi, ci: (bi, 0, ci))
    st_spec = pl.BlockSpec((2, 1, 1, cbk), lambda bi, ci: (0, bi, 0, ci))
    chan = lambda rows: pl.BlockSpec((rows, cbk), lambda bi, ci: (0, ci))
    in_specs = [seq_spec, st_spec, chan(LRU_TAPS), chan(1),
                pl.BlockSpec((1, cbk, 4 * cbk), lambda bi, ci: (ci, 0, 0)), chan(4), chan(2)]
    args = [zl, h0, cw, cb, wg, gb, lam]
    if final_only:
        out_shape = jax.ShapeDtypeStruct((2, b, 1, c), F32)
        out_spec = st_spec
    else:
        in_specs.append(seq_spec)
        args.append(gg)
        out_shape = jax.ShapeDtypeStruct((b, seq, c), BF16)
        out_spec = seq_spec
    coef = pltpu.VMEM((seq, cbk), F32)
    return pl.pallas_call(
        functools.partial(_lru_kernel, seq=seq, final_only=final_only),
        out_shape=out_shape,
        grid=(b, c // cbk),
        in_specs=in_specs,
        out_specs=out_spec,
        scratch_shapes=[pltpu.VMEM((SUBLANES * pitch, cbk), F32),
                        pltpu.VMEM((seq + 2 * SUBLANES * SUBLANES, cbk), F32)] + [coef] * 8,
        compiler_params=_cparams(("arbitrary", "arbitrary")),
        name=name,
    )(*args)


def _stream_rows(streams, sem):
    i = pl.program_id(0)
    n = pl.num_programs(0)

    def copies(step):
        slot = lax.rem(step, STREAM_RING)
        out = []
        for k, (src, buf) in enumerate(streams):
            rows = buf.shape[1]
            src_rows = pl.ds(pl.multiple_of(step * rows, rows), rows)
            out.append(pltpu.make_async_copy(src.at[src_rows], buf.at[slot], sem.at[k, slot]))
        return out

    @pl.when(i == 0)
    def _():
        for ahead in range(STREAM_RING - 1):
            for cp in copies(i + ahead):
                cp.start()

    @pl.when(i + STREAM_RING - 1 < n)
    def _():
        for cp in copies(i + STREAM_RING - 1):
            cp.start()

    for cp in copies(i):
        cp.wait()
    return lax.rem(i, STREAM_RING)


def _outproj_kernel(cv_hbm, lr_hbm, x_hbm, g1_ref, sh_ref, sc_ref, ng_ref, wo_ref, rh_ref, rc_ref, rb_ref,
                    x1_ref, h2_ref, lg_ref, cv_buf, lr_buf, x_buf, sem):
    c = cv_buf.shape[2]
    slot = _stream_rows(((cv_hbm, cv_buf), (lr_hbm, lr_buf), (x_hbm, x_buf)), sem)
    y = _dot(cv_buf[slot], wo_ref[0:c, :]) + _dot(lr_buf[slot], wo_ref[c:2 * c, :])
    x1 = x_buf[slot] + g1_ref[0] * y
    x1_ref[...] = x1
    h2 = _rms_mod(x1, ng_ref[...], sh_ref[0], sc_ref[0])
    h_hi, h_lo = _split_bf16(h2)
    _store_row_tiles(h2_ref, h2)
    rows = rh_ref.shape[0]
    both = _dot_nt(rc_ref[...], h_hi)
    lg_ref[...] = both[:rows] + both[rows:] + _dot_nt(rh_ref[...], h_lo) + rb_ref[...]


def _outproj(conv_l, lru_l, x2, g1, sh2, sc2, ng, wo, r_hi, r_both, r_b, rows_per_mod, tm):
    t, d = x2.shape
    c = conv_l.shape[1]
    blocks_per_mod = rows_per_mod // tm
    mod_spec = pl.BlockSpec((1, 1, d), lambda i: (i // blocks_per_mod, 0, 0))
    full = lambda a: pl.BlockSpec(a.shape, lambda i: (0, 0))
    hbm = pl.BlockSpec(memory_space=pl.ANY)
    return pl.pallas_call(
        _outproj_kernel,
        out_shape=[jax.ShapeDtypeStruct((t, d), F32), jax.ShapeDtypeStruct((t * ROW_TILE, LANES), F32),
                   jax.ShapeDtypeStruct((ROUTE_ROWS, t), F32)],
        grid=(t // tm,),
        in_specs=[hbm, hbm, hbm, mod_spec, mod_spec, mod_spec,
                  full(ng), full(wo), full(r_hi), full(r_both), full(r_b)],
        out_specs=[pl.BlockSpec((tm, d), lambda i: (i, 0)), pl.BlockSpec((tm * ROW_TILE, LANES), lambda i: (i, 0)),
                   pl.BlockSpec((ROUTE_ROWS, tm), lambda i: (0, i))],
        scratch_shapes=[pltpu.VMEM((STREAM_RING, tm, c), conv_l.dtype), pltpu.VMEM((STREAM_RING, tm, c), lru_l.dtype),
                        pltpu.VMEM((STREAM_RING, tm, d), F32), pltpu.SemaphoreType.DMA((3, STREAM_RING))],
        compiler_params=_cparams(("arbitrary",)),
        name="out_proj_router",
    )(conv_l, lru_l, x2, g1, sh2, sc2, ng, wo, r_hi, r_both, r_b)


def _route_chunk(lg_ref, lanes):
    n = lanes.size
    e = EXPERTS_PER_GROUP
    lgrp = lg_ref[0:N_GROUPS, lanes]
    gidx = lax.broadcasted_iota(jnp.int32, (N_GROUPS, n), 0)
    m = jnp.max(lgrp, axis=0, keepdims=True)
    ex = jnp.exp(lgrp - m)
    pg = ex / jnp.sum(ex, axis=0, keepdims=True)
    p_grp = jnp.max(pg, axis=0, keepdims=True)
    grp = jnp.min(jnp.where(pg == p_grp, gidx, N_GROUPS), axis=0, keepdims=True)
    le = jnp.zeros((e, n), F32)
    for g in range(N_GROUPS):
        rows = lg_ref[EXPERT_ROW0 + g * e:EXPERT_ROW0 + (g + 1) * e, lanes]
        le = jnp.where(grp == g, rows, le)
    m = jnp.max(le, axis=0, keepdims=True)
    ex = jnp.exp(le - m)
    pe = ex / jnp.sum(ex, axis=0, keepdims=True)
    eidx = lax.broadcasted_iota(jnp.int32, (e, n), 0)
    p1 = jnp.max(pe, axis=0, keepdims=True)
    i1 = jnp.min(jnp.where(pe == p1, eidx, e), axis=0, keepdims=True)
    pe2 = jnp.where(eidx == i1, -1.0, pe)
    p2 = jnp.max(pe2, axis=0, keepdims=True)
    i2 = jnp.min(jnp.where(pe2 == p2, eidx, e), axis=0, keepdims=True)
    denom = p1 + p2
    base = grp * e
    return (base + i1, base + i2), (p_grp * p1 / denom, p_grp * p2 / denom)


def _route_kernel(lg_ref, gate_ref, eid_ref, cnt_ref):
    tl = lg_ref.shape[1]
    chunk = 2 * LANES
    ne = N_EXPERTS

    @pl.when(pl.program_id(0) == 0)
    def _():
        cnt_ref[...] = jnp.zeros_like(cnt_ref)

    ones = jnp.ones((chunk, LANES), BF16)
    eidx = lax.broadcasted_iota(jnp.int32, (ne, chunk), 0)
    total = jnp.zeros((ne, LANES), F32)
    for cix in range(tl // chunk):
        lanes = pl.ds(cix * chunk, chunk)
        eids, gates = _route_chunk(lg_ref, lanes)
        for k in range(TOP_K):
            total = total + _dot((eidx == eids[k]).astype(BF16), ones)
        gate_ref[:, lanes] = jnp.concatenate(gates, axis=0)
        eid_ref[:, lanes] = jnp.concatenate(eids, axis=0)
    cnt_ref[...] = cnt_ref[...] + total


def _route(logits_t):
    rows, t = logits_t.shape
    tl = ROUTE_LANES
    blk = pl.BlockSpec((TOP_K, tl), lambda i: (0, i))
    return pl.pallas_call(
        _route_kernel,
        out_shape=[jax.ShapeDtypeStruct((TOP_K, t), F32), jax.ShapeDtypeStruct((TOP_K, t), jnp.int32),
                   jax.ShapeDtypeStruct((N_EXPERTS, LANES), F32)],
        grid=(t // tl,),
        in_specs=[pl.BlockSpec((rows, tl), lambda i: (0, i))],
        out_specs=[blk, blk, pl.BlockSpec((N_EXPERTS, LANES), lambda i: (0, 0))],
        compiler_params=_cparams(("arbitrary",)),
        name="route",
    )(logits_t)


def _index_bit(j, lane, sub, tile):
    lane_bits = LANES.bit_length() - 1
    tile_bits = lane_bits + SUBLANES.bit_length() - 1
    if j < lane_bits:
        return (lane >> j) & 1
    if j < tile_bits:
        return (sub >> (j - lane_bits)) & 1
    return (tile >> (j - tile_bits)) & 1


def _bitonic_sort(tiles):
    n_t = len(tiles)
    log_n = (n_t * SUBLANES * LANES).bit_length() - 1
    lane_bits = LANES.bit_length() - 1
    tile_bits = lane_bits + SUBLANES.bit_length() - 1
    lane = lax.broadcasted_iota(jnp.int32, (SUBLANES, LANES), 1)
    sub = lax.broadcasted_iota(jnp.int32, (SUBLANES, LANES), 0)
    for k in range(1, log_n + 1):
        for j in range(k - 1, -1, -1):
            new = []
            for v in range(n_t):
                x = tiles[v]
                bj = _index_bit(j, lane, sub, v)
                if j < lane_bits:
                    d = 1 << j
                    p = jnp.where(bj == 1, pltpu.roll(x, d, 1), pltpu.roll(x, LANES - d, 1))
                elif j < tile_bits:
                    m = 1 << (j - lane_bits)
                    p = jnp.where(bj == 1, pltpu.roll(x, m, 0), pltpu.roll(x, SUBLANES - m, 0))
                else:
                    p = tiles[v ^ (1 << (j - tile_bits))]
                bk = _index_bit(k, lane, sub, v) if k < log_n else 0
                take_min = bj == bk
                lo, hi = jnp.minimum(x, p), jnp.maximum(x, p)
                if isinstance(take_min, bool):
                    new.append(lo if take_min else hi)
                else:
                    new.append(jnp.where(take_min, lo, hi))
            tiles = new
    return tiles


def _sort_kernel(eid_ref, src_ref):
    n_t = eid_ref.shape[0] // SUBLANES
    n_bits = (eid_ref.shape[0] * LANES).bit_length() - 1
    lane = lax.broadcasted_iota(jnp.int32, (SUBLANES, LANES), 1)
    sub = lax.broadcasted_iota(jnp.int32, (SUBLANES, LANES), 0)
    tiles = []
    for v in range(n_t):
        n = (v * SUBLANES + sub) * LANES + lane
        tiles.append((eid_ref[v * SUBLANES:(v + 1) * SUBLANES, :] << n_bits) | n)
    tiles = _bitonic_sort(tiles)
    for v in range(n_t):
        src_ref[v * SUBLANES:(v + 1) * SUBLANES, :] = (tiles[v] & ((1 << n_bits) - 1)) * ROW_TILE


def _sort_assignments(eid_rows):
    n = eid_rows.shape[0] * eid_rows.shape[1]
    assert n & (n - 1) == 0 and (N_EXPERTS * n) < 2 ** 31, "keys expert*n + assignment must fit an int32"
    return pl.pallas_call(
        _sort_kernel,
        out_shape=jax.ShapeDtypeStruct(eid_rows.shape, jnp.int32),
        name="dispatch_sort",
    )(eid_rows)


def _tables_kernel(cnt_ref, blk_e_ref, nxt_ref, base_ref, nvalid_ref, nact_ref, *, bm):
    n_blocks = blk_e_ref.shape[0]
    shift = bm.bit_length() - 1

    def per_expert(ex, carry):
        blk0, start = carry
        cnt = cnt_ref[ex]
        end_blk = blk0 + ((cnt + (bm - 1)) >> shift)

        def fill_blk(b, c):
            done = (b - blk0) << shift
            blk_e_ref[b] = ex
            nxt_ref[b] = end_blk
            base_ref[b] = start + done
            nvalid_ref[b] = jnp.minimum(cnt - done, bm)
            return c
        lax.fori_loop(blk0, end_blk, fill_blk, 0)
        return end_blk, start + cnt
    nact, _ = lax.fori_loop(0, N_EXPERTS, per_expert, (0, 0))
    nact_ref[0] = nact

    def fill_idle(b, c):
        blk_e_ref[b] = N_EXPERTS - 1
        nxt_ref[b] = n_blocks
        base_ref[b] = 0
        nvalid_ref[b] = 0
        return c
    lax.fori_loop(nact, n_blocks, fill_idle, 0)


def _tables(counts, n_blocks, bm):
    smem = pl.BlockSpec(memory_space=pltpu.SMEM)
    blocks = jax.ShapeDtypeStruct((n_blocks,), jnp.int32)
    return pl.pallas_call(
        functools.partial(_tables_kernel, bm=bm),
        out_shape=[blocks, blocks, blocks, blocks, jax.ShapeDtypeStruct((1,), jnp.int32)],
        in_specs=[smem],
        out_specs=[smem] * 5,
        name="dispatch_tables",
    )(counts)


def _moe_kernel(blk_e_ref, nxt_ref, base_ref, nvalid_ref, nact_ref, src_ref,
                w1_hbm, w3_hbm, w2_hbm, h_hbm, y_hbm,
                wf1, wf3, wf2, w1b, w3b, w2b, xbuf, ybuf, gsem, ssem, wsem, *, n_tok):
    i = pl.program_id(0)
    nact = nact_ref[0]
    n_blocks = blk_e_ref.shape[0]
    bm = xbuf.shape[1] // ROW_TILE
    ring = xbuf.shape[0]
    de = w1b.shape[1]
    chunk = 2 * LANES
    n_chunks = de // chunk
    rows_per_chunk = 2 * bm // n_chunks

    def tile(row):
        return pl.ds(pl.multiple_of(row * ROW_TILE, ROW_TILE), ROW_TILE)

    def tiles_at(offset):
        return pl.ds(pl.multiple_of(offset, ROW_TILE), ROW_TILE)

    def in_offset(out_offset):
        span = ROW_TILE * n_tok
        if span & (span - 1) == 0:
            return out_offset & (span - 1)
        return lax.rem(out_offset, span)

    def gather_start(blk, sl, rows):
        first = base_ref[blk]
        for r in rows:
            src = tiles_at(in_offset(src_ref[first + r]))
            pltpu.make_async_copy(h_hbm.at[src, :], xbuf.at[sl, tile(r), :], gsem.at[sl]).start()

    def scatter_start(blk, sl, rows, real, priorities):
        first = base_ref[blk]
        valid = jnp.where(real, nvalid_ref[blk], 0)
        spare0 = spare_row(jnp.where(real, lax.rem(blk, ring), ring - 1), 0) * ROW_TILE
        for r in rows:
            dst = tiles_at(jnp.where(r < valid, src_ref[first + r], spare0 + r * ROW_TILE))
            pltpu.make_async_copy(ybuf.at[sl, tile(r), :], y_hbm.at[dst, :],
                                  ssem.at[sl]).start(priority=r % priorities)

    def gather_wait(sl):
        pltpu.make_async_copy(h_hbm.at[pl.ds(0, bm * ROW_TILE), :], xbuf.at[sl], gsem.at[sl]).wait()

    def scatter_wait(sl):
        pltpu.make_async_copy(ybuf.at[sl], y_hbm.at[pl.ds(0, bm * ROW_TILE), :], ssem.at[sl]).wait()

    def spare_row(set_id, r):
        return TOP_K * n_tok + set_id * bm + r

    @pl.when(i < nact)
    def _():
        @pl.when(i == 0)
        def _():
            for blk in range(ring - 1):
                gather_start(min(blk, n_blocks - 1), blk, range(bm))
            ybuf[...] = jnp.zeros_like(ybuf)
            for sl in range(ring):
                spare = pltpu.make_async_copy(
                    ybuf.at[sl], y_hbm.at[pl.ds(spare_row(sl, 0) * ROW_TILE, bm * ROW_TILE), :], ssem.at[sl])
                spare.start()
                spare.wait()

        def weight_copies(e):
            return [pltpu.make_async_copy(w_hbm.at[e], stage, wsem.at[k])
                    for k, (w_hbm, stage) in enumerate(((w1_hbm, wf1), (w3_hbm, wf3), (w2_hbm, wf2)))]

        this_e = blk_e_ref[i]

        @pl.when(i == 0)
        def _():
            for cp in weight_copies(this_e):
                cp.start()

        @pl.when((i == 0) | (this_e != blk_e_ref[jnp.maximum(i - 1, 0)]))
        def _():
            for cp in weight_copies(this_e):
                cp.wait()
            for stage, wb in ((wf1, w1b), (wf3, w3b), (wf2, w2b)):
                def cast_rows(rb, carry, stage=stage, wb=wb):
                    rows = pl.ds(pl.multiple_of(rb * LANES, LANES), LANES)
                    wb[rows, :] = stage[rows, :].astype(BF16)
                    return carry
                lax.fori_loop(0, stage.shape[0] // LANES, cast_rows, 0)
            nxt_blk = nxt_ref[i]

            @pl.when(nxt_blk < nact)
            def _():
                for cp in weight_copies(blk_e_ref[nxt_blk]):
                    cp.start(priority=1)

        ahead = jnp.minimum(i + ring - 1, n_blocks - 1)
        prv = jnp.maximum(i - 1, 0)

        def compute_block(slot):
            far_slot = (slot - 1) % ring
            others = [(slot - 1 - k) % ring for k in range(ring - 1)]
            gather_wait(slot)
            xb = _load_row_tiles(xbuf.at[slot], bm).astype(BF16)
            acts = []
            for c in range(n_chunks):
                cols = slice(c * chunk, (c + 1) * chunk)
                a = _dot(xb, w1b[:, cols])
                g = _dot(xb, w3b[:, cols])
                acts.append((_silu(a) * g).astype(BF16))
                if c < n_chunks // 2:
                    gather_start(ahead, far_slot, range(c * rows_per_chunk, (c + 1) * rows_per_chunk))
                else:
                    first = (c - n_chunks // 2) * rows_per_chunk
                    scatter_start(prv, far_slot, range(first, first + rows_per_chunk), i > 0, 2)

            y = _dot(jnp.concatenate(acts, axis=1), w2b[...])

            @pl.when(i >= ring - 1)
            def _():
                scatter_wait(slot)

            _store_row_tiles(ybuf.at[slot], y)

            @pl.when(i == nact - 1)
            def _():
                scatter_start(i, slot, range(bm), True, 1)
                for k, sl in enumerate(others):
                    pl.when(i >= k)(functools.partial(scatter_wait, sl))
                    gather_wait(sl)
                scatter_wait(slot)

        for parity in range(ring):
            pl.when(lax.rem(i, ring) == parity)(functools.partial(compute_block, parity))


def _moe(blk_e, nxt_blk, base, nvalid, nact, src, w1, w3, w2, h2_tiles, n_blocks):
    t = h2_tiles.shape[0] // ROW_TILE
    d = w1.shape[1]
    de = w1.shape[2]
    bm = MOE_BLOCK
    hbm = pl.BlockSpec(memory_space=pl.ANY)
    grid_spec = pltpu.PrefetchScalarGridSpec(
        num_scalar_prefetch=6,
        grid=(n_blocks,),
        in_specs=[hbm, hbm, hbm, hbm],
        out_specs=hbm,
        scratch_shapes=[pltpu.VMEM((d, de), F32), pltpu.VMEM((d, de), F32), pltpu.VMEM((de, d), F32),
                        pltpu.VMEM((d, de), BF16), pltpu.VMEM((d, de), BF16), pltpu.VMEM((de, d), BF16),
                        pltpu.VMEM((MOE_RING, bm * ROW_TILE, LANES), F32),
                        pltpu.VMEM((MOE_RING, bm * ROW_TILE, LANES), F32),
                        pltpu.SemaphoreType.DMA((MOE_RING,)), pltpu.SemaphoreType.DMA((MOE_RING,)),
                        pltpu.SemaphoreType.DMA((3,))],
    )
    return pl.pallas_call(
        functools.partial(_moe_kernel, n_tok=t),
        out_shape=jax.ShapeDtypeStruct(((TOP_K * t + MOE_RING * bm) * ROW_TILE, LANES), F32),
        grid_spec=grid_spec,
        compiler_params=_cparams(("arbitrary",)),
        name="moe_experts",
    )(blk_e, nxt_blk, base, nvalid, nact, src, w1, w3, w2, h2_tiles)


def _final_kernel(x1_ref, y0_ref, y1_ref, gt_ref, g2_ref, fg_ref, o_ref):
    gt = gt_ref[...]
    tm = x1_ref.shape[0]
    y2 = gt[:, 0:1] * _load_row_tiles(y0_ref, tm) + gt[:, 1:2] * _load_row_tiles(y1_ref, tm)
    x = x1_ref[...] + g2_ref[0] * y2
    o_ref[...] = x * lax.rsqrt(jnp.mean(x * x, axis=-1, keepdims=True) + EPS) * fg_ref[...]


def _final(x1, y, gates_tok, g2, fg, rows_per_mod, tm):
    t, d = x1.shape
    nb = t // tm
    blocks_per_mod = rows_per_mod // tm
    return pl.pallas_call(
        _final_kernel,
        out_shape=jax.ShapeDtypeStruct((t, d), F32),
        grid=(nb,),
        in_specs=[pl.BlockSpec((tm, d), lambda i: (i, 0)),
                  pl.BlockSpec((tm * ROW_TILE, LANES), lambda i: (i, 0)),
                  pl.BlockSpec((tm * ROW_TILE, LANES), lambda i: (i + nb, 0)),
                  pl.BlockSpec((tm, TOP_K), lambda i: (i, 0)),
                  pl.BlockSpec((1, 1, d), lambda i: (i // blocks_per_mod, 0, 0)),
                  pl.BlockSpec((1, d), lambda i: (0, 0))],
        out_specs=pl.BlockSpec((tm, d), lambda i: (i, 0)),
        compiler_params=_cparams(("arbitrary",)),
        name="combine_final_norm",
    )(x1, y, y, gates_tok, g2, fg)


def kernel(x, c, ctx, c_ctx, w_ada, b_ada, norm1_g, norm2_g, w_in, conv_dw, conv_b, conv_ln_g, conv_ln_b,
           lru_conv_w, lru_conv_b, lru_wa, lru_ba, lru_wx, lru_bx, lru_lam, w_out,
           router_wg, router_bg, router_we, router_be, w1, w3, w2, final_g):
    assert w_ada.shape[0] == 1, "single-layer block"
    assert x.shape[2] == ROW_TILE * LANES, "row-tile layout: one (ROW_TILE, LANES) tile per token row"
    b, s, d = x.shape
    n_ctx = ctx.shape[1]
    t = b * s
    cc = conv_dw.shape[2]
    lw = lru_conv_w.shape[2]

    c_rows = jnp.zeros((SUBLANES, d), F32).at[:b].set(c).at[b].set(c_ctx)
    mod = _ada(c_rows, w_ada[0], b_ada)
    mod_l = mod[:b].reshape(b, 6, 1, d)
    sh1, sc1, g1, sh2, sc2, g2 = (mod_l[:, k] for k in range(6))
    mod_c = mod[b].reshape(6, 1, 1, d)
    csh1, csc1 = mod_c[0], mod_c[1]

    glu_scale = jnp.where(jnp.arange(w_in.shape[2]) < 2 * cc, 0.5, 1.0).astype(F32)
    w_in_b = (w_in[0] * glu_scale).astype(BF16)
    w_out_b = w_out[0].astype(BF16)
    heads_per_blk = LRU_LANES // lru_wa.shape[3]
    n_cblk = lw // LRU_LANES

    def blockdiag(wh):
        hd = wh.shape[1]
        wh = wh.reshape(n_cblk, heads_per_blk, hd, hd)
        eye = jnp.eye(heads_per_blk, dtype=wh.dtype)
        return jnp.einsum("chij,hg->chigj", wh, eye).reshape(n_cblk, LRU_LANES, LRU_LANES)

    wg = jnp.concatenate([blockdiag(lru_wa[0, 0]), blockdiag(lru_wx[0, 0]),
                          blockdiag(lru_wa[0, 1]), blockdiag(lru_wx[0, 1])], axis=2).astype(BF16)
    gb = 0.5 * jnp.stack([lru_ba[0, 0], lru_bx[0, 0], lru_ba[0, 1], lru_bx[0, 1]])
    lam = lru_lam[0]

    zc = _inproj_ctx(ctx.reshape(b * n_ctx, d), csh1, csc1, norm1_g, w_in_b[:, 2 * cc:2 * cc + lw], n_ctx)
    h0 = _lru(zc.reshape(b, n_ctx, lw), None, jnp.zeros((2, b, 1, lw), F32),
              lru_conv_w[0], lru_conv_b, wg, gb, lam, True, "rglru_ctx")

    x2 = x.reshape(t, d)
    conv_w_rep = jnp.repeat(conv_dw[0], SUBLANES, axis=0).reshape(CONV_TAPS * SUBLANES, cc // LANES, LANES)
    conv_w_rep = conv_w_rep.transpose(1, 0, 2)
    conv_l, zl, gg = _inproj(x2, sh1, sc1, norm1_g, w_in_b, conv_w_rep, conv_b, conv_ln_g, conv_ln_b, s)
    lru_l = _lru(zl.reshape(b, s, lw), gg.reshape(b, s, lw), h0,
                 lru_conv_w[0], lru_conv_b, wg, gb, lam, False, "rglru")

    wr = jnp.zeros((ROUTE_ROWS, d), F32)
    wr = wr.at[:N_GROUPS].set(router_wg[0].T)
    wr = wr.at[EXPERT_ROW0:EXPERT_ROW0 + N_EXPERTS].set(router_we[0].reshape(d, N_EXPERTS).T)
    rb = jnp.zeros((ROUTE_ROWS, 1), F32)
    rb = rb.at[:N_GROUPS, 0].set(router_bg[0])
    rb = rb.at[EXPERT_ROW0:EXPERT_ROW0 + N_EXPERTS, 0].set(router_be[0].reshape(-1))
    r_hi, r_lo = _split_bf16(wr)
    x1, h2, logits_t = _outproj(conv_l, lru_l.reshape(t, lw), x2, g1, sh2, sc2, norm2_g, w_out_b,
                                r_hi, jnp.concatenate([r_hi, r_lo], axis=0), rb, s, TOKEN_BLOCK)

    gates, eid, counts = _route(logits_t)
    n_blocks = (TOP_K * t) // MOE_BLOCK + N_EXPERTS
    src = jnp.pad(_sort_assignments(eid.reshape(TOP_K * t // LANES, LANES)).reshape(-1), (0, MOE_BLOCK))
    blk_e, nxt_blk, base, nvalid, nact = _tables(counts[:, 0].astype(jnp.int32), n_blocks, MOE_BLOCK)
    y = _moe(blk_e, nxt_blk, base, nvalid, nact, src, w1[0], w3[0], w2[0], h2, n_blocks)
    out = _final(x1, y, gates.T, g2, final_g.reshape(1, d), s, TOKEN_BLOCK)
    return out.reshape(b, s, d)
```
